```python
import jax, jax.numpy as jnp
from jax import lax
import numpy as np

D_MODEL = 2048
BATCH = 2
SEQ = 4096
DEPTH = 2

CTX_LEN = 256
GRID_W = 64
HG_HEADS = 8
HG_KEY_DIM = 128
HG_VAL_DIM = 128
HG_KEY_WIDTH = HG_HEADS * HG_KEY_DIM
HG_WIDTH = HG_HEADS * HG_VAL_DIM
CONV_CH = D_MODEL - HG_WIDTH
CONV_GROUPS = 8
CONV_WIDTH = 31
CHUNK = 64
D_FF = 5632
FFN_CONV = 3
N_MOD = 6
EPS = 1e-6
LN_EPS = 1e-5
IN_SIZES = (HG_KEY_WIDTH, HG_KEY_WIDTH, HG_KEY_WIDTH, HG_WIDTH, HG_WIDTH, CONV_CH, CONV_CH)
IN_COLS = sum(IN_SIZES)

kernel_name = "hgrn2_conformer_hybrid_dit"


def rms_norm(x, w):
    xf = x.astype(jnp.float32)
    y = xf * lax.rsqrt(jnp.mean(xf * xf, axis=-1, keepdims=True) + EPS)
    return (y * w.astype(jnp.float32)).astype(x.dtype)


def modulate(h, shift, scale):
    return h * (1 + scale) + shift


def dwconv1d(x, w, b):
    k, ch = w.shape
    pad = k // 2
    y = lax.conv_general_dilated(x, w[:, None, :].astype(x.dtype), window_strides=(1,),
                                 padding=((pad, pad),), dimension_numbers=('NWC', 'WIO', 'NWC'),
                                 feature_group_count=ch)
    return y + b.astype(x.dtype)


def dwconv2d_grid(x, w, b):
    bsz, length, ch = x.shape
    rows = length // GRID_W
    kh, kw = w.shape[0], w.shape[1]
    xg = x.reshape(bsz, rows, GRID_W, ch)
    y = lax.conv_general_dilated(xg, w[:, :, None, :].astype(x.dtype), window_strides=(1, 1),
                                 padding=((kh // 2, kh // 2), (kw // 2, kw // 2)),
                                 dimension_numbers=('NHWC', 'HWIO', 'NHWC'), feature_group_count=ch)
    return y.reshape(bsz, length, ch) + b.astype(x.dtype)


def group_layer_norm(u, w, b):
    bsz, length, ch = u.shape
    ug = u.astype(jnp.float32).reshape(bsz, length, CONV_GROUPS, ch // CONV_GROUPS)
    mu = jnp.mean(ug, axis=-1, keepdims=True)
    d = ug - mu
    var = jnp.mean(d * d, axis=-1, keepdims=True)
    y = (d * lax.rsqrt(var + LN_EPS)).reshape(bsz, length, ch)
    return (y * w.astype(jnp.float32) + b.astype(jnp.float32)).astype(u.dtype)


def hgrn2_chunk_scan(q, logf, k, v, s0):
    bsz, length, nh, dk = q.shape
    dv = v.shape[-1]
    n = length // CHUNK
    mask = jnp.tril(jnp.ones((CHUNK, CHUNK), dtype=bool))[:, :, None]

    def to_chunks(t):
        return t.reshape(bsz, n, CHUNK, nh, t.shape[-1]).transpose(1, 0, 3, 2, 4)

    def step(state, xs):
        qc, lc, kc, vc = xs
        bcum = jnp.cumsum(lc, axis=-2)
        b_last = bcum[:, :, -1:, :]
        o_inter = jnp.einsum('bhck,bhkv->bhcv', qc * jnp.exp(bcum), state)
        diff = bcum[:, :, :, None, :] - bcum[:, :, None, :, :]
        decay = jnp.exp(jnp.where(mask, diff, -jnp.inf))
        att = jnp.einsum('bhtk,bhsk,bhtsk->bhts', qc, kc, decay)
        o = o_inter + jnp.einsum('bhts,bhsv->bhtv', att, vc)
        new_state = jnp.exp(b_last[:, :, 0, :])[..., None] * state + \
            jnp.einsum('bhck,bhcv->bhkv', kc * jnp.exp(b_last - bcum), vc)
        return new_state, o

    s_fin, o = lax.scan(step, s0, (to_chunks(q), to_chunks(logf), to_chunks(k), to_chunks(v)))
    o = o.transpose(1, 0, 3, 2, 4).reshape(bsz, length, nh, dv)
    return o, s_fin


def hgrn2_direction(q, f_raw, v, lb, s0, reverse):
    if reverse:
        q, f_raw, v = jnp.flip(q, 1), jnp.flip(f_raw, 1), jnp.flip(v, 1)
    lb32 = lb.reshape(HG_HEADS, HG_KEY_DIM).astype(jnp.float32)
    f = lb32 + (1.0 - lb32) * jax.nn.sigmoid(f_raw.astype(jnp.float32))
    o, s_fin = hgrn2_chunk_scan(q.astype(jnp.float32), jnp.log(f), 1.0 - f, v.astype(jnp.float32), s0)
    if reverse:
        o = jnp.flip(o, 1)
    return o, s_fin


def hgrn2_readout(o, g, w):
    o = o * lax.rsqrt(jnp.mean(o * o, axis=-1, keepdims=True) + EPS) * w.astype(jnp.float32)
    bsz, length = o.shape[0], o.shape[1]
    return (o.reshape(bsz, length, HG_WIDTH) * jax.nn.silu(g.astype(jnp.float32))).astype(g.dtype)


def conformer_conv(a, b, conv_w, conv_b, ln_w, ln_b):
    u = a * jax.nn.sigmoid(b)
    u = dwconv1d(u, conv_w, conv_b)
    u = group_layer_norm(u, ln_w, ln_b)
    return jax.nn.silu(u)


def split_columns(p):
    points = []
    s = 0
    for n in IN_SIZES[:-1]:
        s += n
        points.append(s)
    return jnp.split(p, points, axis=-1)


def project(h, w_in):
    q, ff, fb, v, g, a, b = split_columns(h @ w_in)
    bsz, length = h.shape[0], h.shape[1]
    hk = lambda t: t.reshape(bsz, length, HG_HEADS, HG_KEY_DIM)
    hv = lambda t: t.reshape(bsz, length, HG_HEADS, HG_VAL_DIM)
    return hk(jax.nn.silu(q)), hk(ff), hk(fb), hv(v), g, a, b


def token_mixers(h, hc, w_in, lb_f, lb_b, hg_norm_w, conv_w, conv_b, conv_ln_w, conv_ln_b, w_out, ctx_out):
    q, ff, fb, v, g, a, b = project(h, w_in)
    qc, ffc, fbc, vc, gc, ac, bc = project(hc, w_in)
    zero = jnp.zeros((h.shape[0], HG_HEADS, HG_KEY_DIM, HG_VAL_DIM), jnp.float32)
    oc_f, sc_f = hgrn2_direction(qc, ffc, vc, lb_f, zero, False)
    oc_b, sc_b = hgrn2_direction(qc, fbc, vc, lb_b, zero, True)
    o_f, _ = hgrn2_direction(q, ff, v, lb_f, sc_f, False)
    o_b, _ = hgrn2_direction(q, fb, v, lb_b, sc_b, True)
    y = jnp.concatenate([hgrn2_readout(o_f + o_b, g, hg_norm_w),
                         conformer_conv(a, b, conv_w, conv_b, conv_ln_w, conv_ln_b)], axis=-1) @ w_out
    if not ctx_out:
        return y, None
    yc = jnp.concatenate([hgrn2_readout(oc_f + oc_b, gc, hg_norm_w),
                          conformer_conv(ac, bc, conv_w, conv_b, conv_ln_w, conv_ln_b)], axis=-1) @ w_out
    return y, yc


def conv_ffn(h, w_up, cw, cb, w_down, grid):
    gate, val = jnp.split(h @ w_up, 2, axis=-1)
    if grid:
        gate = dwconv2d_grid(gate, cw, cb)
    else:
        gate = dwconv1d(gate, cw[FFN_CONV // 2], cb)
    return (jax.nn.gelu(gate, approximate=False) * val) @ w_down


def setup_inputs(seed: int = 0) -> dict:
    key = jax.random.key(seed)
    ks = jax.random.split(key, 20)
    nrm = lambda k, shape, s: jax.random.normal(k, shape, jnp.float32) * s
    D = D_MODEL
    return {
        "x": nrm(ks[0], (BATCH, SEQ, D), 1.0),
        "c": nrm(ks[1], (BATCH, D), 1.0),
        "ctx": nrm(ks[2], (BATCH, CTX_LEN, D), 1.0),
        "c_ctx": nrm(ks[3], (D,), 1.0),
        "w_mod": nrm(ks[4], (DEPTH, D, N_MOD * D), 0.5 * D ** -0.5),
        "b_mod": nrm(ks[5], (DEPTH, N_MOD * D), 0.02),
        "norm_w": 1.0 + nrm(ks[6], (DEPTH, 4, D), 0.02),
        "w_in": nrm(ks[7], (DEPTH, D, IN_COLS), D ** -0.5),
        "lb_logits": nrm(ks[8], (2, DEPTH, HG_KEY_WIDTH), 1.0),
        "hg_norm_w": 1.0 + nrm(ks[9], (DEPTH, HG_VAL_DIM), 0.02),
        "conv_w": nrm(ks[10], (DEPTH, CONV_WIDTH, CONV_CH), CONV_WIDTH ** -0.5),
        "conv_b": nrm(ks[11], (DEPTH, CONV_CH), 0.02),
        "conv_ln_w": 1.0 + nrm(ks[12], (DEPTH, CONV_CH), 0.02),
        "conv_ln_b": nrm(ks[13], (DEPTH, CONV_CH), 0.02),
        "w_out": nrm(ks[14], (DEPTH, D, D), D ** -0.5),
        "ffn_up": nrm(ks[15], (DEPTH, D, 2 * D_FF), D ** -0.5),
        "ffn_conv_w": nrm(ks[16], (DEPTH, FFN_CONV, FFN_CONV, D_FF), (FFN_CONV * FFN_CONV) ** -0.5),
        "ffn_conv_b": nrm(ks[17], (DEPTH, D_FF), 0.02),
        "ffn_down": nrm(ks[18], (DEPTH, D_FF, D), D_FF ** -0.5),
    }


def reference(x, c, ctx, c_ctx, w_mod, b_mod, norm_w, w_in, lb_logits, hg_norm_w, conv_w, conv_b,
              conv_ln_w, conv_ln_b, w_out, ffn_up, ffn_conv_w, ffn_conv_b, ffn_down):
    p = jax.nn.softmax(lb_logits.astype(jnp.float32), axis=1)
    lbs = jnp.cumsum(p, axis=1) - p[:, :1]
    silu_c = jax.nn.silu(c)
    silu_cc = jax.nn.silu(c_ctx)
    xc = ctx
    for l in range(DEPTH):
        ctx_out = l < DEPTH - 1
        mod = (silu_c @ w_mod[l] + b_mod[l])[:, None, :]
        modc = silu_cc @ w_mod[l] + b_mod[l]
        sh1, sc1, g1, sh2, sc2, g2 = jnp.split(mod, N_MOD, axis=-1)
        sh1c, sc1c, g1c, sh2c, sc2c, g2c = jnp.split(modc, N_MOD, axis=-1)
        h = modulate(rms_norm(x, norm_w[l, 0]), sh1, sc1)
        hc = modulate(rms_norm(xc, norm_w[l, 0]), sh1c, sc1c)
        y, yc = token_mixers(h, hc, w_in[l], lbs[0, l], lbs[1, l], hg_norm_w[l], conv_w[l], conv_b[l],
                             conv_ln_w[l], conv_ln_b[l], w_out[l], ctx_out)
        x = x + g1 * rms_norm(y, norm_w[l, 1])
        h2 = modulate(rms_norm(x, norm_w[l, 2]), sh2, sc2)
        x = x + g2 * rms_norm(conv_ffn(h2, ffn_up[l], ffn_conv_w[l], ffn_conv_b[l], ffn_down[l], True),
                              norm_w[l, 3])
        if ctx_out:
            xc = xc + g1c * rms_norm(yc, norm_w[l, 1])
            h2c = modulate(rms_norm(xc, norm_w[l, 2]), sh2c, sc2c)
            xc = xc + g2c * rms_norm(conv_ffn(h2c, ffn_up[l], ffn_conv_w[l], ffn_conv_b[l], ffn_down[l], False),
                                     norm_w[l, 3])
    return x
```

```python
import functools

import numpy as np
import jax
import jax.numpy as jnp
from jax import lax
from jax.experimental import pallas as pl
from jax.experimental.pallas import tpu as pltpu

F32 = jnp.float32
BF16 = jnp.bfloat16

D_MODEL = 2048
DEPTH = 2
GRID_W = 64
HEADS = 8
HEAD_DIM = 128
HG_WIDTH = HEADS * HEAD_DIM
CONV_CH = D_MODEL - HG_WIDTH
CONV_WIDTH = 31
CONV_PAD = 16
D_FF = 5632
N_MOD = 6
MOD_ROWS = 8
EPS = 1e-6
LN_EPS = 1e-5
IN_COLS = 3 * HG_WIDTH + 2 * HG_WIDTH + 2 * CONV_CH

CHUNK = 64
N_LEVELS = 6
N_EXP_BLOCKS = N_LEVELS + 2

VMEM_LIMIT = 56 * 1024 * 1024


def _cparams(sem):
    return pltpu.CompilerParams(dimension_semantics=sem, vmem_limit_bytes=VMEM_LIMIT)


def _silu(x):
    return x * jax.nn.sigmoid(x)


def _rms_rows(x):
    return x * lax.rsqrt(jnp.mean(x * x, axis=-1, keepdims=True) + EPS)


def _mod_kernel(s_ref, w_ref, b_ref, o_ref):
    w = w_ref[0].astype(BF16)
    o_ref[0] = jnp.dot(s_ref[...], w, preferred_element_type=F32) + b_ref[0]


def _mod_call(s_rows, w_mod, b_mod):
    tn = 1024
    n = w_mod.shape[-1]
    return pl.pallas_call(
        _mod_kernel,
        grid=(DEPTH, n // tn),
        in_specs=[
            pl.BlockSpec((MOD_ROWS, D_MODEL), lambda l, j: (0, 0)),
            pl.BlockSpec((1, D_MODEL, tn), lambda l, j: (l, 0, j)),
            pl.BlockSpec((1, 1, tn), lambda l, j: (l, 0, j)),
        ],
        out_specs=pl.BlockSpec((1, MOD_ROWS, tn), lambda l, j: (l, 0, j)),
        out_shape=jax.ShapeDtypeStruct((DEPTH, MOD_ROWS, n), F32),
        compiler_params=_cparams(("arbitrary", "arbitrary")),
        name="mod_matmul",
    )(s_rows, w_mod, b_mod.reshape(DEPTH, 1, n))


def _prenorm(x_ref, mod_ref, nw_ref, h_ref, shift_row):
    y = _rms_rows(x_ref[...]) * nw_ref[...]
    h = y * (1.0 + mod_ref[0, shift_row + 1:shift_row + 2, :]) + mod_ref[0, shift_row:shift_row + 1, :]
    h_ref[...] = h.astype(BF16)


def _proj_kernel(x_ref, mod_ref, nw_ref, w_ref, q_ref, f_ref, r_ref, h_ref, *, nq, nf):
    j = pl.program_id(1)

    @pl.when(j == 0)
    def _():
        _prenorm(x_ref, mod_ref, nw_ref, h_ref, 0)

    acc = jnp.dot(h_ref[...], w_ref[...], preferred_element_type=F32)

    @pl.when(j < nq)
    def _():
        q_ref[...] = _silu(acc).astype(BF16)

    @pl.when((j >= nq) & (j < nq + nf))
    def _():
        f_ref[...] = acc

    @pl.when(j >= nq + nf)
    def _():
        r_ref[...] = acc.astype(BF16)


def _proj_call(x2, mod3, nw, w_bf, tm, mod_row):
    m = x2.shape[0]
    tn = 512
    nq = HG_WIDTH // tn
    nf = 2 * HG_WIDTH // tn
    nr = (IN_COLS - 3 * HG_WIDTH) // tn
    return pl.pallas_call(
        functools.partial(_proj_kernel, nq=nq, nf=nf),
        grid=(m // tm, nq + nf + nr),
        in_specs=[
            pl.BlockSpec((tm, D_MODEL), lambda i, j: (i, 0)),
            pl.BlockSpec((1, MOD_ROWS, D_MODEL), lambda i, j: (mod_row(i), 0, 0)),
            pl.BlockSpec((1, D_MODEL), lambda i, j: (0, 0)),
            pl.BlockSpec((D_MODEL, tn), lambda i, j: (0, j)),
        ],
        out_specs=[
            pl.BlockSpec((tm, tn), lambda i, j: (i, jnp.minimum(j, nq - 1))),
            pl.BlockSpec((tm, tn), lambda i, j: (i, jnp.clip(j - nq, 0, nf - 1))),
            pl.BlockSpec((tm, tn), lambda i, j: (i, jnp.clip(j - nq - nf, 0, nr - 1))),
        ],
        out_shape=[
            jax.ShapeDtypeStruct((m, HG_WIDTH), BF16),
            jax.ShapeDtypeStruct((m, 2 * HG_WIDTH), F32),
            jax.ShapeDtypeStruct((m, nr * tn), BF16),
        ],
        scratch_shapes=[pltpu.VMEM((tm, D_MODEL), BF16)],
        compiler_params=_cparams(("arbitrary", "arbitrary")),
        name="in_proj",
    )(x2, mod3, nw, w_bf)


def _ffn_up_kernel(x_ref, mod_ref, nw_ref, w_ref, g_ref, v_ref, h_ref, *, ng):
    j = pl.program_id(1)

    @pl.when(j == 0)
    def _():
        _prenorm(x_ref, mod_ref, nw_ref, h_ref, 3)

    acc = jnp.dot(h_ref[...], w_ref[...], preferred_element_type=F32).astype(BF16)

    @pl.when(j < ng)
    def _():
        g_ref[...] = acc

    @pl.when(j >= ng)
    def _():
        v_ref[...] = acc


def _ffn_up_call(x2, mod3, nw, w_bf, tm, mod_row):
    m = x2.shape[0]
    tn = 512
    ng = D_FF // tn
    return pl.pallas_call(
        functools.partial(_ffn_up_kernel, ng=ng),
        grid=(m // tm, 2 * ng),
        in_specs=[
            pl.BlockSpec((tm, D_MODEL), lambda i, j: (i, 0)),
            pl.BlockSpec((1, MOD_ROWS, D_MODEL), lambda i, j: (mod_row(i), 0, 0)),
            pl.BlockSpec((1, D_MODEL), lambda i, j: (0, 0)),
            pl.BlockSpec((D_MODEL, tn), lambda i, j: (0, j)),
        ],
        out_specs=[
            pl.BlockSpec((tm, tn), lambda i, j: (i, jnp.minimum(j, ng - 1))),
            pl.BlockSpec((tm, tn), lambda i, j: (i, jnp.maximum(j - ng, 0))),
        ],
        out_shape=[
            jax.ShapeDtypeStruct((m, D_FF), BF16),
            jax.ShapeDtypeStruct((m, D_FF), BF16),
        ],
        scratch_shapes=[pltpu.VMEM((tm, D_MODEL), BF16)],
        compiler_params=_cparams(("arbitrary", "arbitrary")),
        name="ffn_up",
    )(x2, mod3, nw, w_bf)


def _scan_constants():
    c = CHUNK
    mats = np.zeros((2, N_EXP_BLOCKS, c, c), np.float32)
    masks = np.zeros((2, N_LEVELS + 1, c, c), np.float32)
    t = np.arange(c)[:, None]
    r = np.arange(c)[None, :]
    mats[0, 0] = r <= t
    mats[0, 1] = r > t
    mats[1, 0] = r >= t
    mats[1, 1] = r < t
    for lvl in range(N_LEVELS):
        half = c >> (lvl + 1)
        start = (t // (2 * half)) * (2 * half)
        mid = start + half
        right = t >= mid
        mats[0, 2 + lvl] = np.where(right, (r >= mid) & (r <= t), (r > t) & (r < mid))
        mats[1, 2 + lvl] = np.where(right, (r >= mid) & (r < t), (r >= t) & (r < mid))
        same = (t // (2 * half)) == (r // (2 * half))
        r_right = (r - (r // (2 * half)) * (2 * half)) >= half
        masks[0, lvl] = same & right & ~r_right
        masks[1, lvl] = same & ~right & r_right
    masks[:, N_LEVELS] = np.eye(c, dtype=np.float32)
    return mats.reshape(2, N_EXP_BLOCKS * c, c), masks


_NT = (((1,), (1,)), ((), ()))
_TN = (((0,), (0,)), ((), ()))


def _scan_kernel(lbl_ref, mats_ref, masks_ref, qf_ref, qb_ref, vf_ref, vb_ref, ff_ref, fb_ref,
                 s0_ref, of_ref, ob_ref, sfin_ref, st_ref, *, layer, n_chunks):
    c = CHUNK
    step = pl.program_id(2)

    @pl.when(step == 0)
    def _():
        st_ref[...] = s0_ref[0, 0]

    rows = [lbl_ref[:, j, :] for j in range(DEPTH)]
    mx = functools.reduce(jnp.maximum, rows)
    es = [jnp.exp(rw - mx) for rw in rows]
    tot = functools.reduce(lambda a, b: a + b, es)
    lb = jnp.zeros_like(mx)
    for j in range(1, layer + 1):
        lb = lb + es[j] / tot

    def one_chunk(d, q_ref, v_ref, f_ref, o_ref, row0):
        q_bf = q_ref[0, pl.ds(row0, c), :]
        q = q_bf.astype(F32)
        v = v_ref[0, pl.ds(row0, c), :]
        raw = f_ref[0, pl.ds(row0, c), :]
        lbd = lb[d:d + 1, :]
        f = lbd + (1.0 - lbd) * jax.nn.sigmoid(raw)
        logf = jnp.log(f)
        k = 1.0 - f
        hi = logf.astype(BF16)
        rem = logf - hi.astype(F32)
        mid = rem.astype(BF16)
        lo = (rem - mid.astype(F32)).astype(BF16)
        mat = mats_ref[d]
        x = (jnp.dot(mat, hi, preferred_element_type=F32)
             + jnp.dot(mat, mid, preferred_element_type=F32)
             + jnp.dot(mat, lo, preferred_element_type=F32))
        e = jnp.exp(x)
        e_inter = e[0:c]
        e_state = e[c:2 * c]
        att = masks_ref[d, N_LEVELS] * lax.dot_general(q_bf, k.astype(BF16), _NT,
                                                       preferred_element_type=F32)
        for lvl in range(N_LEVELS):
            el = e[(2 + lvl) * c:(3 + lvl) * c]
            ql = (q * el).astype(BF16)
            kl = (k * el).astype(BF16)
            att = att + masks_ref[d, lvl] * lax.dot_general(ql, kl, _NT, preferred_element_type=F32)
        st = st_ref[d]
        o = lax.dot_general((q * e_inter).astype(BF16), st.astype(BF16), _NT,
                            preferred_element_type=F32)
        o = o + jnp.dot(att.astype(BF16), v, preferred_element_type=F32)
        o_ref[0, pl.ds(row0, c), :] = o
        last = c - 1 if d == 0 else 0
        decay = e_inter[last:last + 1, :]
        ks = (k * e_state).astype(BF16)
        st_ref[d] = st * decay + lax.dot_general(v, ks, _TN, preferred_element_type=F32)

    def body(ci, carry):
        one_chunk(0, qf_ref, vf_ref, ff_ref, of_ref, pl.multiple_of(ci * c, c))
        one_chunk(1, qb_ref, vb_ref, fb_ref, ob_ref, pl.multiple_of((n_chunks - 1 - ci) * c, c))
        return carry

    lax.fori_loop(0, n_chunks, body, 0)

    @pl.when(step == pl.num_programs(2) - 1)
    def _():
        sfin_ref[0, 0] = st_ref[...]


def _scan_call(q3, f3, r3, lb_logits, s0, mats, masks, layer, tl):
    bsz, length, _ = q3.shape
    nl = length // tl
    fwd = lambda b, h, i: (b, i, h)
    bwd = lambda b, h, i: (b, nl - 1 - i, h)
    bwd_f = lambda b, h, i: (b, nl - 1 - i, HEADS + h)
    blk = (1, tl, HEAD_DIM)
    return pl.pallas_call(
        functools.partial(_scan_kernel, layer=layer, n_chunks=tl // CHUNK),
        grid=(bsz, HEADS, nl),
        in_specs=[
            pl.BlockSpec((2, DEPTH, HEAD_DIM), lambda b, h, i: (0, 0, h)),
            pl.BlockSpec(mats.shape, lambda b, h, i: (0, 0, 0)),
            pl.BlockSpec(masks.shape, lambda b, h, i: (0, 0, 0, 0)),
            pl.BlockSpec(blk, fwd),
            pl.BlockSpec(blk, bwd),
            pl.BlockSpec(blk, fwd),
            pl.BlockSpec(blk, bwd),
            pl.BlockSpec(blk, fwd),
            pl.BlockSpec(blk, bwd_f),
            pl.BlockSpec((1, 1, 2, HEAD_DIM, HEAD_DIM), lambda b, h, i: (b, h, 0, 0, 0)),
        ],
        out_specs=[
            pl.BlockSpec(blk, fwd),
            pl.BlockSpec(blk, bwd),
            pl.BlockSpec((1, 1, 2, HEAD_DIM, HEAD_DIM), lambda b, h, i: (b, h, 0, 0, 0)),
        ],
        out_shape=[
            jax.ShapeDtypeStruct((bsz, length, HG_WIDTH), F32),
            jax.ShapeDtypeStruct((bsz, length, HG_WIDTH), F32),
            jax.ShapeDtypeStruct((bsz, HEADS, 2, HEAD_DIM, HEAD_DIM), F32),
        ],
        scratch_shapes=[pltpu.VMEM((2, HEAD_DIM, HEAD_DIM), F32)],
        compiler_params=_cparams(("arbitrary", "arbitrary", "arbitrary")),
        name="hgrn2_scan",
    )(lb_logits, mats, masks, q3, q3, r3, r3, f3, f3, s0)


def _cconv_kernel(a_ref, b_ref, cw_ref, cb_ref, lw_ref, lb_ref, o_ref, u_ref, *, length, rows):
    zeros = jnp.zeros((CONV_PAD, HEAD_DIM), F32)
    u_ref[0:CONV_PAD, :] = zeros
    u_ref[CONV_PAD + length:2 * CONV_PAD + length, :] = zeros
    n_tiles = length // rows

    def fill(ti, carry):
        r0 = pl.multiple_of(ti * rows, rows)
        a = a_ref[0, pl.ds(r0, rows), :].astype(F32)
        b = b_ref[0, pl.ds(r0, rows), :].astype(F32)
        u_ref[pl.ds(CONV_PAD + r0, rows), :] = a * jax.nn.sigmoid(b)
        return carry

    lax.fori_loop(0, n_tiles, fill, 0)

    half = CONV_WIDTH // 2

    def conv(ti, carry):
        r0 = pl.multiple_of(ti * rows, rows)
        acc = jnp.zeros((rows, HEAD_DIM), F32)
        for j in range(CONV_WIDTH):
            acc = acc + cw_ref[j:j + 1, :] * u_ref[pl.ds(r0 + (CONV_PAD - half + j), rows), :]
        acc = acc + cb_ref[...]
        mu = jnp.mean(acc, axis=-1, keepdims=True)
        dlt = acc - mu
        var = jnp.mean(dlt * dlt, axis=-1, keepdims=True)
        y = dlt * lax.rsqrt(var + LN_EPS) * lw_ref[...] + lb_ref[...]
        o_ref[0, pl.ds(r0, rows), :] = _silu(y).astype(BF16)
        return carry

    lax.fori_loop(0, n_tiles, conv, 0)


def _cconv_call(r3, cw, cb, lw, lb):
    bsz, length, _ = r3.shape
    groups = CONV_CH // HEAD_DIM
    a_col = 2 * HG_WIDTH // HEAD_DIM
    b_col = a_col + groups
    vec = pl.BlockSpec((1, HEAD_DIM), lambda b, g: (0, g))
    return pl.pallas_call(
        functools.partial(_cconv_kernel, length=length, rows=128),
        grid=(bsz, groups),
        in_specs=[
            pl.BlockSpec((1, length, HEAD_DIM), lambda b, g: (b, 0, a_col + g)),
            pl.BlockSpec((1, length, HEAD_DIM), lambda b, g: (b, 0, b_col + g)),
            pl.BlockSpec((CONV_WIDTH, HEAD_DIM), lambda b, g: (0, g)),
            vec, vec, vec,
        ],
        out_specs=pl.BlockSpec((1, length, HEAD_DIM), lambda b, g: (b, 0, g)),
        out_shape=jax.ShapeDtypeStruct((bsz, length, CONV_CH), BF16),
        scratch_shapes=[pltpu.VMEM((length + 2 * CONV_PAD, HEAD_DIM), F32)],
        compiler_params=_cparams(("arbitrary", "arbitrary")),
        name="conformer_conv",
    )(r3, r3, cw, cb.reshape(1, -1), lw.reshape(1, -1), lb.reshape(1, -1))


def _outproj_kernel(of_ref, ob_ref, g_ref, cv_ref, x_ref, mod_ref, hgw_ref, nw_ref, w_ref, o_ref):
    o = of_ref[...] + ob_ref[...]
    heads = []
    for h in range(HEADS):
        oh = o[:, h * HEAD_DIM:(h + 1) * HEAD_DIM]
        heads.append(_rms_rows(oh) * hgw_ref[...])
    r = jnp.concatenate(heads, axis=-1) * _silu(g_ref[...].astype(F32))
    y = jnp.dot(r.astype(BF16), w_ref[0:HG_WIDTH, :], preferred_element_type=F32)
    y = y + jnp.dot(cv_ref[...], w_ref[HG_WIDTH:D_MODEL, :], preferred_element_type=F32)
    o_ref[...] = x_ref[...] + mod_ref[0, 2:3, :] * (_rms_rows(y) * nw_ref[...])


def _outproj_call(of2, ob2, r2, cv2, x2, mod3, hgw, nw, w_bf, tm, mod_row):
    m = x2.shape[0]
    return pl.pallas_call(
        _outproj_kernel,
        grid=(m // tm,),
        in_specs=[
            pl.BlockSpec((tm, HG_WIDTH), lambda i: (i, 0)),
            pl.BlockSpec((tm, HG_WIDTH), lambda i: (i, 0)),
            pl.BlockSpec((tm, HG_WIDTH), lambda i: (i, 1)),
            pl.BlockSpec((tm, CONV_CH), lambda i: (i, 0)),
            pl.BlockSpec((tm, D_MODEL), lambda i: (i, 0)),
            pl.BlockSpec((1, MOD_ROWS, D_MODEL), lambda i: (mod_row(i), 0, 0)),
            pl.BlockSpec((1, HEAD_DIM), lambda i: (0, 0)),
            pl.BlockSpec((1, D_MODEL), lambda i: (0, 0)),
            pl.BlockSpec((D_MODEL, D_MODEL), lambda i: (0, 0)),
        ],
        out_specs=pl.BlockSpec((tm, D_MODEL), lambda i: (i, 0)),
        out_shape=jax.ShapeDtypeStruct((m, D_MODEL), F32),
        compiler_params=_cparams(("arbitrary",)),
        name="out_proj",
    )(of2, ob2, r2, cv2, x2, mod3, hgw, nw, w_bf)


EXT_PAD = 8


def _ffn_down_kernel(gp_ref, gm_ref, gn_ref, val_ref, cw_ref, cb_ref, w_ref, x_ref, mod_ref, nw_ref,
                     o_ref, acc_ref, ext_ref, *, tm, tiles_per_seq, grid_mode):
    i = pl.program_id(0)
    kk = pl.program_id(1)
    tk = gm_ref.shape[1]
    base = EXT_PAD + GRID_W
    first = (i % tiles_per_seq) == 0
    last = (i % tiles_per_seq) == tiles_per_seq - 1
    zpad = jnp.zeros((EXT_PAD, tk), F32)
    ext_ref[0:EXT_PAD, :] = zpad
    ext_ref[base + tm + GRID_W:base + tm + GRID_W + EXT_PAD, :] = zpad
    ext_ref[EXT_PAD:base, :] = jnp.where(first, 0.0, gp_ref[...].astype(F32))
    ext_ref[base:base + tm, :] = gm_ref[...].astype(F32)
    ext_ref[base + tm:base + tm + GRID_W, :] = jnp.where(last, 0.0, gn_ref[...].astype(F32))

    def tap(dh, dw):
        return cw_ref[dh, dw:dw + 1, :] * ext_ref[pl.ds(base + (dh - 1) * GRID_W + (dw - 1), tm), :]

    if grid_mode:
        col = lax.broadcasted_iota(jnp.int32, (tm, tk), 0) & (GRID_W - 1)
        left = tap(0, 0) + tap(1, 0) + tap(2, 0)
        mid = tap(0, 1) + tap(1, 1) + tap(2, 1)
        right = tap(0, 2) + tap(1, 2) + tap(2, 2)
        conv = mid + jnp.where(col == 0, 0.0, left) + jnp.where(col == GRID_W - 1, 0.0, right)
    else:
        conv = tap(1, 0) + tap(1, 1) + tap(1, 2)
    conv = conv + cb_ref[...]
    gelu = 0.5 * conv * (1.0 + lax.erf(conv * np.float32(np.sqrt(0.5))))
    act = (gelu * val_ref[...].astype(F32)).astype(BF16)
    part = jnp.dot(act, w_ref[...], preferred_element_type=F32)

    @pl.when(kk == 0)
    def _():
        acc_ref[...] = part

    @pl.when(kk > 0)
    def _():
        acc_ref[...] += part

    @pl.when(kk == pl.num_programs(1) - 1)
    def _():
        o_ref[...] = x_ref[...] + mod_ref[0, 5:6, :] * (_rms_rows(acc_ref[...]) * nw_ref[...])


def _ffn_down_call(gate2, val2, cw, cb, w_bf, x2, mod3, nw, tm, seq_len, mod_row, grid_mode):
    m = x2.shape[0]
    tk = 512
    hb = tm // GRID_W
    n_hblk = m // GRID_W
    tiles_per_seq = seq_len // tm
    return pl.pallas_call(
        functools.partial(_ffn_down_kernel, tm=tm, tiles_per_seq=tiles_per_seq, grid_mode=grid_mode),
        grid=(m // tm, D_FF // tk),
        in_specs=[
            pl.BlockSpec((GRID_W, tk), lambda i, k: (jnp.maximum(i * hb - 1, 0), k)),
            pl.BlockSpec((tm, tk), lambda i, k: (i, k)),
            pl.BlockSpec((GRID_W, tk), lambda i, k: (jnp.minimum((i + 1) * hb, n_hblk - 1), k)),
            pl.BlockSpec((tm, tk), lambda i, k: (i, k)),
            pl.BlockSpec((3, 3, tk), lambda i, k: (0, 0, k)),
            pl.BlockSpec((1, tk), lambda i, k: (0, k)),
            pl.BlockSpec((tk, D_MODEL), lambda i, k: (k, 0)),
            pl.BlockSpec((tm, D_MODEL), lambda i, k: (i, 0)),
            pl.BlockSpec((1, MOD_ROWS, D_MODEL), lambda i, k: (mod_row(i), 0, 0)),
            pl.BlockSpec((1, D_MODEL), lambda i, k: (0, 0)),
        ],
        out_specs=pl.BlockSpec((tm, D_MODEL), lambda i, k: (i, 0)),
        out_shape=jax.ShapeDtypeStruct((m, D_MODEL), F32),
        scratch_shapes=[
            pltpu.VMEM((tm, D_MODEL), F32),
            pltpu.VMEM((tm + 2 * GRID_W + 2 * EXT_PAD, tk), F32),
        ],
        compiler_params=_cparams(("arbitrary", "arbitrary")),
        name="ffn_down",
    )(gate2, gate2, gate2, val2, cw, cb.reshape(1, -1), w_bf, x2, mod3, nw)


def _stream(x3, mod3, l, params, s0, consts, tiles, tl, row_of, grid_mode, full):
    bsz, length, _ = x3.shape
    m = bsz * length
    x2 = x3.reshape(m, D_MODEL)
    nw = params["norm_w"][l]
    tm, tm_out = tiles
    mod_row = row_of(tm)
    q2, f2, r2 = _proj_call(x2, mod3, nw[0:1], params["w_in"][l], tm, mod_row)
    mats, masks = consts
    of3, ob3, sfin = _scan_call(q2.reshape(bsz, length, -1), f2.reshape(bsz, length, -1),
                                r2.reshape(bsz, length, -1), params["lb_logits"], s0, mats, masks, l, tl)
    if not full:
        return None, sfin
    cv3 = _cconv_call(r2.reshape(bsz, length, -1), params["conv_w"][l], params["conv_b"][l],
                      params["conv_ln_w"][l], params["conv_ln_b"][l])
    x2 = _outproj_call(of3.reshape(m, -1), ob3.reshape(m, -1), r2, cv3.reshape(m, -1), x2, mod3,
                       params["hg_norm_w"][l].reshape(1, -1), nw[1:2], params["w_out"][l], tm_out,
                       row_of(tm_out))
    gate2, val2 = _ffn_up_call(x2, mod3, nw[2:3], params["ffn_up"][l], tm, mod_row)
    x2 = _ffn_down_call(gate2, val2, params["ffn_conv_w"][l], params["ffn_conv_b"][l],
                        params["ffn_down"][l], x2, mod3, nw[3:4], tm, length, mod_row, grid_mode)
    return x2.reshape(bsz, length, D_MODEL), sfin


def kernel(x, c, ctx, c_ctx, w_mod, b_mod, norm_w, w_in, lb_logits, hg_norm_w, conv_w, conv_b, conv_ln_w,
           conv_ln_b, w_out, ffn_up, ffn_conv_w, ffn_conv_b, ffn_down):
    bsz, seq, _ = x.shape
    ctx_len = ctx.shape[1]
    params = dict(norm_w=norm_w, w_in=w_in.astype(BF16), lb_logits=lb_logits, hg_norm_w=hg_norm_w,
                  conv_w=conv_w, conv_b=conv_b, conv_ln_w=conv_ln_w, conv_ln_b=conv_ln_b,
                  w_out=w_out.astype(BF16), ffn_up=ffn_up.astype(BF16), ffn_conv_w=ffn_conv_w,
                  ffn_conv_b=ffn_conv_b, ffn_down=ffn_down.astype(BF16))
    mats_np, masks_np = _scan_constants()
    consts = (jnp.asarray(mats_np, BF16), jnp.asarray(masks_np, F32))

    cond = jnp.concatenate([c, c_ctx[None, :], jnp.zeros((MOD_ROWS - bsz - 1, D_MODEL), c.dtype)], axis=0)
    mod = _mod_call(_silu(cond).astype(BF16), w_mod, b_mod)
    mod = mod[:, :bsz + 1].reshape(DEPTH, bsz + 1, N_MOD, D_MODEL)
    mod = jnp.pad(mod, ((0, 0), (0, 0), (0, MOD_ROWS - N_MOD), (0, 0)))

    lat_row = lambda tm: (lambda i: i // (seq // tm))
    ctx_row = lambda tm: (lambda i: bsz)
    zero_state = jnp.zeros((bsz, HEADS, 2, HEAD_DIM, HEAD_DIM), F32)
    xc = ctx
    for l in range(DEPTH):
        last = l == DEPTH - 1
        xc, s_ctx = _stream(xc, mod[l], l, params, zero_state, consts, (ctx_len, ctx_len), ctx_len,
                            ctx_row, grid_mode=False, full=not last)
        x, _ = _stream(x, mod[l], l, params, s_ctx, consts, (512, 256), 512, lat_row,
                       grid_mode=True, full=True)
    return x
```

```python
import functools

import numpy as np
import jax
import jax.numpy as jnp
from jax import lax
from jax.experimental import pallas as pl
from jax.experimental.pallas import tpu as pltpu

F32 = jnp.float32
BF16 = jnp.bfloat16

D_MODEL = 2048
DEPTH = 2
GRID_W = 64
HEADS = 8
HEAD_DIM = 128
HG_WIDTH = HEADS * HEAD_DIM
CONV_CH = D_MODEL - HG_WIDTH
CONV_WIDTH = 31
CONV_PAD = 16
D_FF = 5632
N_MOD = 6
MOD_ROWS = 8
EPS = 1e-6
LN_EPS = 1e-5
LOG2_E = float(np.log2(np.e))
IN_COLS = 3 * HG_WIDTH + 2 * HG_WIDTH + 2 * CONV_CH

CHUNK = 64
N_LEVELS = 6

VMEM_LIMIT = 56 * 1024 * 1024


def _cparams(sem):
    return pltpu.CompilerParams(dimension_semantics=sem, vmem_limit_bytes=VMEM_LIMIT)


def _silu(x):
    return x * jax.nn.sigmoid(x)


def _rms_rows(x):
    return x * lax.rsqrt(jnp.mean(x * x, axis=-1, keepdims=True) + EPS)


def _mod_kernel(s_ref, w_ref, b_ref, o_ref):
    w = w_ref[0].astype(BF16)
    o_ref[0] = jnp.dot(s_ref[...], w, preferred_element_type=F32) + b_ref[0]


def _mod_call(s_rows, w_mod, b_mod):
    tn = 1024
    n = w_mod.shape[-1]
    return pl.pallas_call(
        _mod_kernel,
        grid=(DEPTH, n // tn),
        in_specs=[
            pl.BlockSpec((MOD_ROWS, D_MODEL), lambda l, j: (0, 0)),
            pl.BlockSpec((1, D_MODEL, tn), lambda l, j: (l, 0, j)),
            pl.BlockSpec((1, 1, tn), lambda l, j: (l, 0, j)),
        ],
        out_specs=pl.BlockSpec((1, MOD_ROWS, tn), lambda l, j: (l, 0, j)),
        out_shape=jax.ShapeDtypeStruct((DEPTH, MOD_ROWS, n), F32),
        compiler_params=_cparams(("arbitrary", "arbitrary")),
        name="mod_matmul",
    )(s_rows, w_mod, b_mod.reshape(DEPTH, 1, n))


def _prenorm(x_ref, mod_ref, nw_ref, h_ref, shift_row):
    y = _rms_rows(x_ref[...]) * nw_ref[...]
    h = y * (1.0 + mod_ref[0, shift_row + 1:shift_row + 2, :]) + mod_ref[0, shift_row:shift_row + 1, :]
    h_ref[...] = h.astype(BF16)


def _norm_proj_kernel(x_ref, mod_ref, nw_ref, w_ref, o_ref, h_ref, *, shift_row):
    @pl.when(pl.program_id(1) == 0)
    def _():
        _prenorm(x_ref, mod_ref, nw_ref, h_ref, shift_row)

    o_ref[...] = jnp.dot(h_ref[...], w_ref[...], preferred_element_type=F32).astype(o_ref.dtype)


def _norm_proj_call(x2, mod3, nw, w_bf, layer, col0, n_cols, tm, tn, mod_row, shift_row, name):
    m = x2.shape[0]
    c0 = col0 // tn
    return pl.pallas_call(
        functools.partial(_norm_proj_kernel, shift_row=shift_row),
        grid=(m // tm, n_cols // tn),
        in_specs=[
            pl.BlockSpec((tm, D_MODEL), lambda i, j: (i, 0)),
            pl.BlockSpec((1, MOD_ROWS, D_MODEL), lambda i, j: (mod_row(i), 0, 0)),
            pl.BlockSpec((1, D_MODEL), lambda i, j: (0, 0)),
            pl.BlockSpec((None, D_MODEL, tn), lambda i, j: (layer, 0, c0 + j)),
        ],
        out_specs=[
            pl.BlockSpec((tm, tn), lambda i, j: (i, j)),
            pl.BlockSpec((tm, D_MODEL), lambda i, j: (i, 0)),
        ],
        out_shape=[
            jax.ShapeDtypeStruct((m, n_cols), BF16),
            jax.ShapeDtypeStruct((m, D_MODEL), BF16),
        ],
        compiler_params=_cparams(("arbitrary", "arbitrary")),
        name=name,
    )(x2, mod3, nw, w_bf)


def _proj_kernel(h_ref, w_ref, o_ref, *, act):
    acc = jnp.dot(h_ref[...], w_ref[...], preferred_element_type=F32)
    if act:
        acc = _silu(acc)
    o_ref[...] = acc.astype(o_ref.dtype)


def _proj_call(h2, w_bf, layer, col0, n_cols, tm, tn, act, out_dtype, name):
    m = h2.shape[0]
    c0 = col0 // tn
    return pl.pallas_call(
        functools.partial(_proj_kernel, act=act),
        grid=(m // tm, n_cols // tn),
        in_specs=[
            pl.BlockSpec((tm, D_MODEL), lambda i, j: (i, 0)),
            pl.BlockSpec((None, D_MODEL, tn), lambda i, j: (layer, 0, c0 + j)),
        ],
        out_specs=pl.BlockSpec((tm, tn), lambda i, j: (i, j)),
        out_shape=jax.ShapeDtypeStruct((m, n_cols), out_dtype),
        compiler_params=_cparams(("arbitrary", "arbitrary")),
        name=name,
    )(h2, w_bf)


def _scan_constants():
    c = CHUNK
    t = np.arange(c)[:, None]
    r = np.arange(c)[None, :]
    tri = np.stack([np.tile(r <= t, (1, 3)), np.tile(r >= t, (1, 3))]).astype(np.float32)
    level = np.full((2, c, c), -1, np.int32)
    for lvl in range(N_LEVELS):
        half = c >> (lvl + 1)
        same = (t // (2 * half)) == (r // (2 * half))
        t_late = (t % (2 * half)) >= half
        r_late = (r % (2 * half)) >= half
        level[0][same & t_late & ~r_late] = lvl
        level[1][same & ~t_late & r_late] = lvl
    level[:, np.arange(c), np.arange(c)] = N_LEVELS
    return tri, level


_NT = (((1,), (1,)), ((), ()))
_TN = (((0,), (0,)), ((), ()))


def _neg_abs(x):
    bits = lax.bitcast_convert_type(x, jnp.int32) | jnp.int32(-2 ** 31)
    return lax.bitcast_convert_type(bits, F32)


def _scan_kernel(lbl_ref, tri_ref, level_ref, qf_ref, qb_ref, vf_ref, vb_ref, ff_ref, fb_ref,
                 s0_ref, of_ref, ob_ref, sfin_ref, st_ref, *, layer, n_chunks):
    c = CHUNK
    step = pl.program_id(2)

    @pl.when(step == 0)
    def _():
        st_ref[...] = s0_ref[0, 0]

    rows = [lbl_ref[:, j, :] for j in range(DEPTH)]
    mx = functools.reduce(jnp.maximum, rows)
    es = [jnp.exp(rw - mx) for rw in rows]
    tot = functools.reduce(lambda a, b: a + b, es)
    lb = jnp.zeros_like(mx)
    for j in range(1, layer + 1):
        lb = lb + es[j] / tot

    tri = tri_ref[...]
    level = level_ref[...]
    sub = lax.broadcasted_iota(jnp.int32, (c // 8, 8, HEAD_DIM), 1)

    def ref_rows(b, half, d):
        pick = half - 1 if d == 0 else half
        if half >= 8:
            parts = []
            for p0 in range(0, c, 2 * half):
                parts.append(jnp.broadcast_to(b[p0 + pick:p0 + pick + 1, :], (2 * half, HEAD_DIM)))
            return parts[0] if len(parts) == 1 else jnp.concatenate(parts, axis=0)
        b3 = b.reshape(c // 8, 8, HEAD_DIM)
        out = None
        for g0 in range(0, 8, 2 * half):
            cand = jnp.broadcast_to(b3[:, g0 + pick:g0 + pick + 1, :], b3.shape)
            out = cand if out is None else jnp.where(sub >= g0, cand, out)
        return out.reshape(c, HEAD_DIM)

    refs = ((qf_ref, vf_ref, ff_ref, of_ref), (qb_ref, vb_ref, fb_ref, ob_ref))

    def gates(d, row0):
        raw = refs[d][2][0, row0:row0 + c, :]
        lbd = lb[d:d + 1, :]
        f = lbd + (1.0 - lbd) * jax.nn.sigmoid(raw)
        logf = jnp.log(f)
        hi = logf.astype(BF16)
        rem = logf - hi.astype(F32)
        mid = rem.astype(BF16)
        lo = (rem - mid.astype(F32)).astype(BF16)
        b = jnp.dot(tri[d], jnp.concatenate([hi, mid, lo], axis=0), preferred_element_type=F32)
        return (1.0 - f).astype(BF16), b * LOG2_E

    def intra(d, row0, k_bf, b):
        q_bf = refs[d][0][0, row0:row0 + c, :]
        att = jnp.where(level[d] == N_LEVELS,
                        lax.dot_general(q_bf, k_bf, _NT, preferred_element_type=F32), 0.0)
        for lvl in range(N_LEVELS):
            half = c >> (lvl + 1)
            el = jnp.exp2(_neg_abs(b - ref_rows(b, half, d))).astype(BF16)
            att = jnp.where(level[d] == lvl,
                            lax.dot_general(q_bf * el, k_bf * el, _NT, preferred_element_type=F32), att)
        last = c - 1 if d == 0 else 0
        blast = b[last:last + 1, :]
        q_in = q_bf * jnp.exp2(b).astype(BF16)
        k_out = k_bf * jnp.exp2(blast - b).astype(BF16)
        return att.astype(BF16), q_in, k_out, jnp.exp2(blast)

    def readout(d, row0, att, q_in, k_out, decay, st):
        v = refs[d][1][0, row0:row0 + c, :]
        o = lax.dot_general(q_in, st.astype(BF16), _NT, preferred_element_type=F32)
        refs[d][3][0, row0:row0 + c, :] = o + jnp.dot(att, v, preferred_element_type=F32)
        return st * decay + lax.dot_general(v, k_out, _TN, preferred_element_type=F32)

    work = []
    for ci in range(n_chunks):
        work.append((0, ci * c))
        work.append((1, (n_chunks - 1 - ci) * c))
    st = [st_ref[0], st_ref[1]]
    stage1 = {}
    stage2 = {}
    for n in range(len(work) + 2):
        if n >= 2:
            d, row0 = work[n - 2]
            st[d] = readout(d, row0, *stage2.pop(n - 2), st[d])
        if 1 <= n <= len(work):
            d, row0 = work[n - 1]
            stage2[n - 1] = intra(d, row0, *stage1.pop(n - 1))
        if n < len(work):
            stage1[n] = gates(*work[n])
    st_ref[0] = st[0]
    st_ref[1] = st[1]

    @pl.when(step == pl.num_programs(2) - 1)
    def _():
        sfin_ref[0, 0] = st_ref[...]


def _scan_call(q3, f3, r3, lb_logits, s0, tri, level, layer, tl):
    bsz, length, _ = q3.shape
    nl = length // tl
    fwd = lambda b, h, i: (b, i, h)
    bwd = lambda b, h, i: (b, nl - 1 - i, h)
    bwd_f = lambda b, h, i: (b, nl - 1 - i, HEADS + h)
    blk = (1, tl, HEAD_DIM)
    return pl.pallas_call(
        functools.partial(_scan_kernel, layer=layer, n_chunks=tl // CHUNK),
        grid=(bsz, HEADS, nl),
        in_specs=[
            pl.BlockSpec((2, DEPTH, HEAD_DIM), lambda b, h, i: (0, 0, h)),
            pl.BlockSpec(tri.shape, lambda b, h, i: (0, 0, 0)),
            pl.BlockSpec(level.shape, lambda b, h, i: (0, 0, 0)),
            pl.BlockSpec(blk, fwd),
            pl.BlockSpec(blk, bwd),
            pl.BlockSpec(blk, fwd),
            pl.BlockSpec(blk, bwd),
            pl.BlockSpec(blk, fwd),
            pl.BlockSpec(blk, bwd_f),
            pl.BlockSpec((1, 1, 2, HEAD_DIM, HEAD_DIM), lambda b, h, i: (b, h, 0, 0, 0)),
        ],
        out_specs=[
            pl.BlockSpec(blk, fwd),
            pl.BlockSpec(blk, bwd),
            pl.BlockSpec((1, 1, 2, HEAD_DIM, HEAD_DIM), lambda b, h, i: (b, h, 0, 0, 0)),
        ],
        out_shape=[
            jax.ShapeDtypeStruct((bsz, length, HG_WIDTH), F32),
            jax.ShapeDtypeStruct((bsz, length, HG_WIDTH), F32),
            jax.ShapeDtypeStruct((bsz, HEADS, 2, HEAD_DIM, HEAD_DIM), F32),
        ],
        scratch_shapes=[pltpu.VMEM((2, HEAD_DIM, HEAD_DIM), F32)],
        compiler_params=_cparams(("arbitrary", "arbitrary", "arbitrary")),
        name="hgrn2_scan",
    )(lb_logits, tri, level, q3, q3, r3, r3, f3, f3, s0)


def _cconv_kernel(a_ref, b_ref, cw_ref, cb_ref, lw_ref, lb_ref, o_ref, u_ref, *, length, rows):
    zeros = jnp.zeros((CONV_PAD, HEAD_DIM), F32)
    u_ref[0:CONV_PAD, :] = zeros
    u_ref[CONV_PAD + length:2 * CONV_PAD + length, :] = zeros
    n_tiles = length // rows

    def fill(ti, carry):
        r0 = pl.multiple_of(ti * rows, rows)
        a = a_ref[0, pl.ds(r0, rows), :].astype(F32)
        b = b_ref[0, pl.ds(r0, rows), :].astype(F32)
        u_ref[pl.ds(CONV_PAD + r0, rows), :] = a * jax.nn.sigmoid(b)
        return carry

    lax.fori_loop(0, n_tiles, fill, 0)

    half = CONV_WIDTH // 2

    def conv(ti, carry):
        r0 = pl.multiple_of(ti * rows, rows)
        acc = jnp.zeros((rows, HEAD_DIM), F32)
        for j in range(CONV_WIDTH):
            acc = acc + cw_ref[j:j + 1, :] * u_ref[pl.ds(r0 + (CONV_PAD - half + j), rows), :]
        acc = acc + cb_ref[...]
        mu = jnp.mean(acc, axis=-1, keepdims=True)
        dlt = acc - mu
        var = jnp.mean(dlt * dlt, axis=-1, keepdims=True)
        y = dlt * lax.rsqrt(var + LN_EPS) * lw_ref[...] + lb_ref[...]
        o_ref[0, pl.ds(r0, rows), :] = _silu(y).astype(BF16)
        return carry

    lax.fori_loop(0, n_tiles, conv, 0)


def _cconv_call(r3, cw, cb, lw, lb):
    bsz, length, _ = r3.shape
    groups = CONV_CH // HEAD_DIM
    a_col = 2 * HG_WIDTH // HEAD_DIM
    b_col = a_col + groups
    vec = pl.BlockSpec((1, HEAD_DIM), lambda b, g: (0, g))
    return pl.pallas_call(
        functools.partial(_cconv_kernel, length=length, rows=128),
        grid=(bsz, groups),
        in_specs=[
            pl.BlockSpec((1, length, HEAD_DIM), lambda b, g: (b, 0, a_col + g)),
            pl.BlockSpec((1, length, HEAD_DIM), lambda b, g: (b, 0, b_col + g)),
            pl.BlockSpec((CONV_WIDTH, HEAD_DIM), lambda b, g: (0, g)),
            vec, vec, vec,
        ],
        out_specs=pl.BlockSpec((1, length, HEAD_DIM), lambda b, g: (b, 0, g)),
        out_shape=jax.ShapeDtypeStruct((bsz, length, CONV_CH), BF16),
        scratch_shapes=[pltpu.VMEM((length + 2 * CONV_PAD, HEAD_DIM), F32)],
        compiler_params=_cparams(("arbitrary", "arbitrary")),
        name="conformer_conv",
    )(r3, r3, cw, cb.reshape(1, -1), lw.reshape(1, -1), lb.reshape(1, -1))


def _outproj_kernel(of_ref, ob_ref, g_ref, cv_ref, x_ref, mod_ref, hgw_ref, nw_ref, w_ref, o_ref):
    o = of_ref[...] + ob_ref[...]
    heads = []
    for h in range(HEADS):
        oh = o[:, h * HEAD_DIM:(h + 1) * HEAD_DIM]
        heads.append(_rms_rows(oh) * hgw_ref[...])
    r = jnp.concatenate(heads, axis=-1) * _silu(g_ref[...].astype(F32))
    y = jnp.dot(r.astype(BF16), w_ref[0:HG_WIDTH, :], preferred_element_type=F32)
    y = y + jnp.dot(cv_ref[...], w_ref[HG_WIDTH:D_MODEL, :], preferred_element_type=F32)
    o_ref[...] = x_ref[...] + mod_ref[0, 2:3, :] * (_rms_rows(y) * nw_ref[...])


def _outproj_call(of2, ob2, r2, cv2, x2, mod3, hgw, nw, w_bf, layer, tm, mod_row):
    m = x2.shape[0]
    return pl.pallas_call(
        _outproj_kernel,
        grid=(m // tm,),
        in_specs=[
            pl.BlockSpec((tm, HG_WIDTH), lambda i: (i, 0)),
            pl.BlockSpec((tm, HG_WIDTH), lambda i: (i, 0)),
            pl.BlockSpec((tm, HG_WIDTH), lambda i: (i, 1)),
            pl.BlockSpec((tm, CONV_CH), lambda i: (i, 0)),
            pl.BlockSpec((tm, D_MODEL), lambda i: (i, 0)),
            pl.BlockSpec((1, MOD_ROWS, D_MODEL), lambda i: (mod_row(i), 0, 0)),
            pl.BlockSpec((1, HEAD_DIM), lambda i: (0, 0)),
            pl.BlockSpec((1, D_MODEL), lambda i: (0, 0)),
            pl.BlockSpec((None, D_MODEL, D_MODEL), lambda i: (layer, 0, 0)),
        ],
        out_specs=pl.BlockSpec((tm, D_MODEL), lambda i: (i, 0)),
        out_shape=jax.ShapeDtypeStruct((m, D_MODEL), F32),
        compiler_params=_cparams(("arbitrary",)),
        name="out_proj",
    )(of2, ob2, r2, cv2, x2, mod3, hgw, nw, w_bf)


EXT_PAD = 8


def _ffn_down_kernel(gp_ref, gm_ref, gn_ref, val_ref, cw_ref, cb_ref, w_ref, x_ref, mod_ref, nw_ref,
                     o_ref, acc_ref, ext_ref, *, tm, tiles_per_seq, grid_mode):
    i = pl.program_id(0)
    kk = pl.program_id(1)
    tk = gm_ref.shape[1]
    base = EXT_PAD + GRID_W
    first = (i % tiles_per_seq) == 0
    last = (i % tiles_per_seq) == tiles_per_seq - 1
    zpad = jnp.zeros((EXT_PAD, tk), F32)
    ext_ref[0:EXT_PAD, :] = zpad
    ext_ref[base + tm + GRID_W:base + tm + GRID_W + EXT_PAD, :] = zpad
    ext_ref[EXT_PAD:base, :] = jnp.where(first, 0.0, gp_ref[...].astype(F32))
    ext_ref[base:base + tm, :] = gm_ref[...].astype(F32)
    ext_ref[base + tm:base + tm + GRID_W, :] = jnp.where(last, 0.0, gn_ref[...].astype(F32))

    def tap(dh, dw):
        return cw_ref[dh, dw:dw + 1, :] * ext_ref[pl.ds(base + (dh - 1) * GRID_W + (dw - 1), tm), :]

    if grid_mode:
        col = lax.broadcasted_iota(jnp.int32, (tm, tk), 0) & (GRID_W - 1)
        left = tap(0, 0) + tap(1, 0) + tap(2, 0)
        mid = tap(0, 1) + tap(1, 1) + tap(2, 1)
        right = tap(0, 2) + tap(1, 2) + tap(2, 2)
        conv = mid + jnp.where(col == 0, 0.0, left) + jnp.where(col == GRID_W - 1, 0.0, right)
    else:
        conv = tap(1, 0) + tap(1, 1) + tap(1, 2)
    conv = conv + cb_ref[...]
    gelu = 0.5 * conv * (1.0 + lax.erf(conv * np.float32(np.sqrt(0.5))))
    act = (gelu * val_ref[...].astype(F32)).astype(BF16)
    part = jnp.dot(act, w_ref[...], preferred_element_type=F32)

    @pl.when(kk == 0)
    def _():
        acc_ref[...] = part

    @pl.when(kk > 0)
    def _():
        acc_ref[...] += part

    @pl.when(kk == pl.num_programs(1) - 1)
    def _():
        o_ref[...] = x_ref[...] + mod_ref[0, 5:6, :] * (_rms_rows(acc_ref[...]) * nw_ref[...])


def _ffn_down_call(gv2, cw, cb, w_bf, layer, x2, mod3, nw, tm, seq_len, mod_row, grid_mode):
    m = x2.shape[0]
    tk = 512
    nk = D_FF // tk
    hb = tm // GRID_W
    n_hblk = m // GRID_W
    tiles_per_seq = seq_len // tm
    return pl.pallas_call(
        functools.partial(_ffn_down_kernel, tm=tm, tiles_per_seq=tiles_per_seq, grid_mode=grid_mode),
        grid=(m // tm, D_FF // tk),
        in_specs=[
            pl.BlockSpec((GRID_W, tk), lambda i, k: (jnp.maximum(i * hb - 1, 0), k)),
            pl.BlockSpec((tm, tk), lambda i, k: (i, k)),
            pl.BlockSpec((GRID_W, tk), lambda i, k: (jnp.minimum((i + 1) * hb, n_hblk - 1), k)),
            pl.BlockSpec((tm, tk), lambda i, k: (i, nk + k)),
            pl.BlockSpec((3, 3, tk), lambda i, k: (0, 0, k)),
            pl.BlockSpec((1, tk), lambda i, k: (0, k)),
            pl.BlockSpec((None, tk, D_MODEL), lambda i, k: (layer, k, 0)),
            pl.BlockSpec((tm, D_MODEL), lambda i, k: (i, 0)),
            pl.BlockSpec((1, MOD_ROWS, D_MODEL), lambda i, k: (mod_row(i), 0, 0)),
            pl.BlockSpec((1, D_MODEL), lambda i, k: (0, 0)),
        ],
        out_specs=pl.BlockSpec((tm, D_MODEL), lambda i, k: (i, 0)),
        out_shape=jax.ShapeDtypeStruct((m, D_MODEL), F32),
        scratch_shapes=[
            pltpu.VMEM((tm, D_MODEL), F32),
            pltpu.VMEM((tm + 2 * GRID_W + 2 * EXT_PAD, tk), F32),
        ],
        compiler_params=_cparams(("arbitrary", "arbitrary")),
        name="ffn_down",
    )(gv2, gv2, gv2, gv2, cw, cb.reshape(1, -1), w_bf, x2, mod3, nw)


def _stream(x3, mod3, l, params, s0, consts, tiles, tl, row_of, grid_mode, full):
    bsz, length, _ = x3.shape
    m = bsz * length
    x2 = x3.reshape(m, D_MODEL)
    nw = params["norm_w"][l]
    tm, tm_out, tm_down = tiles
    tn = 512
    w_in = params["w_in"]
    r2, h2 = _norm_proj_call(x2, mod3, nw[0:1], w_in, l, 3 * HG_WIDTH, IN_COLS - 3 * HG_WIDTH, tm, tn,
                             row_of(tm), 0, "in_proj_vgab")
    q2 = _proj_call(h2, w_in, l, 0, HG_WIDTH, tm, tn, True, BF16, "in_proj_q")
    f2 = _proj_call(h2, w_in, l, HG_WIDTH, 2 * HG_WIDTH, tm, tn, False, F32, "in_proj_f")
    tri, level = consts
    of3, ob3, sfin = _scan_call(q2.reshape(bsz, length, -1), f2.reshape(bsz, length, -1),
                                r2.reshape(bsz, length, -1), params["lb_logits"], s0, tri, level, l, tl)
    if not full:
        return None, sfin
    cv3 = _cconv_call(r2.reshape(bsz, length, -1), params["conv_w"][l], params["conv_b"][l],
                      params["conv_ln_w"][l], params["conv_ln_b"][l])
    x2 = _outproj_call(of3.reshape(m, -1), ob3.reshape(m, -1), r2, cv3.reshape(m, -1), x2, mod3,
                       params["hg_norm_w"][l].reshape(1, -1), nw[1:2], params["w_out"], l, tm_out,
                       row_of(tm_out))
    gv2, _ = _norm_proj_call(x2, mod3, nw[2:3], params["ffn_up"], l, 0, 2 * D_FF, tm, tn, row_of(tm), 3,
                             "ffn_up")
    x2 = _ffn_down_call(gv2, params["ffn_conv_w"][l], params["ffn_conv_b"][l], params["ffn_down"], l,
                        x2, mod3, nw[3:4], tm_down, length, row_of(tm_down), grid_mode)
    return x2.reshape(bsz, length, D_MODEL), sfin


def kernel(x, c, ctx, c_ctx, w_mod, b_mod, norm_w, w_in, lb_logits, hg_norm_w, conv_w, conv_b, conv_ln_w,
           conv_ln_b, w_out, ffn_up, ffn_conv_w, ffn_conv_b, ffn_down):
    bsz, seq, _ = x.shape
    ctx_len = ctx.shape[1]
    params = dict(norm_w=norm_w, w_in=w_in.astype(BF16), lb_logits=lb_logits, hg_norm_w=hg_norm_w,
                  conv_w=conv_w, conv_b=conv_b, conv_ln_w=conv_ln_w, conv_ln_b=conv_ln_b,
                  w_out=w_out.astype(BF16), ffn_up=ffn_up.astype(BF16), ffn_conv_w=ffn_conv_w,
                  ffn_conv_b=ffn_conv_b, ffn_down=ffn_down.astype(BF16))
    tri_np, level_np = _scan_constants()
    consts = (jnp.asarray(tri_np, BF16), jnp.asarray(level_np, jnp.int32))

    cond = jnp.concatenate([c, c_ctx[None, :], jnp.zeros((MOD_ROWS - bsz - 1, D_MODEL), c.dtype)], axis=0)
    mod = _mod_call(_silu(cond).astype(BF16), w_mod, b_mod)
    mod = mod[:, :bsz + 1].reshape(DEPTH, bsz + 1, N_MOD, D_MODEL)
    mod = jnp.pad(mod, ((0, 0), (0, 0), (0, MOD_ROWS - N_MOD), (0, 0)))

    lat_row = lambda tm: (lambda i: i // (seq // tm))
    ctx_row = lambda tm: (lambda i: bsz)
    zero_state = jnp.zeros((bsz, HEADS, 2, HEAD_DIM, HEAD_DIM), F32)
    xc = ctx
    for l in range(DEPTH):
        last = l == DEPTH - 1
        xc, s_ctx = _stream(xc, mod[l], l, params, zero_state, consts, (bsz * ctx_len, ctx_len, ctx_len),
                            ctx_len, ctx_row, grid_mode=False, full=not last)
        x, _ = _stream(x, mod[l], l, params, s_ctx, consts, (1024, 256, 512), 512, lat_row,
                       grid_mode=True, full=True)
    return x
```

```python
import functools

import numpy as np
import jax
import jax.numpy as jnp
from jax import lax
from jax.experimental import pallas as pl
from jax.experimental.pallas import tpu as pltpu

F32 = jnp.float32
BF16 = jnp.bfloat16

D_MODEL = 2048
DEPTH = 2
GRID_W = 64
HEADS = 8
HEAD_DIM = 128
HG_WIDTH = HEADS * HEAD_DIM
CONV_CH = D_MODEL - HG_WIDTH
CONV_WIDTH = 31
CONV_PAD = 16
D_FF = 5632
N_MOD = 6
MOD_ROWS = 8
EPS = 1e-6
LN_EPS = 1e-5
LOG2_E = float(np.log2(np.e))
IN_COLS = 3 * HG_WIDTH + 2 * HG_WIDTH + 2 * CONV_CH

CHUNK = 64
N_LEVELS = 6

VMEM_LIMIT = 56 * 1024 * 1024


def _cparams(sem):
    return pltpu.CompilerParams(dimension_semantics=sem, vmem_limit_bytes=VMEM_LIMIT)


def _silu(x):
    return x * jax.nn.sigmoid(x)


def _rms_rows(x):
    return x * lax.rsqrt(jnp.mean(x * x, axis=-1, keepdims=True) + EPS)


def _mod_kernel(s_ref, w_ref, b_ref, o_ref):
    w = w_ref[0].astype(BF16)
    o_ref[0] = jnp.dot(s_ref[...], w, preferred_element_type=F32) + b_ref[0]


def _mod_call(s_rows, w_mod, b_mod):
    tn = 1024
    n = w_mod.shape[-1]
    return pl.pallas_call(
        _mod_kernel,
        grid=(DEPTH, n // tn),
        in_specs=[
            pl.BlockSpec((MOD_ROWS, D_MODEL), lambda l, j: (0, 0)),
            pl.BlockSpec((1, D_MODEL, tn), lambda l, j: (l, 0, j)),
            pl.BlockSpec((1, 1, tn), lambda l, j: (l, 0, j)),
        ],
        out_specs=pl.BlockSpec((1, MOD_ROWS, tn), lambda l, j: (l, 0, j)),
        out_shape=jax.ShapeDtypeStruct((DEPTH, MOD_ROWS, n), F32),
        compiler_params=_cparams(("arbitrary", "arbitrary")),
        name="mod_matmul",
    )(s_rows, w_mod, b_mod.reshape(DEPTH, 1, n))


def _prenorm(x_ref, mod_ref, nw_ref, h_ref, shift_row):
    y = _rms_rows(x_ref[...]) * nw_ref[...]
    h = y * (1.0 + mod_ref[0, shift_row + 1:shift_row + 2, :]) + mod_ref[0, shift_row:shift_row + 1, :]
    h_ref[...] = h.astype(BF16)


def _prenorm_kernel(x_ref, mod_ref, nw_ref, h_ref, *, shift_row):
    _prenorm(x_ref, mod_ref, nw_ref, h_ref, shift_row)


def _prenorm_call(x2, mod3, nw, tm, mod_row, shift_row, name):
    m = x2.shape[0]
    return pl.pallas_call(
        functools.partial(_prenorm_kernel, shift_row=shift_row),
        grid=(m // tm,),
        in_specs=[
            pl.BlockSpec((tm, D_MODEL), lambda i: (i, 0)),
            pl.BlockSpec((1, MOD_ROWS, D_MODEL), lambda i: (mod_row(i), 0, 0)),
            pl.BlockSpec((1, D_MODEL), lambda i: (0, 0)),
        ],
        out_specs=pl.BlockSpec((tm, D_MODEL), lambda i: (i, 0)),
        out_shape=jax.ShapeDtypeStruct((m, D_MODEL), BF16),
        compiler_params=_cparams(("arbitrary",)),
        name=name,
    )(x2, mod3, nw)


def _proj_kernel(h_ref, w_ref, o_ref, wb_ref, *, act):
    @pl.when(pl.program_id(1) == 0)
    def _():
        wb_ref[...] = w_ref[...].astype(BF16)

    acc = jnp.dot(h_ref[...], wb_ref[...], preferred_element_type=F32)
    if act:
        acc = _silu(acc)
    o_ref[...] = acc.astype(o_ref.dtype)


def _proj_call(h2, w, layer, col0, n_cols, tm, tn, act, out_dtype, name):
    m = h2.shape[0]
    c0 = col0 // tn
    return pl.pallas_call(
        functools.partial(_proj_kernel, act=act),
        grid=(n_cols // tn, m // tm),
        in_specs=[
            pl.BlockSpec((tm, D_MODEL), lambda j, i: (i, 0)),
            pl.BlockSpec((None, D_MODEL, tn), lambda j, i: (layer, 0, c0 + j)),
        ],
        out_specs=pl.BlockSpec((tm, tn), lambda j, i: (i, j)),
        out_shape=jax.ShapeDtypeStruct((m, n_cols), out_dtype),
        scratch_shapes=[pltpu.VMEM((D_MODEL, tn), BF16)],
        compiler_params=_cparams(("arbitrary", "arbitrary")),
        name=name,
    )(h2, w)


def _scan_constants():
    c = CHUNK
    t = np.arange(c)[:, None]
    r = np.arange(c)[None, :]
    tri = np.stack([np.tile(r <= t, (1, 3)), np.tile(r >= t, (1, 3))]).astype(np.float32)
    level = np.full((2, c, c), -1, np.int32)
    for lvl in range(N_LEVELS):
        half = c >> (lvl + 1)
        same = (t // (2 * half)) == (r // (2 * half))
        t_late = (t % (2 * half)) >= half
        r_late = (r % (2 * half)) >= half
        level[0][same & t_late & ~r_late] = lvl
        level[1][same & ~t_late & r_late] = lvl
    level[:, np.arange(c), np.arange(c)] = N_LEVELS
    return tri, level


_NT = (((1,), (1,)), ((), ()))
_TN = (((0,), (0,)), ((), ()))


def _neg_abs(x):
    bits = lax.bitcast_convert_type(x, jnp.int32) | jnp.int32(-2 ** 31)
    return lax.bitcast_convert_type(bits, F32)


def _scan_kernel(lbl_ref, tri_ref, level_ref, qf_ref, qb_ref, vf_ref, vb_ref, ff_ref, fb_ref,
                 s0_ref, of_ref, ob_ref, sfin_ref, st_ref, *, layer, n_chunks):
    c = CHUNK
    step = pl.program_id(2)

    @pl.when(step == 0)
    def _():
        st_ref[...] = s0_ref[0, 0]

    rows = [lbl_ref[:, j, :] for j in range(DEPTH)]
    mx = functools.reduce(jnp.maximum, rows)
    es = [jnp.exp(rw - mx) for rw in rows]
    tot = functools.reduce(lambda a, b: a + b, es)
    lb = jnp.zeros_like(mx)
    for j in range(1, layer + 1):
        lb = lb + es[j] / tot

    tri = tri_ref[...]
    level = level_ref[...]
    sub = lax.broadcasted_iota(jnp.int32, (c // 8, 8, HEAD_DIM), 1)

    def ref_rows(b, half, d):
        pick = half - 1 if d == 0 else half
        if half >= 8:
            parts = []
            for p0 in range(0, c, 2 * half):
                parts.append(jnp.broadcast_to(b[p0 + pick:p0 + pick + 1, :], (2 * half, HEAD_DIM)))
            return parts[0] if len(parts) == 1 else jnp.concatenate(parts, axis=0)
        b3 = b.reshape(c // 8, 8, HEAD_DIM)
        out = None
        for g0 in range(0, 8, 2 * half):
            cand = jnp.broadcast_to(b3[:, g0 + pick:g0 + pick + 1, :], b3.shape)
            out = cand if out is None else jnp.where(sub >= g0, cand, out)
        return out.reshape(c, HEAD_DIM)

    refs = ((qf_ref, vf_ref, ff_ref, of_ref), (qb_ref, vb_ref, fb_ref, ob_ref))

    def gates(d, row0):
        raw = refs[d][2][0, row0:row0 + c, :]
        lbd = lb[d:d + 1, :]
        f = lbd + (1.0 - lbd) * jax.nn.sigmoid(raw)
        logf = jnp.log(f)
        hi = logf.astype(BF16)
        rem = logf - hi.astype(F32)
        mid = rem.astype(BF16)
        lo = (rem - mid.astype(F32)).astype(BF16)
        b = jnp.dot(tri[d], jnp.concatenate([hi, mid, lo], axis=0), preferred_element_type=F32)
        return (1.0 - f).astype(BF16), b * LOG2_E

    def intra(d, row0, k_bf, b):
        q_bf = refs[d][0][0, row0:row0 + c, :]
        att = jnp.where(level[d] == N_LEVELS,
                        lax.dot_general(q_bf, k_bf, _NT, preferred_element_type=F32), 0.0)
        for lvl in range(N_LEVELS):
            half = c >> (lvl + 1)
            el = jnp.exp2(_neg_abs(b - ref_rows(b, half, d))).astype(BF16)
            att = jnp.where(level[d] == lvl,
                            lax.dot_general(q_bf * el, k_bf * el, _NT, preferred_element_type=F32), att)
        last = c - 1 if d == 0 else 0
        blast = b[last:last + 1, :]
        q_in = q_bf * jnp.exp2(b).astype(BF16)
        k_out = k_bf * jnp.exp2(blast - b).astype(BF16)
        return att.astype(BF16), q_in, k_out, jnp.exp2(blast)

    def readout(d, row0, att, q_in, k_out, decay, st):
        v = refs[d][1][0, row0:row0 + c, :]
        o = lax.dot_general(q_in, st.astype(BF16), _NT, preferred_element_type=F32)
        refs[d][3][0, row0:row0 + c, :] = o + jnp.dot(att, v, preferred_element_type=F32)
        return st * decay + lax.dot_general(v, k_out, _TN, preferred_element_type=F32)

    work = []
    for ci in range(n_chunks):
        work.append((0, ci * c))
        work.append((1, (n_chunks - 1 - ci) * c))
    st = [st_ref[0], st_ref[1]]
    stage1 = {}
    stage2 = {}
    for n in range(len(work) + 2):
        if n >= 2:
            d, row0 = work[n - 2]
            st[d] = readout(d, row0, *stage2.pop(n - 2), st[d])
        if 1 <= n <= len(work):
            d, row0 = work[n - 1]
            stage2[n - 1] = intra(d, row0, *stage1.pop(n - 1))
        if n < len(work):
            stage1[n] = gates(*work[n])
    st_ref[0] = st[0]
    st_ref[1] = st[1]

    @pl.when(step == pl.num_programs(2) - 1)
    def _():
        sfin_ref[0, 0] = st_ref[...]


def _scan_call(q3, f3, r3, lb_logits, s0, tri, level, layer, tl):
    bsz, length, _ = q3.shape
    nl = length // tl
    fwd = lambda b, h, i: (b, i, h)
    bwd = lambda b, h, i: (b, nl - 1 - i, h)
    bwd_f = lambda b, h, i: (b, nl - 1 - i, HEADS + h)
    blk = (1, tl, HEAD_DIM)
    return pl.pallas_call(
        functools.partial(_scan_kernel, layer=layer, n_chunks=tl // CHUNK),
        grid=(bsz, HEADS, nl),
        in_specs=[
            pl.BlockSpec((2, DEPTH, HEAD_DIM), lambda b, h, i: (0, 0, h)),
            pl.BlockSpec(tri.shape, lambda b, h, i: (0, 0, 0)),
            pl.BlockSpec(level.shape, lambda b, h, i: (0, 0, 0)),
            pl.BlockSpec(blk, fwd),
            pl.BlockSpec(blk, bwd),
            pl.BlockSpec(blk, fwd),
            pl.BlockSpec(blk, bwd),
            pl.BlockSpec(blk, fwd),
            pl.BlockSpec(blk, bwd_f),
            pl.BlockSpec((1, 1, 2, HEAD_DIM, HEAD_DIM), lambda b, h, i: (b, h, 0, 0, 0)),
        ],
        out_specs=[
            pl.BlockSpec(blk, fwd),
            pl.BlockSpec(blk, bwd),
            pl.BlockSpec((1, 1, 2, HEAD_DIM, HEAD_DIM), lambda b, h, i: (b, h, 0, 0, 0)),
        ],
        out_shape=[
            jax.ShapeDtypeStruct((bsz, length, HG_WIDTH), F32),
            jax.ShapeDtypeStruct((bsz, length, HG_WIDTH), F32),
            jax.ShapeDtypeStruct((bsz, HEADS, 2, HEAD_DIM, HEAD_DIM), F32),
        ],
        scratch_shapes=[pltpu.VMEM((2, HEAD_DIM, HEAD_DIM), F32)],
        compiler_params=_cparams(("arbitrary", "arbitrary", "arbitrary")),
        name="hgrn2_scan",
    )(lb_logits, tri, level, q3, q3, r3, r3, f3, f3, s0)


def _cconv_kernel(a_ref, b_ref, cw_ref, cb_ref, lw_ref, lb_ref, o_ref, u_ref, *, length, rows):
    zeros = jnp.zeros((CONV_PAD, HEAD_DIM), F32)
    u_ref[0:CONV_PAD, :] = zeros
    u_ref[CONV_PAD + length:2 * CONV_PAD + length, :] = zeros
    n_tiles = length // rows

    def fill(ti, carry):
        r0 = pl.multiple_of(ti * rows, rows)
        a = a_ref[0, pl.ds(r0, rows), :].astype(F32)
        b = b_ref[0, pl.ds(r0, rows), :].astype(F32)
        u_ref[pl.ds(CONV_PAD + r0, rows), :] = a * jax.nn.sigmoid(b)
        return carry

    lax.fori_loop(0, n_tiles, fill, 0)

    half = CONV_WIDTH // 2

    def conv(ti, carry):
        r0 = pl.multiple_of(ti * rows, rows)
        lead = CONV_PAD - half
        span = -(-(lead + CONV_WIDTH - 1) // 8) * 8 - 8
        acc = jnp.zeros((rows, HEAD_DIM), F32)
        for phase in range(8):
            window = u_ref[pl.ds(r0 + phase, rows + span), :]
            for off in range(phase, lead + CONV_WIDTH, 8):
                if off >= lead:
                    j = off - lead
                    acc = acc + cw_ref[j:j + 1, :] * window[off - phase:off - phase + rows]
        acc = acc + cb_ref[...]
        mu = jnp.mean(acc, axis=-1, keepdims=True)
        dlt = acc - mu
        var = jnp.mean(dlt * dlt, axis=-1, keepdims=True)
        y = dlt * lax.rsqrt(var + LN_EPS) * lw_ref[...] + lb_ref[...]
        o_ref[0, pl.ds(r0, rows), :] = _silu(y).astype(BF16)
        return carry

    lax.fori_loop(0, n_tiles, conv, 0, unroll=4)


def _cconv_call(r3, cw, cb, lw, lb):
    bsz, length, _ = r3.shape
    groups = CONV_CH // HEAD_DIM
    a_col = 2 * HG_WIDTH // HEAD_DIM
    b_col = a_col + groups
    vec = pl.BlockSpec((1, HEAD_DIM), lambda b, g: (0, g))
    return pl.pallas_call(
        functools.partial(_cconv_kernel, length=length, rows=128),
        grid=(bsz, groups),
        in_specs=[
            pl.BlockSpec((1, length, HEAD_DIM), lambda b, g: (b, 0, a_col + g)),
            pl.BlockSpec((1, length, HEAD_DIM), lambda b, g: (b, 0, b_col + g)),
            pl.BlockSpec((CONV_WIDTH, HEAD_DIM), lambda b, g: (0, g)),
            vec, vec, vec,
        ],
        out_specs=pl.BlockSpec((1, length, HEAD_DIM), lambda b, g: (b, 0, g)),
        out_shape=jax.ShapeDtypeStruct((bsz, length, CONV_CH), BF16),
        scratch_shapes=[pltpu.VMEM((length + 2 * CONV_PAD, HEAD_DIM), F32)],
        compiler_params=_cparams(("arbitrary", "arbitrary")),
        name="conformer_conv",
    )(r3, r3, cw, cb.reshape(1, -1), lw.reshape(1, -1), lb.reshape(1, -1))


def _outproj_kernel(of_ref, ob_ref, g_ref, cv_ref, x_ref, mod_ref, hgw_ref, nw_ref, w_ref, o_ref):
    o = of_ref[...] + ob_ref[...]
    heads = []
    for h in range(HEADS):
        oh = o[:, h * HEAD_DIM:(h + 1) * HEAD_DIM]
        heads.append(_rms_rows(oh) * hgw_ref[...])
    r = jnp.concatenate(heads, axis=-1) * _silu(g_ref[...].astype(F32))
    y = jnp.dot(r.astype(BF16), w_ref[0:HG_WIDTH, :], preferred_element_type=F32)
    y = y + jnp.dot(cv_ref[...], w_ref[HG_WIDTH:D_MODEL, :], preferred_element_type=F32)
    o_ref[...] = x_ref[...] + mod_ref[0, 2:3, :] * (_rms_rows(y) * nw_ref[...])


def _outproj_call(of2, ob2, r2, cv2, x2, mod3, hgw, nw, w_bf, layer, tm, mod_row):
    m = x2.shape[0]
    return pl.pallas_call(
        _outproj_kernel,
        grid=(m // tm,),
        in_specs=[
            pl.BlockSpec((tm, HG_WIDTH), lambda i: (i, 0)),
            pl.BlockSpec((tm, HG_WIDTH), lambda i: (i, 0)),
            pl.BlockSpec((tm, HG_WIDTH), lambda i: (i, 1)),
            pl.BlockSpec((tm, CONV_CH), lambda i: (i, 0)),
            pl.BlockSpec((tm, D_MODEL), lambda i: (i, 0)),
            pl.BlockSpec((1, MOD_ROWS, D_MODEL), lambda i: (mod_row(i), 0, 0)),
            pl.BlockSpec((1, HEAD_DIM), lambda i: (0, 0)),
            pl.BlockSpec((1, D_MODEL), lambda i: (0, 0)),
            pl.BlockSpec((None, D_MODEL, D_MODEL), lambda i: (layer, 0, 0)),
        ],
        out_specs=pl.BlockSpec((tm, D_MODEL), lambda i: (i, 0)),
        out_shape=jax.ShapeDtypeStruct((m, D_MODEL), F32),
        compiler_params=_cparams(("arbitrary",)),
        name="out_proj",
    )(of2, ob2, r2, cv2, x2, mod3, hgw, nw, w_bf)


EXT_PAD = 8
ROW_BLOCK = 16
MXU_COLS = 256


def _gate_chunk(gp_ref, gm_ref, gn_ref, val_ref, cw_ref, cb_ref, ext_ref, act_out, i, mm_tile, mm_tiles,
                *, tm, tiles_per_seq, grid_mode):
    tk = gm_ref.shape[1]
    base = EXT_PAD + GRID_W
    first = (i % tiles_per_seq) == 0
    last = (i % tiles_per_seq) == tiles_per_seq - 1
    zpad = jnp.zeros((EXT_PAD, tk), F32)
    ext_ref[0:EXT_PAD, :] = zpad
    ext_ref[base + tm + GRID_W:base + tm + GRID_W + EXT_PAD, :] = zpad
    ext_ref[EXT_PAD:base, :] = jnp.where(first, 0.0, gp_ref[...].astype(F32))
    ext_ref[base:base + tm, :] = gm_ref[...].astype(F32)
    ext_ref[base + tm:base + tm + GRID_W, :] = jnp.where(last, 0.0, gn_ref[...].astype(F32))

    rows = ROW_BLOCK
    row_id = lax.broadcasted_iota(jnp.int32, (rows, tk), 0)
    taps = [[0.5 * cw_ref[dh, dw:dw + 1, :] for dw in range(3)] for dh in range(3)]
    bias = 0.5 * cb_ref[...]

    n_blocks = tm // rows
    for r0 in range(0, tm, rows):
        blk = r0 // rows
        for n in range(blk * mm_tiles // n_blocks, (blk + 1) * mm_tiles // n_blocks):
            mm_tile(n)

        def tap(dh, dw):
            return taps[dh][dw] * ext_ref[base + r0 + (dh - 1) * GRID_W + (dw - 1):
                                          base + r0 + (dh - 1) * GRID_W + (dw - 1) + rows, :]

        if grid_mode:
            left = tap(0, 0) + tap(1, 0) + tap(2, 0)
            right = tap(0, 2) + tap(1, 2) + tap(2, 2)
            if r0 % GRID_W == 0:
                left = jnp.where(row_id == 0, 0.0, left)
            if (r0 + rows) % GRID_W == 0:
                right = jnp.where(row_id == rows - 1, 0.0, right)
            conv = tap(0, 1) + tap(1, 1) + tap(2, 1) + left + right
        else:
            conv = tap(1, 0) + tap(1, 1) + tap(1, 2)
        half = conv + bias
        gelu = half * (1.0 + lax.erf(half * np.float32(np.sqrt(2.0))))
        act_out[r0:r0 + rows, :] = gelu.astype(BF16) * val_ref[r0:r0 + rows, :]


def _ffn_down_kernel(gp_ref, gm_ref, gn_ref, val_ref, cw_ref, cb_ref, w_ref, x_ref, mod_ref, nw_ref,
                     o_ref, acc_ref, ext_ref, act0_ref, act1_ref, *, tm, tiles_per_seq, grid_mode):
    i = pl.program_id(0)
    kk = pl.program_id(1)

    @pl.when(kk == 0)
    def _():
        acc_ref[...] = jnp.zeros_like(acc_ref)
        act1_ref[...] = jnp.zeros_like(act1_ref)

    def step(act_in, act_out):
        def mm_tile(n):
            cols = slice(n * MXU_COLS, (n + 1) * MXU_COLS)
            acc_ref[:, cols] += jnp.dot(act_in[...], w_ref[:, cols], preferred_element_type=F32)

        _gate_chunk(gp_ref, gm_ref, gn_ref, val_ref, cw_ref, cb_ref, ext_ref, act_out, i, mm_tile,
                    D_MODEL // MXU_COLS, tm=tm, tiles_per_seq=tiles_per_seq, grid_mode=grid_mode)

    @pl.when(kk % 2 == 0)
    def _():
        step(act1_ref, act0_ref)

    @pl.when(kk % 2 == 1)
    def _():
        step(act0_ref, act1_ref)

    @pl.when(kk == pl.num_programs(1) - 1)
    def _():
        o_ref[...] = x_ref[...] + mod_ref[0, 5:6, :] * (_rms_rows(acc_ref[...]) * nw_ref[...])


def _ffn_down_call(gv2, cw, cb, w_bf, layer, x2, mod3, nw, tm, seq_len, mod_row, grid_mode):
    m = x2.shape[0]
    tk = 512
    nk = D_FF // tk
    hb = tm // GRID_W
    n_hblk = m // GRID_W
    tiles_per_seq = seq_len // tm
    cur = lambda k: jnp.minimum(k, nk - 1)
    return pl.pallas_call(
        functools.partial(_ffn_down_kernel, tm=tm, tiles_per_seq=tiles_per_seq, grid_mode=grid_mode),
        grid=(m // tm, nk + 1),
        in_specs=[
            pl.BlockSpec((GRID_W, tk), lambda i, k: (jnp.maximum(i * hb - 1, 0), cur(k))),
            pl.BlockSpec((tm, tk), lambda i, k: (i, cur(k))),
            pl.BlockSpec((GRID_W, tk), lambda i, k: (jnp.minimum((i + 1) * hb, n_hblk - 1), cur(k))),
            pl.BlockSpec((tm, tk), lambda i, k: (i, nk + cur(k))),
            pl.BlockSpec((3, 3, tk), lambda i, k: (0, 0, cur(k))),
            pl.BlockSpec((1, tk), lambda i, k: (0, cur(k))),
            pl.BlockSpec((None, tk, D_MODEL), lambda i, k: (layer, jnp.maximum(k - 1, 0), 0)),
            pl.BlockSpec((tm, D_MODEL), lambda i, k: (i, 0)),
            pl.BlockSpec((1, MOD_ROWS, D_MODEL), lambda i, k: (mod_row(i), 0, 0)),
            pl.BlockSpec((1, D_MODEL), lambda i, k: (0, 0)),
        ],
        out_specs=pl.BlockSpec((tm, D_MODEL), lambda i, k: (i, 0)),
        out_shape=jax.ShapeDtypeStruct((m, D_MODEL), F32),
        scratch_shapes=[
            pltpu.VMEM((tm, D_MODEL), F32),
            pltpu.VMEM((tm + 2 * GRID_W + 2 * EXT_PAD, tk), F32),
            pltpu.VMEM((tm, tk), BF16),
            pltpu.VMEM((tm, tk), BF16),
        ],
        compiler_params=_cparams(("arbitrary", "arbitrary")),
        name="ffn_down",
    )(gv2, gv2, gv2, gv2, cw, cb.reshape(1, -1), w_bf, x2, mod3, nw)


def _stream(x3, mod3, l, params, s0, consts, tiles, tl, row_of, grid_mode, full):
    bsz, length, _ = x3.shape
    m = bsz * length
    x2 = x3.reshape(m, D_MODEL)
    nw = params["norm_w"][l]
    tm, tm_norm, tm_out, tm_down = tiles
    tn = 1024
    w_in = params["w_in"]
    h2 = _prenorm_call(x2, mod3, nw[0:1], tm_norm, row_of(tm_norm), 0, "mix_prenorm")
    q2 = _proj_call(h2, w_in, l, 0, HG_WIDTH, tm, tn, True, BF16, "in_proj_q")
    f2 = _proj_call(h2, w_in, l, HG_WIDTH, 2 * HG_WIDTH, tm, tn, False, F32, "in_proj_f")
    r2 = _proj_call(h2, w_in, l, 3 * HG_WIDTH, IN_COLS - 3 * HG_WIDTH, tm, tn, False, BF16, "in_proj_vgab")
    tri, level = consts
    of3, ob3, sfin = _scan_call(q2.reshape(bsz, length, -1), f2.reshape(bsz, length, -1),
                                r2.reshape(bsz, length, -1), params["lb_logits"], s0, tri, level, l, tl)
    if not full:
        return None, sfin
    cv3 = _cconv_call(r2.reshape(bsz, length, -1), params["conv_w"][l], params["conv_b"][l],
                      params["conv_ln_w"][l], params["conv_ln_b"][l])
    x2 = _outproj_call(of3.reshape(m, -1), ob3.reshape(m, -1), r2, cv3.reshape(m, -1), x2, mod3,
                       params["hg_norm_w"][l].reshape(1, -1), nw[1:2], params["w_out"], l, tm_out,
                       row_of(tm_out))
    h2 = _prenorm_call(x2, mod3, nw[2:3], tm_norm, row_of(tm_norm), 3, "ffn_prenorm")
    gv2 = _proj_call(h2, params["ffn_up"], l, 0, 2 * D_FF, tm, tn, False, BF16, "ffn_up")
    x2 = _ffn_down_call(gv2, params["ffn_conv_w"][l], params["ffn_conv_b"][l], params["ffn_down"], l,
                        x2, mod3, nw[3:4], tm_down, length, row_of(tm_down), grid_mode)
    return x2.reshape(bsz, length, D_MODEL), sfin


def kernel(x, c, ctx, c_ctx, w_mod, b_mod, norm_w, w_in, lb_logits, hg_norm_w, conv_w, conv_b, conv_ln_w,
           conv_ln_b, w_out, ffn_up, ffn_conv_w, ffn_conv_b, ffn_down):
    bsz, seq, _ = x.shape
    ctx_len = ctx.shape[1]
    params = dict(norm_w=norm_w, w_in=w_in, lb_logits=lb_logits, hg_norm_w=hg_norm_w,
                  conv_w=conv_w, conv_b=conv_b, conv_ln_w=conv_ln_w, conv_ln_b=conv_ln_b,
                  w_out=w_out.astype(BF16), ffn_up=ffn_up, ffn_conv_w=ffn_conv_w,
                  ffn_conv_b=ffn_conv_b, ffn_down=ffn_down.astype(BF16))
    tri_np, level_np = _scan_constants()
    consts = (jnp.asarray(tri_np, BF16), jnp.asarray(level_np, jnp.int32))

    cond = jnp.concatenate([c, c_ctx[None, :], jnp.zeros((MOD_ROWS - bsz - 1, D_MODEL), c.dtype)], axis=0)
    mod = _mod_call(_silu(cond).astype(BF16), w_mod, b_mod)
    mod = mod[:, :bsz + 1].reshape(DEPTH, bsz + 1, N_MOD, D_MODEL)
    mod = jnp.pad(mod, ((0, 0), (0, 0), (0, MOD_ROWS - N_MOD), (0, 0)))

    lat_row = lambda tm: (lambda i: i // (seq // tm))
    ctx_row = lambda tm: (lambda i: bsz)
    zero_state = jnp.zeros((bsz, HEADS, 2, HEAD_DIM, HEAD_DIM), F32)
    xc = ctx
    for l in range(DEPTH):
        last = l == DEPTH - 1
        xc, s_ctx = _stream(xc, mod[l], l, params, zero_state, consts,
                            (bsz * ctx_len, ctx_len, ctx_len, ctx_len), ctx_len, ctx_row,
                            grid_mode=False, full=not last)
        x, _ = _stream(x, mod[l], l, params, s_ctx, consts, (1024, 512, 256, 512), 512, lat_row,
                       grid_mode=True, full=True)
    return x
```

```python
import functools

import numpy as np
import jax
import jax.numpy as jnp
from jax import lax
from jax.experimental import pallas as pl
from jax.experimental.pallas import tpu as pltpu

F32 = jnp.float32
BF16 = jnp.bfloat16

D_MODEL = 2048
DEPTH = 2
GRID_W = 64
HEADS = 8
HEAD_DIM = 128
HG_WIDTH = HEADS * HEAD_DIM
CONV_CH = D_MODEL - HG_WIDTH
CONV_WIDTH = 31
CONV_PAD = 16
D_FF = 5632
N_MOD = 6
MOD_ROWS = 8
EPS = 1e-6
LN_EPS = 1e-5
LOG2_E = float(np.log2(np.e))
IN_COLS = 3 * HG_WIDTH + 2 * HG_WIDTH + 2 * CONV_CH

CHUNK = 64
N_LEVELS = 6

VMEM_LIMIT = 56 * 1024 * 1024


def _cparams(sem):
    return pltpu.CompilerParams(dimension_semantics=sem, vmem_limit_bytes=VMEM_LIMIT)


def _silu(x):
    return x * jax.nn.sigmoid(x)


def _rms_rows(x):
    return x * lax.rsqrt(jnp.mean(x * x, axis=-1, keepdims=True) + EPS)


def _mod_kernel(s_ref, w_ref, b_ref, o_ref):
    w = w_ref[0].astype(BF16)
    o_ref[0] = jnp.dot(s_ref[...], w, preferred_element_type=F32) + b_ref[0]


def _mod_call(s_rows, w_mod, b_mod):
    tn = 1024
    n = w_mod.shape[-1]
    return pl.pallas_call(
        _mod_kernel,
        grid=(DEPTH, n // tn),
        in_specs=[
            pl.BlockSpec((MOD_ROWS, D_MODEL), lambda l, j: (0, 0)),
            pl.BlockSpec((1, D_MODEL, tn), lambda l, j: (l, 0, j)),
            pl.BlockSpec((1, 1, tn), lambda l, j: (l, 0, j)),
        ],
        out_specs=pl.BlockSpec((1, MOD_ROWS, tn), lambda l, j: (l, 0, j)),
        out_shape=jax.ShapeDtypeStruct((DEPTH, MOD_ROWS, n), F32),
        compiler_params=_cparams(("arbitrary", "arbitrary")),
        name="mod_matmul",
    )(s_rows, w_mod, b_mod.reshape(DEPTH, 1, n))


def _prenorm(x, mod_ref, nw_ref, shift_row):
    y = _rms_rows(x) * nw_ref[...]
    h = y * (1.0 + mod_ref[0, shift_row + 1:shift_row + 2, :]) + mod_ref[0, shift_row:shift_row + 1, :]
    return h.astype(BF16)


def _prenorm_kernel(x_ref, mod_ref, nw_ref, h_ref, *, shift_row):
    h_ref[...] = _prenorm(x_ref[...], mod_ref, nw_ref, shift_row)


def _prenorm_call(x2, mod3, nw, tm, mod_row, shift_row, name):
    m = x2.shape[0]
    return pl.pallas_call(
        functools.partial(_prenorm_kernel, shift_row=shift_row),
        grid=(m // tm,),
        in_specs=[
            pl.BlockSpec((tm, D_MODEL), lambda i: (i, 0)),
            pl.BlockSpec((1, MOD_ROWS, D_MODEL), lambda i: (mod_row(i), 0, 0)),
            pl.BlockSpec((1, D_MODEL), lambda i: (0, 0)),
        ],
        out_specs=pl.BlockSpec((tm, D_MODEL), lambda i: (i, 0)),
        out_shape=jax.ShapeDtypeStruct((m, D_MODEL), BF16),
        compiler_params=_cparams(("arbitrary",)),
        name=name,
    )(x2, mod3, nw)


def _proj_kernel(h_ref, w_ref, o_ref, wb_ref, *, act):
    @pl.when(pl.program_id(1) == 0)
    def _():
        wb_ref[...] = w_ref[...].astype(BF16)

    acc = jnp.dot(h_ref[...], wb_ref[...], preferred_element_type=F32)
    if act:
        acc = _silu(acc)
    o_ref[...] = acc.astype(o_ref.dtype)


def _proj_call(h2, w, layer, col0, n_cols, tm, tn, act, out_dtype, name):
    m = h2.shape[0]
    c0 = col0 // tn
    return pl.pallas_call(
        functools.partial(_proj_kernel, act=act),
        grid=(n_cols // tn, m // tm),
        in_specs=[
            pl.BlockSpec((tm, D_MODEL), lambda j, i: (i, 0)),
            pl.BlockSpec((None, D_MODEL, tn), lambda j, i: (layer, 0, c0 + j)),
        ],
        out_specs=pl.BlockSpec((tm, tn), lambda j, i: (i, j)),
        out_shape=jax.ShapeDtypeStruct((m, n_cols), out_dtype),
        scratch_shapes=[pltpu.VMEM((D_MODEL, tn), BF16)],
        compiler_params=_cparams(("arbitrary", "arbitrary")),
        name=name,
    )(h2, w)


def _scan_constants():
    c = CHUNK
    t = np.arange(c)[:, None]
    r = np.arange(c)[None, :]
    tri = np.stack([np.tile(r <= t, (1, 3)), np.tile(r >= t, (1, 3))]).astype(np.float32)
    level = np.full((2, c, c), -1, np.int32)
    for lvl in range(N_LEVELS):
        half = c >> (lvl + 1)
        same = (t // (2 * half)) == (r // (2 * half))
        t_late = (t % (2 * half)) >= half
        r_late = (r % (2 * half)) >= half
        level[0][same & t_late & ~r_late] = lvl
        level[1][same & ~t_late & r_late] = lvl
    level[:, np.arange(c), np.arange(c)] = N_LEVELS
    return tri, level


_NT = (((1,), (1,)), ((), ()))
_TN = (((0,), (0,)), ((), ()))


def _neg_abs(x):
    bits = lax.bitcast_convert_type(x, jnp.int32) | jnp.int32(-2 ** 31)
    return lax.bitcast_convert_type(bits, F32)


def _scan_kernel(lbl_ref, tri_ref, level_ref, qf_ref, qb_ref, vf_ref, vb_ref, ff_ref, fb_ref,
                 s0_ref, of_ref, ob_ref, sfin_ref, st_ref, *, layer, n_chunks):
    c = CHUNK
    step = pl.program_id(2)

    @pl.when(step == 0)
    def _():
        st_ref[...] = s0_ref[0, 0]

    rows = [lbl_ref[:, j, :] for j in range(DEPTH)]
    mx = functools.reduce(jnp.maximum, rows)
    es = [jnp.exp(rw - mx) for rw in rows]
    tot = functools.reduce(lambda a, b: a + b, es)
    lb = jnp.zeros_like(mx)
    for j in range(1, layer + 1):
        lb = lb + es[j] / tot

    tri = tri_ref[...]
    level = level_ref[...]
    sub = lax.broadcasted_iota(jnp.int32, (c // 8, 8, HEAD_DIM), 1)

    def ref_rows(b, half, d):
        pick = half - 1 if d == 0 else half
        if half >= 8:
            parts = []
            for p0 in range(0, c, 2 * half):
                parts.append(jnp.broadcast_to(b[p0 + pick:p0 + pick + 1, :], (2 * half, HEAD_DIM)))
            return parts[0] if len(parts) == 1 else jnp.concatenate(parts, axis=0)
        b3 = b.reshape(c // 8, 8, HEAD_DIM)
        out = None
        for g0 in range(0, 8, 2 * half):
            cand = jnp.broadcast_to(b3[:, g0 + pick:g0 + pick + 1, :], b3.shape)
            out = cand if out is None else jnp.where(sub >= g0, cand, out)
        return out.reshape(c, HEAD_DIM)

    refs = ((qf_ref, vf_ref, ff_ref, of_ref), (qb_ref, vb_ref, fb_ref, ob_ref))

    def gates(d, row0):
        raw = refs[d][2][0, row0:row0 + c, :]
        lbd = lb[d:d + 1, :]
        f = lbd + (1.0 - lbd) * jax.nn.sigmoid(raw)
        logf = jnp.log(f)
        hi = logf.astype(BF16)
        rem = logf - hi.astype(F32)
        mid = rem.astype(BF16)
        lo = (rem - mid.astype(F32)).astype(BF16)
        b = jnp.dot(tri[d], jnp.concatenate([hi, mid, lo], axis=0), preferred_element_type=F32)
        return (1.0 - f).astype(BF16), b * LOG2_E

    def intra(d, row0, k_bf, b):
        q_bf = refs[d][0][0, row0:row0 + c, :]
        att = jnp.where(level[d] == N_LEVELS,
                        lax.dot_general(q_bf, k_bf, _NT, preferred_element_type=F32), 0.0)
        for lvl in range(N_LEVELS):
            half = c >> (lvl + 1)
            el = jnp.exp2(_neg_abs(b - ref_rows(b, half, d))).astype(BF16)
            att = jnp.where(level[d] == lvl,
                            lax.dot_general(q_bf * el, k_bf * el, _NT, preferred_element_type=F32), att)
        last = c - 1 if d == 0 else 0
        blast = b[last:last + 1, :]
        q_in = q_bf * jnp.exp2(b).astype(BF16)
        k_out = k_bf * jnp.exp2(blast - b).astype(BF16)
        return att.astype(BF16), q_in, k_out, jnp.exp2(blast)

    def readout(d, row0, att, q_in, k_out, decay, st):
        v = refs[d][1][0, row0:row0 + c, :]
        o = lax.dot_general(q_in, st.astype(BF16), _NT, preferred_element_type=F32)
        refs[d][3][0, row0:row0 + c, :] = o + jnp.dot(att, v, preferred_element_type=F32)
        return st * decay + lax.dot_general(v, k_out, _TN, preferred_element_type=F32)

    work = []
    for ci in range(n_chunks):
        work.append((0, ci * c))
        work.append((1, (n_chunks - 1 - ci) * c))
    st = [st_ref[0], st_ref[1]]
    stage1 = {}
    stage2 = {}
    for n in range(len(work) + 2):
        if n >= 2:
            d, row0 = work[n - 2]
            st[d] = readout(d, row0, *stage2.pop(n - 2), st[d])
        if 1 <= n <= len(work):
            d, row0 = work[n - 1]
            stage2[n - 1] = intra(d, row0, *stage1.pop(n - 1))
        if n < len(work):
            stage1[n] = gates(*work[n])
    st_ref[0] = st[0]
    st_ref[1] = st[1]

    @pl.when(step == pl.num_programs(2) - 1)
    def _():
        sfin_ref[0, 0] = st_ref[...]


def _scan_call(q3, f3, r3, lb_logits, s0, tri, level, layer, tl):
    bsz, length, _ = q3.shape
    nl = length // tl
    fwd = lambda b, h, i: (b, i, h)
    bwd = lambda b, h, i: (b, nl - 1 - i, h)
    bwd_f = lambda b, h, i: (b, nl - 1 - i, HEADS + h)
    blk = (1, tl, HEAD_DIM)
    return pl.pallas_call(
        functools.partial(_scan_kernel, layer=layer, n_chunks=tl // CHUNK),
        grid=(bsz, HEADS, nl),
        in_specs=[
            pl.BlockSpec((2, DEPTH, HEAD_DIM), lambda b, h, i: (0, 0, h)),
            pl.BlockSpec(tri.shape, lambda b, h, i: (0, 0, 0)),
            pl.BlockSpec(level.shape, lambda b, h, i: (0, 0, 0)),
            pl.BlockSpec(blk, fwd),
            pl.BlockSpec(blk, bwd),
            pl.BlockSpec(blk, fwd),
            pl.BlockSpec(blk, bwd),
            pl.BlockSpec(blk, fwd),
            pl.BlockSpec(blk, bwd_f),
            pl.BlockSpec((1, 1, 2, HEAD_DIM, HEAD_DIM), lambda b, h, i: (b, h, 0, 0, 0)),
        ],
        out_specs=[
            pl.BlockSpec(blk, fwd),
            pl.BlockSpec(blk, bwd),
            pl.BlockSpec((1, 1, 2, HEAD_DIM, HEAD_DIM), lambda b, h, i: (b, h, 0, 0, 0)),
        ],
        out_shape=[
            jax.ShapeDtypeStruct((bsz, length, HG_WIDTH), F32),
            jax.ShapeDtypeStruct((bsz, length, HG_WIDTH), F32),
            jax.ShapeDtypeStruct((bsz, HEADS, 2, HEAD_DIM, HEAD_DIM), F32),
        ],
        scratch_shapes=[pltpu.VMEM((2, HEAD_DIM, HEAD_DIM), F32)],
        compiler_params=_cparams(("arbitrary", "arbitrary", "arbitrary")),
        name="hgrn2_scan",
    )(lb_logits, tri, level, q3, q3, r3, r3, f3, f3, s0)


def _cconv_kernel(a_ref, b_ref, cw_ref, cb_ref, lw_ref, lb_ref, o_ref, u_ref, *, length, rows):
    zeros = jnp.zeros((CONV_PAD, HEAD_DIM), F32)
    u_ref[0:CONV_PAD, :] = zeros
    u_ref[CONV_PAD + length:2 * CONV_PAD + length, :] = zeros
    n_tiles = length // rows

    def fill(ti, carry):
        r0 = pl.multiple_of(ti * rows, rows)
        a = a_ref[0, pl.ds(r0, rows), :].astype(F32)
        b = b_ref[0, pl.ds(r0, rows), :].astype(F32)
        u_ref[pl.ds(CONV_PAD + r0, rows), :] = a * jax.nn.sigmoid(b)
        return carry

    lax.fori_loop(0, n_tiles, fill, 0)

    half = CONV_WIDTH // 2

    def conv(ti, carry):
        r0 = pl.multiple_of(ti * rows, rows)
        lead = CONV_PAD - half
        span = -(-(lead + CONV_WIDTH - 1) // 8) * 8 - 8
        acc = jnp.zeros((rows, HEAD_DIM), F32)
        for phase in range(8):
            window = u_ref[pl.ds(r0 + phase, rows + span), :]
            for off in range(phase, lead + CONV_WIDTH, 8):
                if off >= lead:
                    j = off - lead
                    acc = acc + cw_ref[j:j + 1, :] * window[off - phase:off - phase + rows]
        acc = acc + cb_ref[...]
        mu = jnp.mean(acc, axis=-1, keepdims=True)
        dlt = acc - mu
        var = jnp.mean(dlt * dlt, axis=-1, keepdims=True)
        y = dlt * lax.rsqrt(var + LN_EPS) * lw_ref[...] + lb_ref[...]
        o_ref[0, pl.ds(r0, rows), :] = _silu(y).astype(BF16)
        return carry

    lax.fori_loop(0, n_tiles, conv, 0, unroll=4)


def _cconv_call(r3, cw, cb, lw, lb):
    bsz, length, _ = r3.shape
    groups = CONV_CH // HEAD_DIM
    a_col = 2 * HG_WIDTH // HEAD_DIM
    b_col = a_col + groups
    vec = pl.BlockSpec((1, HEAD_DIM), lambda b, g: (0, g))
    return pl.pallas_call(
        functools.partial(_cconv_kernel, length=length, rows=128),
        grid=(bsz, groups),
        in_specs=[
            pl.BlockSpec((1, length, HEAD_DIM), lambda b, g: (b, 0, a_col + g)),
            pl.BlockSpec((1, length, HEAD_DIM), lambda b, g: (b, 0, b_col + g)),
            pl.BlockSpec((CONV_WIDTH, HEAD_DIM), lambda b, g: (0, g)),
            vec, vec, vec,
        ],
        out_specs=pl.BlockSpec((1, length, HEAD_DIM), lambda b, g: (b, 0, g)),
        out_shape=jax.ShapeDtypeStruct((bsz, length, CONV_CH), BF16),
        scratch_shapes=[pltpu.VMEM((length + 2 * CONV_PAD, HEAD_DIM), F32)],
        compiler_params=_cparams(("arbitrary", "arbitrary")),
        name="conformer_conv",
    )(r3, r3, cw, cb.reshape(1, -1), lw.reshape(1, -1), lb.reshape(1, -1))


def _outproj_kernel(of_ref, ob_ref, g_ref, cv_ref, x_ref, mod_ref, hgw_ref, nw_ref, nw2_ref, w_ref, o_ref,
                    h_ref):
    o = of_ref[...] + ob_ref[...]
    heads = []
    for h in range(HEADS):
        oh = o[:, h * HEAD_DIM:(h + 1) * HEAD_DIM]
        heads.append(_rms_rows(oh) * hgw_ref[...])
    r = jnp.concatenate(heads, axis=-1) * _silu(g_ref[...].astype(F32))
    y = jnp.dot(r.astype(BF16), w_ref[0:HG_WIDTH, :], preferred_element_type=F32)
    y = y + jnp.dot(cv_ref[...], w_ref[HG_WIDTH:D_MODEL, :], preferred_element_type=F32)
    x_new = x_ref[...] + mod_ref[0, 2:3, :] * (_rms_rows(y) * nw_ref[...])
    o_ref[...] = x_new
    h_ref[...] = _prenorm(x_new, mod_ref, nw2_ref, 3)


def _outproj_call(of2, ob2, r2, cv2, x2, mod3, hgw, nw, nw2, w_bf, layer, tm, mod_row):
    m = x2.shape[0]
    return pl.pallas_call(
        _outproj_kernel,
        grid=(m // tm,),
        in_specs=[
            pl.BlockSpec((tm, HG_WIDTH), lambda i: (i, 0)),
            pl.BlockSpec((tm, HG_WIDTH), lambda i: (i, 0)),
            pl.BlockSpec((tm, HG_WIDTH), lambda i: (i, 1)),
            pl.BlockSpec((tm, CONV_CH), lambda i: (i, 0)),
            pl.BlockSpec((tm, D_MODEL), lambda i: (i, 0)),
            pl.BlockSpec((1, MOD_ROWS, D_MODEL), lambda i: (mod_row(i), 0, 0)),
            pl.BlockSpec((1, HEAD_DIM), lambda i: (0, 0)),
            pl.BlockSpec((1, D_MODEL), lambda i: (0, 0)),
            pl.BlockSpec((1, D_MODEL), lambda i: (0, 0)),
            pl.BlockSpec((None, D_MODEL, D_MODEL), lambda i: (layer, 0, 0)),
        ],
        out_specs=[
            pl.BlockSpec((tm, D_MODEL), lambda i: (i, 0)),
            pl.BlockSpec((tm, D_MODEL), lambda i: (i, 0)),
        ],
        out_shape=[
            jax.ShapeDtypeStruct((m, D_MODEL), F32),
            jax.ShapeDtypeStruct((m, D_MODEL), BF16),
        ],
        compiler_params=_cparams(("arbitrary",)),
        name="out_proj",
    )(of2, ob2, r2, cv2, x2, mod3, hgw, nw, nw2, w_bf)


ROW_BLOCK = 16
MXU_COLS = 256
FFN_CHUNK = 512
LANE_BLOCK = 256


def _gate_chunk(gp_ref, gm_ref, gn_ref, val_ref, cw_ref, cb_ref, ext_ref, act_out, first, last, mm_tile,
                mm_tiles, *, tm, grid_mode):
    tk = FFN_CHUNK
    rows = ROW_BLOCK
    lanes = LANE_BLOCK
    n_blocks = tm // rows
    total = n_blocks * (tk // lanes)
    row_id = lax.broadcasted_iota(jnp.int32, (rows, lanes), 0)
    edge = GRID_W if grid_mode else tm
    ext_ref[0:GRID_W, :] = jnp.where(first, 0.0, gp_ref[...].astype(F32))
    ext_ref[GRID_W:GRID_W + tm, :] = gm_ref[...].astype(F32)
    ext_ref[GRID_W + tm:2 * GRID_W + tm, :] = jnp.where(last, 0.0, gn_ref[...].astype(F32))

    done = 0
    for l0 in range(0, tk, lanes):
        ls = slice(l0, l0 + lanes)
        taps = 0.5 * cw_ref[:, :, ls]
        bias = 0.5 * cb_ref[:, ls]

        def columns(r0):
            ce = ext_ref[GRID_W + r0:GRID_W + r0 + rows, ls]
            if not grid_mode:
                return [taps[1, dw:dw + 1] * ce for dw in range(3)]
            up = ext_ref[r0:r0 + rows, ls]
            dn = ext_ref[2 * GRID_W + r0:2 * GRID_W + r0 + rows, ls]
            return [taps[0, dw:dw + 1] * up + taps[1, dw:dw + 1] * ce + taps[2, dw:dw + 1] * dn
                    for dw in range(3)]

        cur = columns(0)
        prev_left_row = None
        for blk in range(n_blocks):
            r0 = blk * rows
            for n in range(done * mm_tiles // total, (done + 1) * mm_tiles // total):
                mm_tile(n)
            done += 1
            nxt = columns(r0 + rows) if blk + 1 < n_blocks else None
            left, mid, right = cur
            from_prev = pltpu.roll(left, 1, axis=0)
            if r0 % edge == 0:
                from_prev = jnp.where(row_id == 0, 0.0, from_prev)
            else:
                from_prev = jnp.where(row_id == 0, prev_left_row, from_prev)
            from_next = pltpu.roll(right, rows - 1, axis=0)
            if (r0 + rows) % edge == 0:
                from_next = jnp.where(row_id == rows - 1, 0.0, from_next)
            else:
                from_next = jnp.where(row_id == rows - 1, nxt[2][0:1, :], from_next)
            half = mid + from_prev + from_next + bias
            gelu = half * (1.0 + lax.erf(half * np.float32(np.sqrt(2.0))))
            act_out[r0:r0 + rows, ls] = gelu.astype(BF16) * val_ref[r0:r0 + rows, ls]
            prev_left_row = left[rows - 1:rows, :]
            cur = nxt


def _ffn_down_kernel(gp_ref, gm_ref, gn_ref, val_ref, cw_ref, cb_ref, w_ref, x_ref, mod_ref, nw_ref,
                     o_ref, acc_ref, ext_ref, act0_ref, act1_ref, *, tm, tiles_per_seq, grid_mode):
    i = pl.program_id(0)
    kk = pl.program_id(1)
    first = (i % tiles_per_seq) == 0
    last = (i % tiles_per_seq) == tiles_per_seq - 1

    @pl.when(kk == 0)
    def _():
        acc_ref[...] = jnp.zeros_like(acc_ref)
        act1_ref[...] = jnp.zeros_like(act1_ref)

    def step(act_in, act_out):
        def mm_tile(n):
            cols = slice(n * MXU_COLS, (n + 1) * MXU_COLS)
            acc_ref[:, cols] += jnp.dot(act_in[...], w_ref[:, cols], preferred_element_type=F32)

        _gate_chunk(gp_ref, gm_ref, gn_ref, val_ref, cw_ref, cb_ref, ext_ref, act_out, first, last, mm_tile,
                    D_MODEL // MXU_COLS, tm=tm, grid_mode=grid_mode)

    @pl.when(kk % 2 == 0)
    def _():
        step(act1_ref, act0_ref)

    @pl.when(kk % 2 == 1)
    def _():
        step(act0_ref, act1_ref)

    @pl.when(kk == pl.num_programs(1) - 1)
    def _():
        o_ref[...] = x_ref[...] + mod_ref[0, 5:6, :] * (_rms_rows(acc_ref[...]) * nw_ref[...])


def _ffn_down_call(gv2, cw, cb, w_bf, layer, x2, mod3, nw, tm, seq_len, mod_row, grid_mode):
    m = x2.shape[0]
    tk = FFN_CHUNK
    nk = D_FF // tk
    hb = tm // GRID_W
    n_hblk = m // GRID_W
    tiles_per_seq = seq_len // tm
    cur = lambda k: jnp.minimum(k, nk - 1)
    return pl.pallas_call(
        functools.partial(_ffn_down_kernel, tm=tm, tiles_per_seq=tiles_per_seq, grid_mode=grid_mode),
        grid=(m // tm, nk + 1),
        in_specs=[
            pl.BlockSpec((GRID_W, tk), lambda i, k: (jnp.maximum(i * hb - 1, 0), cur(k))),
            pl.BlockSpec((tm, tk), lambda i, k: (i, cur(k))),
            pl.BlockSpec((GRID_W, tk), lambda i, k: (jnp.minimum((i + 1) * hb, n_hblk - 1), cur(k))),
            pl.BlockSpec((tm, tk), lambda i, k: (i, nk + cur(k))),
            pl.BlockSpec((3, 3, tk), lambda i, k: (0, 0, cur(k))),
            pl.BlockSpec((1, tk), lambda i, k: (0, cur(k))),
            pl.BlockSpec((None, tk, D_MODEL), lambda i, k: (layer, jnp.maximum(k - 1, 0), 0)),
            pl.BlockSpec((tm, D_MODEL), lambda i, k: (i, 0)),
            pl.BlockSpec((1, MOD_ROWS, D_MODEL), lambda i, k: (mod_row(i), 0, 0)),
            pl.BlockSpec((1, D_MODEL), lambda i, k: (0, 0)),
        ],
        out_specs=pl.BlockSpec((tm, D_MODEL), lambda i, k: (i, 0)),
        out_shape=jax.ShapeDtypeStruct((m, D_MODEL), F32),
        scratch_shapes=[
            pltpu.VMEM((tm, D_MODEL), F32),
            pltpu.VMEM((tm + 2 * GRID_W, tk), F32),
            pltpu.VMEM((tm, tk), BF16),
            pltpu.VMEM((tm, tk), BF16),
        ],
        compiler_params=_cparams(("arbitrary", "arbitrary")),
        name="ffn_down",
    )(gv2, gv2, gv2, gv2, cw, cb.reshape(1, -1), w_bf, x2, mod3, nw)


def _stream(x3, mod3, l, params, s0, consts, tiles, tl, row_of, grid_mode, full):
    bsz, length, _ = x3.shape
    m = bsz * length
    x2 = x3.reshape(m, D_MODEL)
    nw = params["norm_w"][l]
    tm, tm_norm, tm_out, tm_down = tiles
    tn = 1024
    w_in = params["w_in"]
    h2 = _prenorm_call(x2, mod3, nw[0:1], tm_norm, row_of(tm_norm), 0, "mix_prenorm")
    q2 = _proj_call(h2, w_in, l, 0, HG_WIDTH, tm, tn, True, BF16, "in_proj_q")
    f2 = _proj_call(h2, w_in, l, HG_WIDTH, 2 * HG_WIDTH, tm, tn, False, F32, "in_proj_f")
    r2 = _proj_call(h2, w_in, l, 3 * HG_WIDTH, IN_COLS - 3 * HG_WIDTH, tm, tn, False, BF16, "in_proj_vgab")
    tri, level = consts
    of3, ob3, sfin = _scan_call(q2.reshape(bsz, length, -1), f2.reshape(bsz, length, -1),
                                r2.reshape(bsz, length, -1), params["lb_logits"], s0, tri, level, l, tl)
    if not full:
        return None, sfin
    cv3 = _cconv_call(r2.reshape(bsz, length, -1), params["conv_w"][l], params["conv_b"][l],
                      params["conv_ln_w"][l], params["conv_ln_b"][l])
    x2, h2 = _outproj_call(of3.reshape(m, -1), ob3.reshape(m, -1), r2, cv3.reshape(m, -1), x2, mod3,
                           params["hg_norm_w"][l].reshape(1, -1), nw[1:2], nw[2:3], params["w_out"], l,
                           tm_out, row_of(tm_out))
    gv2 = _proj_call(h2, params["ffn_up"], l, 0, 2 * D_FF, tm, tn, False, BF16, "ffn_up")
    x2 = _ffn_down_call(gv2, params["ffn_conv_w"][l], params["ffn_conv_b"][l], params["ffn_down"], l,
                        x2, mod3, nw[3:4], tm_down, length, row_of(tm_down), grid_mode)
    return x2.reshape(bsz, length, D_MODEL), sfin


def kernel(x, c, ctx, c_ctx, w_mod, b_mod, norm_w, w_in, lb_logits, hg_norm_w, conv_w, conv_b, conv_ln_w,
           conv_ln_b, w_out, ffn_up, ffn_conv_w, ffn_conv_b, ffn_down):
    bsz, seq, _ = x.shape
    ctx_len = ctx.shape[1]
    params = dict(norm_w=norm_w, w_in=w_in, lb_logits=lb_logits, hg_norm_w=hg_norm_w,
                  conv_w=conv_w, conv_b=conv_b, conv_ln_w=conv_ln_w, conv_ln_b=conv_ln_b,
                  w_out=w_out.astype(BF16), ffn_up=ffn_up, ffn_conv_w=ffn_conv_w,
                  ffn_conv_b=ffn_conv_b, ffn_down=ffn_down.astype(BF16))
    tri_np, level_np = _scan_constants()
    consts = (jnp.asarray(tri_np, BF16), jnp.asarray(level_np, jnp.int32))

    cond = jnp.concatenate([c, c_ctx[None, :], jnp.zeros((MOD_ROWS - bsz - 1, D_MODEL), c.dtype)], axis=0)
    mod = _mod_call(_silu(cond).astype(BF16), w_mod, b_mod)
    mod = mod[:, :bsz + 1].reshape(DEPTH, bsz + 1, N_MOD, D_MODEL)
    mod = jnp.pad(mod, ((0, 0), (0, 0), (0, MOD_ROWS - N_MOD), (0, 0)))

    lat_row = lambda tm: (lambda i: i // (seq // tm))
    ctx_row = lambda tm: (lambda i: bsz)
    zero_state = jnp.zeros((bsz, HEADS, 2, HEAD_DIM, HEAD_DIM), F32)
    xc = ctx
    for l in range(DEPTH):
        last = l == DEPTH - 1
        xc, s_ctx = _stream(xc, mod[l], l, params, zero_state, consts,
                            (bsz * ctx_len, ctx_len, ctx_len, ctx_len), ctx_len, ctx_row,
                            grid_mode=False, full=not last)
        x, _ = _stream(x, mod[l], l, params, s_ctx, consts, (1024, 512, 256, 512), 1024, lat_row,
                       grid_mode=True, full=True)
    return x
```

```python
import functools
from typing import NamedTuple

import numpy as np
import jax
import jax.numpy as jnp
from jax import lax
from jax.experimental import pallas as pl
from jax.experimental.pallas import tpu as pltpu

F32 = jnp.float32
BF16 = jnp.bfloat16

D_MODEL = 2048
DEPTH = 2
GRID_W = 64
HEADS = 8
HEAD_DIM = 128
HG_WIDTH = HEADS * HEAD_DIM
CONV_CH = D_MODEL - HG_WIDTH
CONV_WIDTH = 31
CONV_PAD = 16
D_FF = 5632
N_MOD = 6
MOD_ROWS = 8
EPS = 1e-6
LN_EPS = 1e-5
LOG2_E = float(np.log2(np.e))
IN_COLS = 3 * HG_WIDTH + 2 * HG_WIDTH + 2 * CONV_CH

CHUNK = 64
N_LEVELS = 6

VMEM_LIMIT = 56 * 1024 * 1024


def _cparams(sem):
    return pltpu.CompilerParams(dimension_semantics=sem, vmem_limit_bytes=VMEM_LIMIT)


def _silu(x):
    return x * jax.nn.sigmoid(x)


def _rms_rows(x):
    return x * lax.rsqrt(jnp.mean(x * x, axis=-1, keepdims=True) + EPS)


def _mod_kernel(s_ref, w_ref, b_ref, o_ref):
    w = w_ref[0].astype(BF16)
    o_ref[0] = jnp.dot(s_ref[...], w, preferred_element_type=F32) + b_ref[0]


def _mod_call(s_rows, w_mod, b_mod):
    tn = 1024
    n = w_mod.shape[-1]
    return pl.pallas_call(
        _mod_kernel,
        grid=(DEPTH, n // tn),
        in_specs=[
            pl.BlockSpec((MOD_ROWS, D_MODEL), lambda l, j: (0, 0)),
            pl.BlockSpec((1, D_MODEL, tn), lambda l, j: (l, 0, j)),
            pl.BlockSpec((1, 1, tn), lambda l, j: (l, 0, j)),
        ],
        out_specs=pl.BlockSpec((1, MOD_ROWS, tn), lambda l, j: (l, 0, j)),
        out_shape=jax.ShapeDtypeStruct((DEPTH, MOD_ROWS, n), F32),
        compiler_params=_cparams(("arbitrary", "arbitrary")),
        name="mod_matmul",
    )(s_rows, w_mod, b_mod.reshape(DEPTH, 1, n))


def _prenorm(x, mod_ref, nw_ref, shift_row):
    y = _rms_rows(x) * nw_ref[...]
    h = y * (1.0 + mod_ref[0, shift_row + 1:shift_row + 2, :]) + mod_ref[0, shift_row:shift_row + 1, :]
    return h.astype(BF16)


def _prenorm_kernel(x_ref, mod_ref, nw_ref, h_ref, *, shift_row):
    h_ref[...] = _prenorm(x_ref[...], mod_ref, nw_ref, shift_row)


def _prenorm_call(x2, mod3, nw, tm, mod_row, shift_row, name):
    m = x2.shape[0]
    return pl.pallas_call(
        functools.partial(_prenorm_kernel, shift_row=shift_row),
        grid=(m // tm,),
        in_specs=[
            pl.BlockSpec((tm, D_MODEL), lambda i: (i, 0)),
            pl.BlockSpec((1, MOD_ROWS, D_MODEL), lambda i: (mod_row(i), 0, 0)),
            pl.BlockSpec((1, D_MODEL), lambda i: (0, 0)),
        ],
        out_specs=pl.BlockSpec((tm, D_MODEL), lambda i: (i, 0)),
        out_shape=jax.ShapeDtypeStruct((m, D_MODEL), BF16),
        compiler_params=_cparams(("arbitrary",)),
        name=name,
    )(x2, mod3, nw)


def _proj_kernel(h_ref, w_ref, o_ref, wb_ref, *, act):
    @pl.when(pl.program_id(1) == 0)
    def _():
        wb_ref[...] = w_ref[...].astype(BF16)

    acc = jnp.dot(h_ref[...], wb_ref[...], preferred_element_type=F32)
    if act:
        acc = _silu(acc)
    o_ref[...] = acc.astype(o_ref.dtype)


def _proj_call(h2, w, layer, col0, n_cols, tm, tn, act, out_dtype, name):
    m = h2.shape[0]
    c0 = col0 // tn
    return pl.pallas_call(
        functools.partial(_proj_kernel, act=act),
        grid=(n_cols // tn, m // tm),
        in_specs=[
            pl.BlockSpec((tm, D_MODEL), lambda j, i: (i, 0)),
            pl.BlockSpec((None, D_MODEL, tn), lambda j, i: (layer, 0, c0 + j)),
        ],
        out_specs=pl.BlockSpec((tm, tn), lambda j, i: (i, j)),
        out_shape=jax.ShapeDtypeStruct((m, n_cols), out_dtype),
        scratch_shapes=[pltpu.VMEM((D_MODEL, tn), BF16)],
        compiler_params=_cparams(("arbitrary", "arbitrary")),
        name=name,
    )(h2, w)


def _scan_constants():
    c = CHUNK
    t = np.arange(c)[:, None]
    r = np.arange(c)[None, :]
    tri = np.stack([np.tile(r <= t, (1, 3)), np.tile(r >= t, (1, 3))]).astype(np.float32)
    level = np.full((2, c, c), -1, np.int32)
    for lvl in range(N_LEVELS):
        half = c >> (lvl + 1)
        same = (t // (2 * half)) == (r // (2 * half))
        t_late = (t % (2 * half)) >= half
        r_late = (r % (2 * half)) >= half
        level[0][same & t_late & ~r_late] = lvl
        level[1][same & ~t_late & r_late] = lvl
    level[:, np.arange(c), np.arange(c)] = N_LEVELS
    return tri, level


_NT = (((1,), (1,)), ((), ()))
_TN = (((0,), (0,)), ((), ()))


def _neg_abs(x):
    bits = lax.bitcast_convert_type(x, jnp.int32) | jnp.int32(-2 ** 31)
    return lax.bitcast_convert_type(bits, F32)


def _scan_kernel(lbl_ref, tri_ref, level_ref, qf_ref, qb_ref, vf_ref, vb_ref, ff_ref, fb_ref,
                 s0_ref, of_ref, ob_ref, sfin_ref, st_ref, *, layer, n_chunks):
    c = CHUNK
    step = pl.program_id(2)

    @pl.when(step == 0)
    def _():
        st_ref[...] = s0_ref[0, 0]

    rows = [lbl_ref[:, j, :] for j in range(DEPTH)]
    mx = functools.reduce(jnp.maximum, rows)
    es = [jnp.exp(rw - mx) for rw in rows]
    tot = functools.reduce(lambda a, b: a + b, es)
    lb = jnp.zeros_like(mx)
    for j in range(1, layer + 1):
        lb = lb + es[j] / tot

    tri = tri_ref[...]
    level = level_ref[...]
    sub = lax.broadcasted_iota(jnp.int32, (c // 8, 8, HEAD_DIM), 1)

    def ref_rows(b, half, d):
        pick = half - 1 if d == 0 else half
        if half >= 8:
            parts = []
            for p0 in range(0, c, 2 * half):
                parts.append(jnp.broadcast_to(b[p0 + pick:p0 + pick + 1, :], (2 * half, HEAD_DIM)))
            return parts[0] if len(parts) == 1 else jnp.concatenate(parts, axis=0)
        b3 = b.reshape(c // 8, 8, HEAD_DIM)
        out = None
        for g0 in range(0, 8, 2 * half):
            cand = jnp.broadcast_to(b3[:, g0 + pick:g0 + pick + 1, :], b3.shape)
            out = cand if out is None else jnp.where(sub >= g0, cand, out)
        return out.reshape(c, HEAD_DIM)

    refs = ((qf_ref, vf_ref, ff_ref, of_ref), (qb_ref, vb_ref, fb_ref, ob_ref))

    def gates(d, row0):
        raw = refs[d][2][0, row0:row0 + c, :]
        lbd = lb[d:d + 1, :]
        f = lbd + (1.0 - lbd) * jax.nn.sigmoid(raw)
        logf = jnp.log(f)
        hi = logf.astype(BF16)
        rem = logf - hi.astype(F32)
        mid = rem.astype(BF16)
        lo = (rem - mid.astype(F32)).astype(BF16)
        b = jnp.dot(tri[d], jnp.concatenate([hi, mid, lo], axis=0), preferred_element_type=F32)
        return (1.0 - f).astype(BF16), b * LOG2_E

    def intra(d, row0, k_bf, b):
        q_bf = refs[d][0][0, row0:row0 + c, :]
        att = jnp.where(level[d] == N_LEVELS,
                        lax.dot_general(q_bf, k_bf, _NT, preferred_element_type=F32), 0.0)
        for lvl in range(N_LEVELS):
            half = c >> (lvl + 1)
            el = jnp.exp2(_neg_abs(b - ref_rows(b, half, d))).astype(BF16)
            att = jnp.where(level[d] == lvl,
                            lax.dot_general(q_bf * el, k_bf * el, _NT, preferred_element_type=F32), att)
        last = c - 1 if d == 0 else 0
        blast = b[last:last + 1, :]
        q_in = q_bf * jnp.exp2(b).astype(BF16)
        k_out = k_bf * jnp.exp2(blast - b).astype(BF16)
        return att.astype(BF16), q_in, k_out, jnp.exp2(blast)

    def readout(d, row0, att, q_in, k_out, decay, st):
        v = refs[d][1][0, row0:row0 + c, :]
        o = lax.dot_general(q_in, st.astype(BF16), _NT, preferred_element_type=F32)
        refs[d][3][0, row0:row0 + c, :] = o + jnp.dot(att, v, preferred_element_type=F32)
        return st * decay + lax.dot_general(v, k_out, _TN, preferred_element_type=F32)

    work = []
    for ci in range(n_chunks):
        work.append((0, ci * c))
        work.append((1, (n_chunks - 1 - ci) * c))
    st = [st_ref[0], st_ref[1]]
    stage1 = {}
    stage2 = {}
    for n in range(len(work) + 2):
        if n >= 2:
            d, row0 = work[n - 2]
            st[d] = readout(d, row0, *stage2.pop(n - 2), st[d])
        if 1 <= n <= len(work):
            d, row0 = work[n - 1]
            stage2[n - 1] = intra(d, row0, *stage1.pop(n - 1))
        if n < len(work):
            stage1[n] = gates(*work[n])
    st_ref[0] = st[0]
    st_ref[1] = st[1]

    @pl.when(step == pl.num_programs(2) - 1)
    def _():
        sfin_ref[0, 0] = st_ref[...]


def _scan_call(q3, f3, r3, lb_logits, s0, tri, level, layer, tl):
    bsz, length, _ = q3.shape
    nl = length // tl
    fwd = lambda b, h, i: (b, i, h)
    bwd = lambda b, h, i: (b, nl - 1 - i, h)
    bwd_f = lambda b, h, i: (b, nl - 1 - i, HEADS + h)
    blk = (1, tl, HEAD_DIM)
    return pl.pallas_call(
        functools.partial(_scan_kernel, layer=layer, n_chunks=tl // CHUNK),
        grid=(bsz, HEADS, nl),
        in_specs=[
            pl.BlockSpec((2, DEPTH, HEAD_DIM), lambda b, h, i: (0, 0, h)),
            pl.BlockSpec(tri.shape, lambda b, h, i: (0, 0, 0)),
            pl.BlockSpec(level.shape, lambda b, h, i: (0, 0, 0)),
            pl.BlockSpec(blk, fwd),
            pl.BlockSpec(blk, bwd),
            pl.BlockSpec(blk, fwd),
            pl.BlockSpec(blk, bwd),
            pl.BlockSpec(blk, fwd),
            pl.BlockSpec(blk, bwd_f),
            pl.BlockSpec((1, 1, 2, HEAD_DIM, HEAD_DIM), lambda b, h, i: (b, h, 0, 0, 0)),
        ],
        out_specs=[
            pl.BlockSpec(blk, fwd),
            pl.BlockSpec(blk, bwd),
            pl.BlockSpec((1, 1, 2, HEAD_DIM, HEAD_DIM), lambda b, h, i: (b, h, 0, 0, 0)),
        ],
        out_shape=[
            jax.ShapeDtypeStruct((bsz, length, HG_WIDTH), F32),
            jax.ShapeDtypeStruct((bsz, length, HG_WIDTH), F32),
            jax.ShapeDtypeStruct((bsz, HEADS, 2, HEAD_DIM, HEAD_DIM), F32),
        ],
        scratch_shapes=[pltpu.VMEM((2, HEAD_DIM, HEAD_DIM), F32)],
        compiler_params=_cparams(("arbitrary", "arbitrary", "arbitrary")),
        name="hgrn2_scan",
    )(lb_logits, tri, level, q3, q3, r3, r3, f3, f3, s0)


def _cconv_kernel(a_ref, b_ref, cw_ref, cb_ref, lw_ref, lb_ref, o_ref, u_ref, *, length, rows):
    zeros = jnp.zeros((CONV_PAD, HEAD_DIM), F32)
    u_ref[0:CONV_PAD, :] = zeros
    u_ref[CONV_PAD + length:2 * CONV_PAD + length, :] = zeros
    n_tiles = length // rows

    def fill(ti, carry):
        r0 = pl.multiple_of(ti * rows, rows)
        a = a_ref[0, pl.ds(r0, rows), :].astype(F32)
        b = b_ref[0, pl.ds(r0, rows), :].astype(F32)
        u_ref[pl.ds(CONV_PAD + r0, rows), :] = a * jax.nn.sigmoid(b)
        return carry

    lax.fori_loop(0, n_tiles, fill, 0)

    half = CONV_WIDTH // 2

    def conv(ti, carry):
        r0 = pl.multiple_of(ti * rows, rows)
        lead = CONV_PAD - half
        span = -(-(lead + CONV_WIDTH - 1) // 8) * 8 - 8
        acc = jnp.zeros((rows, HEAD_DIM), F32)
        for phase in range(8):
            window = u_ref[pl.ds(r0 + phase, rows + span), :]
            for off in range(phase, lead + CONV_WIDTH, 8):
                if off >= lead:
                    j = off - lead
                    acc = acc + cw_ref[j:j + 1, :] * window[off - phase:off - phase + rows]
        acc = acc + cb_ref[...]
        mu = jnp.mean(acc, axis=-1, keepdims=True)
        dlt = acc - mu
        var = jnp.mean(dlt * dlt, axis=-1, keepdims=True)
        y = dlt * lax.rsqrt(var + LN_EPS) * lw_ref[...] + lb_ref[...]
        o_ref[0, pl.ds(r0, rows), :] = _silu(y).astype(BF16)
        return carry

    lax.fori_loop(0, n_tiles, conv, 0, unroll=4)


def _cconv_call(r3, cw, cb, lw, lb):
    bsz, length, _ = r3.shape
    groups = CONV_CH // HEAD_DIM
    a_col = 2 * HG_WIDTH // HEAD_DIM
    b_col = a_col + groups
    vec = pl.BlockSpec((1, HEAD_DIM), lambda b, g: (0, g))
    return pl.pallas_call(
        functools.partial(_cconv_kernel, length=length, rows=128),
        grid=(bsz, groups),
        in_specs=[
            pl.BlockSpec((1, length, HEAD_DIM), lambda b, g: (b, 0, a_col + g)),
            pl.BlockSpec((1, length, HEAD_DIM), lambda b, g: (b, 0, b_col + g)),
            pl.BlockSpec((CONV_WIDTH, HEAD_DIM), lambda b, g: (0, g)),
            vec, vec, vec,
        ],
        out_specs=pl.BlockSpec((1, length, HEAD_DIM), lambda b, g: (b, 0, g)),
        out_shape=jax.ShapeDtypeStruct((bsz, length, CONV_CH), BF16),
        scratch_shapes=[pltpu.VMEM((length + 2 * CONV_PAD, HEAD_DIM), F32)],
        compiler_params=_cparams(("arbitrary", "arbitrary")),
        name="conformer_conv",
    )(r3, r3, cw, cb.reshape(1, -1), lw.reshape(1, -1), lb.reshape(1, -1))


def _outproj_kernel(of_ref, ob_ref, g_ref, cv_ref, x_ref, mod_ref, hgw_ref, nw_ref, nw2_ref, w_ref, o_ref,
                    h_ref):
    o = of_ref[...] + ob_ref[...]
    heads = []
    for h in range(HEADS):
        oh = o[:, h * HEAD_DIM:(h + 1) * HEAD_DIM]
        heads.append(_rms_rows(oh) * hgw_ref[...])
    r = jnp.concatenate(heads, axis=-1) * _silu(g_ref[...].astype(F32))
    y = jnp.dot(r.astype(BF16), w_ref[0:HG_WIDTH, :], preferred_element_type=F32)
    y = y + jnp.dot(cv_ref[...], w_ref[HG_WIDTH:D_MODEL, :], preferred_element_type=F32)
    x_new = x_ref[...] + mod_ref[0, 2:3, :] * (_rms_rows(y) * nw_ref[...])
    o_ref[...] = x_new
    h_ref[...] = _prenorm(x_new, mod_ref, nw2_ref, 3)


def _outproj_call(of2, ob2, r2, cv2, x2, mod3, hgw, nw, nw2, w_bf, layer, tm, mod_row):
    m = x2.shape[0]
    return pl.pallas_call(
        _outproj_kernel,
        grid=(m // tm,),
        in_specs=[
            pl.BlockSpec((tm, HG_WIDTH), lambda i: (i, 0)),
            pl.BlockSpec((tm, HG_WIDTH), lambda i: (i, 0)),
            pl.BlockSpec((tm, HG_WIDTH), lambda i: (i, 1)),
            pl.BlockSpec((tm, CONV_CH), lambda i: (i, 0)),
            pl.BlockSpec((tm, D_MODEL), lambda i: (i, 0)),
            pl.BlockSpec((1, MOD_ROWS, D_MODEL), lambda i: (mod_row(i), 0, 0)),
            pl.BlockSpec((1, HEAD_DIM), lambda i: (0, 0)),
            pl.BlockSpec((1, D_MODEL), lambda i: (0, 0)),
            pl.BlockSpec((1, D_MODEL), lambda i: (0, 0)),
            pl.BlockSpec((None, D_MODEL, D_MODEL), lambda i: (layer, 0, 0)),
        ],
        out_specs=[
            pl.BlockSpec((tm, D_MODEL), lambda i: (i, 0)),
            pl.BlockSpec((tm, D_MODEL), lambda i: (i, 0)),
        ],
        out_shape=[
            jax.ShapeDtypeStruct((m, D_MODEL), F32),
            jax.ShapeDtypeStruct((m, D_MODEL), BF16),
        ],
        compiler_params=_cparams(("arbitrary",)),
        name="out_proj",
    )(of2, ob2, r2, cv2, x2, mod3, hgw, nw, nw2, w_bf)


ROW_BLOCK = 16
MXU_COLS = 256
MXU_ROWS = 256
FFN_CHUNK = 512
LANE_BLOCK = 256


def _gate_chunk(gp_ref, gm_ref, gn_ref, val_ref, cw_ref, cb_ref, ext_ref, act_out, first, last, mm_tile,
                mm_tiles, *, tm, grid_mode):
    tk = FFN_CHUNK
    rows = ROW_BLOCK
    lanes = LANE_BLOCK
    n_blocks = tm // rows
    total = n_blocks * (tk // lanes)
    row_id = lax.broadcasted_iota(jnp.int32, (rows, lanes), 0)
    edge = GRID_W if grid_mode else tm
    ext_ref[0:GRID_W, :] = jnp.where(first, 0.0, gp_ref[...].astype(F32))
    ext_ref[GRID_W:GRID_W + tm, :] = gm_ref[...].astype(F32)
    ext_ref[GRID_W + tm:2 * GRID_W + tm, :] = jnp.where(last, 0.0, gn_ref[...].astype(F32))

    done = 0
    for l0 in range(0, tk, lanes):
        ls = slice(l0, l0 + lanes)
        taps = 0.5 * cw_ref[:, :, ls]
        bias = 0.5 * cb_ref[:, ls]

        def columns(r0):
            ce = ext_ref[GRID_W + r0:GRID_W + r0 + rows, ls]
            if not grid_mode:
                return [taps[1, dw:dw + 1] * ce for dw in range(3)]
            up = ext_ref[r0:r0 + rows, ls]
            dn = ext_ref[2 * GRID_W + r0:2 * GRID_W + r0 + rows, ls]
            return [taps[0, dw:dw + 1] * up + taps[1, dw:dw + 1] * ce + taps[2, dw:dw + 1] * dn
                    for dw in range(3)]

        cur = columns(0)
        prev_left_row = None
        for blk in range(n_blocks):
            r0 = blk * rows
            for n in range(done * mm_tiles // total, (done + 1) * mm_tiles // total):
                mm_tile(n)
            done += 1
            nxt = columns(r0 + rows) if blk + 1 < n_blocks else None
            left, mid, right = cur
            from_prev = pltpu.roll(left, 1, axis=0)
            if r0 % edge == 0:
                from_prev = jnp.where(row_id == 0, 0.0, from_prev)
            else:
                from_prev = jnp.where(row_id == 0, prev_left_row, from_prev)
            from_next = pltpu.roll(right, rows - 1, axis=0)
            if (r0 + rows) % edge == 0:
                from_next = jnp.where(row_id == rows - 1, 0.0, from_next)
            else:
                from_next = jnp.where(row_id == rows - 1, nxt[2][0:1, :], from_next)
            half = mid + from_prev + from_next + bias
            gelu = half * (1.0 + lax.erf(half * np.float32(np.sqrt(2.0))))
            act_out[r0:r0 + rows, ls] = gelu.astype(BF16) * val_ref[r0:r0 + rows, ls]
            prev_left_row = left[rows - 1:rows, :]
            cur = nxt


def _ffn_down_kernel(gp_ref, gm_ref, gn_ref, val_ref, cw_ref, cb_ref, w_ref, x_ref, mod_ref, nw_ref, *rest,
                     tm, tiles_per_seq, grid_mode, next_norm):
    if next_norm:
        modn_ref, nwn_ref, o_ref, h_ref, acc_ref, ext_ref, act0_ref, act1_ref = rest
    else:
        o_ref, acc_ref, ext_ref, act0_ref, act1_ref = rest
    i = pl.program_id(0)
    kk = pl.program_id(1)
    first = (i % tiles_per_seq) == 0
    last = (i % tiles_per_seq) == tiles_per_seq - 1

    @pl.when(kk == 0)
    def _():
        acc_ref[...] = jnp.zeros_like(acc_ref)
        act1_ref[...] = jnp.zeros_like(act1_ref)

    def step(act_in, act_out):
        row_parts = tm // MXU_ROWS

        def mm_tile(n):
            rows = slice((n % row_parts) * MXU_ROWS, (n % row_parts + 1) * MXU_ROWS)
            cols = slice((n // row_parts) * MXU_COLS, (n // row_parts + 1) * MXU_COLS)
            acc_ref[rows, cols] += jnp.dot(act_in[rows, :], w_ref[:, cols], preferred_element_type=F32)

        _gate_chunk(gp_ref, gm_ref, gn_ref, val_ref, cw_ref, cb_ref, ext_ref, act_out, first, last, mm_tile,
                    row_parts * D_MODEL // MXU_COLS, tm=tm, grid_mode=grid_mode)

    @pl.when(kk % 2 == 0)
    def _():
        step(act1_ref, act0_ref)

    @pl.when(kk % 2 == 1)
    def _():
        step(act0_ref, act1_ref)

    @pl.when(kk == pl.num_programs(1) - 1)
    def _():
        x_new = x_ref[...] + mod_ref[0, 5:6, :] * (_rms_rows(acc_ref[...]) * nw_ref[...])
        o_ref[...] = x_new
        if next_norm:
            h_ref[...] = _prenorm(x_new, modn_ref, nwn_ref, 0)


def _ffn_down_call(gv2, cw, cb, w_bf, layer, x2, mod3, nw, tm, seq_len, mod_row, grid_mode, next_norm):
    m = x2.shape[0]
    tk = FFN_CHUNK
    nk = D_FF // tk
    hb = tm // GRID_W
    n_hblk = m // GRID_W
    tiles_per_seq = seq_len // tm
    cur = lambda k: jnp.minimum(k, nk - 1)
    row_spec = pl.BlockSpec((tm, D_MODEL), lambda i, k: (i, 0))
    mod_spec = pl.BlockSpec((1, MOD_ROWS, D_MODEL), lambda i, k: (mod_row(i), 0, 0))
    vec_spec = pl.BlockSpec((1, D_MODEL), lambda i, k: (0, 0))
    extra_in, extra_specs, out_specs = (), [], row_spec
    out_shape = jax.ShapeDtypeStruct((m, D_MODEL), F32)
    if next_norm:
        extra_in, extra_specs = tuple(next_norm), [mod_spec, vec_spec]
        out_specs = [row_spec, row_spec]
        out_shape = [out_shape, jax.ShapeDtypeStruct((m, D_MODEL), BF16)]
    return pl.pallas_call(
        functools.partial(_ffn_down_kernel, tm=tm, tiles_per_seq=tiles_per_seq, grid_mode=grid_mode,
                          next_norm=bool(next_norm)),
        grid=(m // tm, nk + 1),
        in_specs=[
            pl.BlockSpec((GRID_W, tk), lambda i, k: (jnp.maximum(i * hb - 1, 0), cur(k))),
            pl.BlockSpec((tm, tk), lambda i, k: (i, cur(k))),
            pl.BlockSpec((GRID_W, tk), lambda i, k: (jnp.minimum((i + 1) * hb, n_hblk - 1), cur(k))),
            pl.BlockSpec((tm, tk), lambda i, k: (i, nk + cur(k))),
            pl.BlockSpec((3, 3, tk), lambda i, k: (0, 0, cur(k))),
            pl.BlockSpec((1, tk), lambda i, k: (0, cur(k))),
            pl.BlockSpec((None, tk, D_MODEL), lambda i, k: (layer, jnp.maximum(k - 1, 0), 0)),
            row_spec,
            mod_spec,
            vec_spec,
        ] + extra_specs,
        out_specs=out_specs,
        out_shape=out_shape,
        scratch_shapes=[
            pltpu.VMEM((tm, D_MODEL), F32),
            pltpu.VMEM((tm + 2 * GRID_W, tk), F32),
            pltpu.VMEM((tm, tk), BF16),
            pltpu.VMEM((tm, tk), BF16),
        ],
        compiler_params=_cparams(("arbitrary", "arbitrary")),
        name="ffn_down",
    )(gv2, gv2, gv2, gv2, cw, cb.reshape(1, -1), w_bf, x2, mod3, nw, *extra_in)


class _Tiles(NamedTuple):
    proj: int
    proj_f32: int
    norm: int
    out: int
    down: int
    scan: int
    cols: int


def _stream(x3, h2, mod, l, params, s0, consts, tiles, row_of, grid_mode, full, emit_next):
    bsz, length, _ = x3.shape
    m = bsz * length
    x2 = x3.reshape(m, D_MODEL)
    mod3 = mod[l]
    nw = params["norm_w"][l]
    tn = tiles.cols
    w_in = params["w_in"]
    if h2 is None:
        h2 = _prenorm_call(x2, mod3, nw[0:1], tiles.norm, row_of(tiles.norm), 0, "mix_prenorm")
    q2 = _proj_call(h2, w_in, l, 0, HG_WIDTH, tiles.proj, tn, True, BF16, "in_proj_q")
    f2 = _proj_call(h2, w_in, l, HG_WIDTH, 2 * HG_WIDTH, tiles.proj_f32, tn, False, F32, "in_proj_f")
    r2 = _proj_call(h2, w_in, l, 3 * HG_WIDTH, IN_COLS - 3 * HG_WIDTH, tiles.proj, tn, False, BF16,
                    "in_proj_vgab")
    tri, level = consts
    of3, ob3, sfin = _scan_call(q2.reshape(bsz, length, -1), f2.reshape(bsz, length, -1),
                                r2.reshape(bsz, length, -1), params["lb_logits"], s0, tri, level, l,
                                tiles.scan)
    if not full:
        return None, None, sfin
    cv3 = _cconv_call(r2.reshape(bsz, length, -1), params["conv_w"][l], params["conv_b"][l],
                      params["conv_ln_w"][l], params["conv_ln_b"][l])
    x2, h2 = _outproj_call(of3.reshape(m, -1), ob3.reshape(m, -1), r2, cv3.reshape(m, -1), x2, mod3,
                           params["hg_norm_w"][l].reshape(1, -1), nw[1:2], nw[2:3], params["w_out"], l,
                           tiles.out, row_of(tiles.out))
    gv2 = _proj_call(h2, params["ffn_up"], l, 0, 2 * D_FF, tiles.proj, tn, False, BF16, "ffn_up")
    next_norm = (mod[l + 1], params["norm_w"][l + 1][0:1]) if emit_next else None
    out = _ffn_down_call(gv2, params["ffn_conv_w"][l], params["ffn_conv_b"][l], params["ffn_down"], l,
                         x2, mod3, nw[3:4], tiles.down, length, row_of(tiles.down), grid_mode, next_norm)
    x2, h_next = out if emit_next else (out, None)
    return x2.reshape(bsz, length, D_MODEL), h_next, sfin


def kernel(x, c, ctx, c_ctx, w_mod, b_mod, norm_w, w_in, lb_logits, hg_norm_w, conv_w, conv_b, conv_ln_w,
           conv_ln_b, w_out, ffn_up, ffn_conv_w, ffn_conv_b, ffn_down):
    bsz, seq, _ = x.shape
    ctx_len = ctx.shape[1]
    params = dict(norm_w=norm_w, w_in=w_in, lb_logits=lb_logits, hg_norm_w=hg_norm_w,
                  conv_w=conv_w, conv_b=conv_b, conv_ln_w=conv_ln_w, conv_ln_b=conv_ln_b,
                  w_out=w_out.astype(BF16), ffn_up=ffn_up, ffn_conv_w=ffn_conv_w,
                  ffn_conv_b=ffn_conv_b, ffn_down=ffn_down.astype(BF16))
    tri_np, level_np = _scan_constants()
    consts = (jnp.asarray(tri_np, BF16), jnp.asarray(level_np, jnp.int32))

    cond = jnp.concatenate([c, c_ctx[None, :], jnp.zeros((MOD_ROWS - bsz - 1, D_MODEL), c.dtype)], axis=0)
    mod = _mod_call(_silu(cond).astype(BF16), w_mod, b_mod)
    mod = mod[:, :bsz + 1].reshape(DEPTH, bsz + 1, N_MOD, D_MODEL)
    mod = jnp.pad(mod, ((0, 0), (0, 0), (0, MOD_ROWS - N_MOD), (0, 0)))

    lat_row = lambda tm: (lambda i: i // (seq // tm))
    ctx_row = lambda tm: (lambda i: bsz)
    zero_state = jnp.zeros((bsz, HEADS, 2, HEAD_DIM, HEAD_DIM), F32)
    ctx_rows = bsz * ctx_len
    ctx_tiles = _Tiles(proj=ctx_rows, proj_f32=ctx_rows, norm=ctx_len, out=ctx_len, down=ctx_len,
                       scan=ctx_len, cols=1024)
    lat_tiles = _Tiles(proj=2048, proj_f32=1024, norm=512, out=256, down=512, scan=1024, cols=1024)
    xc, hc, h = ctx, None, None
    for l in range(DEPTH):
        last = l == DEPTH - 1
        xc, hc, s_ctx = _stream(xc, hc, mod, l, params, zero_state, consts, ctx_tiles, ctx_row,
                                grid_mode=False, full=not last, emit_next=not last)
        x, h, _ = _stream(x, h, mod, l, params, s_ctx, consts, lat_tiles, lat_row,
                          grid_mode=True, full=True, emit_next=not last)
    return x
```

```python
import functools
from typing import NamedTuple

import numpy as np
import jax
import jax.numpy as jnp
from jax import lax
from jax.experimental import pallas as pl
from jax.experimental.pallas import tpu as pltpu

F32 = jnp.float32
BF16 = jnp.bfloat16

D_MODEL = 2048
DEPTH = 2
GRID_W = 64
HEADS = 8
HEAD_DIM = 128
HG_WIDTH = HEADS * HEAD_DIM
CONV_CH = D_MODEL - HG_WIDTH
CONV_WIDTH = 31
CONV_PAD = 16
D_FF = 5632
N_MOD = 6
MOD_ROWS = 8
EPS = 1e-6
LN_EPS = 1e-5
LOG2_E = float(np.log2(np.e))
IN_COLS = 3 * HG_WIDTH + 2 * HG_WIDTH + 2 * CONV_CH

CHUNK = 64
N_LEVELS = 6

VMEM_LIMIT = 56 * 1024 * 1024


def _cparams(sem):
    return pltpu.CompilerParams(dimension_semantics=sem, vmem_limit_bytes=VMEM_LIMIT)


def _silu(x):
    return x * jax.nn.sigmoid(x)


def _rms_rows(x):
    return x * lax.rsqrt(jnp.mean(x * x, axis=-1, keepdims=True) + EPS)


def _mod_kernel(s_ref, w_ref, b_ref, o_ref):
    w = w_ref[0].astype(BF16)
    o_ref[0] = jnp.dot(s_ref[...], w, preferred_element_type=F32) + b_ref[0]


def _mod_call(s_rows, w_mod, b_mod):
    tn = 1024
    n = w_mod.shape[-1]
    return pl.pallas_call(
        _mod_kernel,
        grid=(DEPTH, n // tn),
        in_specs=[
            pl.BlockSpec((MOD_ROWS, D_MODEL), lambda l, j: (0, 0)),
            pl.BlockSpec((1, D_MODEL, tn), lambda l, j: (l, 0, j)),
            pl.BlockSpec((1, 1, tn), lambda l, j: (l, 0, j)),
        ],
        out_specs=pl.BlockSpec((1, MOD_ROWS, tn), lambda l, j: (l, 0, j)),
        out_shape=jax.ShapeDtypeStruct((DEPTH, MOD_ROWS, n), F32),
        compiler_params=_cparams(("arbitrary", "arbitrary")),
        name="mod_matmul",
    )(s_rows, w_mod, b_mod.reshape(DEPTH, 1, n))


def _prenorm(x, mod_ref, nw_ref, shift_row):
    y = _rms_rows(x) * nw_ref[...]
    h = y * (1.0 + mod_ref[0, shift_row + 1:shift_row + 2, :]) + mod_ref[0, shift_row:shift_row + 1, :]
    return h.astype(BF16)


def _prenorm_kernel(x_ref, mod_ref, nw_ref, h_ref, *, shift_row):
    h_ref[...] = _prenorm(x_ref[...], mod_ref, nw_ref, shift_row)


def _prenorm_call(x2, mod3, nw, tm, mod_row, shift_row, name):
    m = x2.shape[0]
    return pl.pallas_call(
        functools.partial(_prenorm_kernel, shift_row=shift_row),
        grid=(m // tm,),
        in_specs=[
            pl.BlockSpec((tm, D_MODEL), lambda i: (i, 0)),
            pl.BlockSpec((1, MOD_ROWS, D_MODEL), lambda i: (mod_row(i), 0, 0)),
            pl.BlockSpec((1, D_MODEL), lambda i: (0, 0)),
        ],
        out_specs=pl.BlockSpec((tm, D_MODEL), lambda i: (i, 0)),
        out_shape=jax.ShapeDtypeStruct((m, D_MODEL), BF16),
        compiler_params=_cparams(("arbitrary",)),
        name=name,
    )(x2, mod3, nw)


def _proj_kernel(h_ref, w_ref, o_ref, wb_ref, *, act):
    @pl.when(pl.program_id(1) == 0)
    def _():
        wb_ref[...] = w_ref[...].astype(BF16)

    acc = jnp.dot(h_ref[...], wb_ref[...], preferred_element_type=F32)
    if act:
        acc = _silu(acc)
    o_ref[...] = acc.astype(o_ref.dtype)


def _proj_call(h2, w, layer, col0, n_cols, tm, tn, act, out_dtype, name):
    m = h2.shape[0]
    c0 = col0 // tn
    return pl.pallas_call(
        functools.partial(_proj_kernel, act=act),
        grid=(n_cols // tn, m // tm),
        in_specs=[
            pl.BlockSpec((tm, D_MODEL), lambda j, i: (i, 0)),
            pl.BlockSpec((None, D_MODEL, tn), lambda j, i: (layer, 0, c0 + j)),
        ],
        out_specs=pl.BlockSpec((tm, tn), lambda j, i: (i, j)),
        out_shape=jax.ShapeDtypeStruct((m, n_cols), out_dtype),
        scratch_shapes=[pltpu.VMEM((D_MODEL, tn), BF16)],
        compiler_params=_cparams(("arbitrary", "arbitrary")),
        name=name,
    )(h2, w)


def _scan_constants():
    c = CHUNK
    t = np.arange(c)[:, None]
    r = np.arange(c)[None, :]
    tri = np.stack([np.tile(r <= t, (1, 3)), np.tile(r >= t, (1, 3))]).astype(np.float32)
    level = np.full((2, c, c), -1, np.int32)
    for lvl in range(N_LEVELS):
        half = c >> (lvl + 1)
        same = (t // (2 * half)) == (r // (2 * half))
        t_late = (t % (2 * half)) >= half
        r_late = (r % (2 * half)) >= half
        level[0][same & t_late & ~r_late] = lvl
        level[1][same & ~t_late & r_late] = lvl
    level[:, np.arange(c), np.arange(c)] = N_LEVELS
    return tri, level


_NT = (((1,), (1,)), ((), ()))
_TN = (((0,), (0,)), ((), ()))


def _neg_abs(x):
    bits = lax.bitcast_convert_type(x, jnp.int32) | jnp.int32(-2 ** 31)
    return lax.bitcast_convert_type(bits, F32)


def _scan_kernel(lbl_ref, tri_ref, level_ref, qf_ref, qb_ref, vf_ref, vb_ref, ff_ref, fb_ref,
                 s0_ref, of_ref, ob_ref, sfin_ref, st_ref, *, layer, n_chunks):
    c = CHUNK
    step = pl.program_id(2)

    @pl.when(step == 0)
    def _():
        st_ref[...] = s0_ref[0, 0]

    rows = [lbl_ref[:, j, :] for j in range(DEPTH)]
    mx = functools.reduce(jnp.maximum, rows)
    es = [jnp.exp(rw - mx) for rw in rows]
    tot = functools.reduce(lambda a, b: a + b, es)
    lb = jnp.zeros_like(mx)
    for j in range(1, layer + 1):
        lb = lb + es[j] / tot

    tri = tri_ref[...]
    level = level_ref[...]
    sub = lax.broadcasted_iota(jnp.int32, (c // 8, 8, HEAD_DIM), 1)

    def ref_rows(b, half, d):
        pick = half - 1 if d == 0 else half
        if half >= 8:
            parts = []
            for p0 in range(0, c, 2 * half):
                parts.append(jnp.broadcast_to(b[p0 + pick:p0 + pick + 1, :], (2 * half, HEAD_DIM)))
            return parts[0] if len(parts) == 1 else jnp.concatenate(parts, axis=0)
        b3 = b.reshape(c // 8, 8, HEAD_DIM)
        out = None
        for g0 in range(0, 8, 2 * half):
            cand = jnp.broadcast_to(b3[:, g0 + pick:g0 + pick + 1, :], b3.shape)
            out = cand if out is None else jnp.where(sub >= g0, cand, out)
        return out.reshape(c, HEAD_DIM)

    refs = ((qf_ref, vf_ref, ff_ref, of_ref), (qb_ref, vb_ref, fb_ref, ob_ref))

    def gates(d, row0):
        raw = refs[d][2][0, row0:row0 + c, :]
        lbd = lb[d:d + 1, :]
        f = lbd + (1.0 - lbd) * jax.nn.sigmoid(raw)
        logf = jnp.log(f)
        hi = logf.astype(BF16)
        rem = logf - hi.astype(F32)
        mid = rem.astype(BF16)
        lo = (rem - mid.astype(F32)).astype(BF16)
        b = jnp.dot(tri[d], jnp.concatenate([hi, mid, lo], axis=0), preferred_element_type=F32)
        return (1.0 - f).astype(BF16), b * LOG2_E

    def intra(d, row0, k_bf, b):
        q_bf = refs[d][0][0, row0:row0 + c, :]
        att = jnp.where(level[d] == N_LEVELS,
                        lax.dot_general(q_bf, k_bf, _NT, preferred_element_type=F32), 0.0)
        for lvl in range(N_LEVELS):
            half = c >> (lvl + 1)
            el = jnp.exp2(_neg_abs(b - ref_rows(b, half, d))).astype(BF16)
            att = jnp.where(level[d] == lvl,
                            lax.dot_general(q_bf * el, k_bf * el, _NT, preferred_element_type=F32), att)
        last = c - 1 if d == 0 else 0
        blast = b[last:last + 1, :]
        q_in = q_bf * jnp.exp2(b).astype(BF16)
        k_out = k_bf * jnp.exp2(blast - b).astype(BF16)
        return att.astype(BF16), q_in, k_out, jnp.exp2(blast)

    def readout(d, row0, att, q_in, k_out, decay, st):
        v = refs[d][1][0, row0:row0 + c, :]
        o = lax.dot_general(q_in, st.astype(BF16), _NT, preferred_element_type=F32)
        refs[d][3][0, row0:row0 + c, :] = o + jnp.dot(att, v, preferred_element_type=F32)
        return st * decay + lax.dot_general(v, k_out, _TN, preferred_element_type=F32)

    work = []
    for ci in range(n_chunks):
        work.append((0, ci * c))
        work.append((1, (n_chunks - 1 - ci) * c))
    st = [st_ref[0], st_ref[1]]
    stage1 = {}
    stage2 = {}
    for n in range(len(work) + 2):
        if n >= 2:
            d, row0 = work[n - 2]
            st[d] = readout(d, row0, *stage2.pop(n - 2), st[d])
        if 1 <= n <= len(work):
            d, row0 = work[n - 1]
            stage2[n - 1] = intra(d, row0, *stage1.pop(n - 1))
        if n < len(work):
            stage1[n] = gates(*work[n])
    st_ref[0] = st[0]
    st_ref[1] = st[1]

    @pl.when(step == pl.num_programs(2) - 1)
    def _():
        sfin_ref[0, 0] = st_ref[...]


def _scan_call(q3, f3, r3, lb_logits, s0, tri, level, layer, tl):
    bsz, length, _ = q3.shape
    nl = length // tl
    fwd = lambda b, h, i: (b, i, h)
    bwd = lambda b, h, i: (b, nl - 1 - i, h)
    bwd_f = lambda b, h, i: (b, nl - 1 - i, HEADS + h)
    blk = (1, tl, HEAD_DIM)
    return pl.pallas_call(
        functools.partial(_scan_kernel, layer=layer, n_chunks=tl // CHUNK),
        grid=(bsz, HEADS, nl),
        in_specs=[
            pl.BlockSpec((2, DEPTH, HEAD_DIM), lambda b, h, i: (0, 0, h)),
            pl.BlockSpec(tri.shape, lambda b, h, i: (0, 0, 0)),
            pl.BlockSpec(level.shape, lambda b, h, i: (0, 0, 0)),
            pl.BlockSpec(blk, fwd),
            pl.BlockSpec(blk, bwd),
            pl.BlockSpec(blk, fwd),
            pl.BlockSpec(blk, bwd),
            pl.BlockSpec(blk, fwd),
            pl.BlockSpec(blk, bwd_f),
            pl.BlockSpec((1, 1, 2, HEAD_DIM, HEAD_DIM), lambda b, h, i: (b, h, 0, 0, 0)),
        ],
        out_specs=[
            pl.BlockSpec(blk, fwd),
            pl.BlockSpec(blk, bwd),
            pl.BlockSpec((1, 1, 2, HEAD_DIM, HEAD_DIM), lambda b, h, i: (b, h, 0, 0, 0)),
        ],
        out_shape=[
            jax.ShapeDtypeStruct((bsz, length, HG_WIDTH), F32),
            jax.ShapeDtypeStruct((bsz, length, HG_WIDTH), F32),
            jax.ShapeDtypeStruct((bsz, HEADS, 2, HEAD_DIM, HEAD_DIM), F32),
        ],
        scratch_shapes=[pltpu.VMEM((2, HEAD_DIM, HEAD_DIM), F32)],
        compiler_params=_cparams(("arbitrary", "arbitrary", "arbitrary")),
        name="hgrn2_scan",
    )(lb_logits, tri, level, q3, q3, r3, r3, f3, f3, s0)


def _cconv_kernel(a_ref, b_ref, cw_ref, cb_ref, lw_ref, lb_ref, o_ref, u_ref, *, length, rows):
    zeros = jnp.zeros((CONV_PAD, HEAD_DIM), F32)
    u_ref[0:CONV_PAD, :] = zeros
    u_ref[CONV_PAD + length:2 * CONV_PAD + length, :] = zeros
    n_tiles = length // rows

    def fill(ti, carry):
        r0 = pl.multiple_of(ti * rows, rows)
        a = a_ref[0, pl.ds(r0, rows), :].astype(F32)
        b = b_ref[0, pl.ds(r0, rows), :].astype(F32)
        u_ref[pl.ds(CONV_PAD + r0, rows), :] = a * jax.nn.sigmoid(b)
        return carry

    lax.fori_loop(0, n_tiles, fill, 0)

    half = CONV_WIDTH // 2

    def conv(ti, carry):
        r0 = pl.multiple_of(ti * rows, rows)
        lead = CONV_PAD - half
        span = -(-(lead + CONV_WIDTH - 1) // 8) * 8 - 8
        acc = jnp.zeros((rows, HEAD_DIM), F32)
        for phase in range(8):
            window = u_ref[pl.ds(r0 + phase, rows + span), :]
            for off in range(phase, lead + CONV_WIDTH, 8):
                if off >= lead:
                    j = off - lead
                    acc = acc + cw_ref[j:j + 1, :] * window[off - phase:off - phase + rows]
        acc = acc + cb_ref[...]
        mu = jnp.mean(acc, axis=-1, keepdims=True)
        dlt = acc - mu
        var = jnp.mean(dlt * dlt, axis=-1, keepdims=True)
        y = dlt * lax.rsqrt(var + LN_EPS) * lw_ref[...] + lb_ref[...]
        o_ref[0, pl.ds(r0, rows), :] = _silu(y).astype(BF16)
        return carry

    lax.fori_loop(0, n_tiles, conv, 0, unroll=4)


def _cconv_call(r3, cw, cb, lw, lb):
    bsz, length, _ = r3.shape
    groups = CONV_CH // HEAD_DIM
    a_col = 2 * HG_WIDTH // HEAD_DIM
    b_col = a_col + groups
    vec = pl.BlockSpec((1, HEAD_DIM), lambda b, g: (0, g))
    return pl.pallas_call(
        functools.partial(_cconv_kernel, length=length, rows=128),
        grid=(bsz, groups),
        in_specs=[
            pl.BlockSpec((1, length, HEAD_DIM), lambda b, g: (b, 0, a_col + g)),
            pl.BlockSpec((1, length, HEAD_DIM), lambda b, g: (b, 0, b_col + g)),
            pl.BlockSpec((CONV_WIDTH, HEAD_DIM), lambda b, g: (0, g)),
            vec, vec, vec,
        ],
        out_specs=pl.BlockSpec((1, length, HEAD_DIM), lambda b, g: (b, 0, g)),
        out_shape=jax.ShapeDtypeStruct((bsz, length, CONV_CH), BF16),
        scratch_shapes=[pltpu.VMEM((length + 2 * CONV_PAD, HEAD_DIM), F32)],
        compiler_params=_cparams(("arbitrary", "arbitrary")),
        name="conformer_conv",
    )(r3, r3, cw, cb.reshape(1, -1), lw.reshape(1, -1), lb.reshape(1, -1))


def _outproj_kernel(of_ref, ob_ref, g_ref, cv_ref, x_ref, mod_ref, hgw_ref, nw_ref, nw2_ref, w_ref, o_ref,
                    h_ref):
    o = of_ref[...] + ob_ref[...]
    heads = []
    for h in range(HEADS):
        oh = o[:, h * HEAD_DIM:(h + 1) * HEAD_DIM]
        heads.append(_rms_rows(oh) * hgw_ref[...])
    r = jnp.concatenate(heads, axis=-1) * _silu(g_ref[...].astype(F32))
    y = jnp.dot(r.astype(BF16), w_ref[0:HG_WIDTH, :], preferred_element_type=F32)
    y = y + jnp.dot(cv_ref[...], w_ref[HG_WIDTH:D_MODEL, :], preferred_element_type=F32)
    x_new = x_ref[...] + mod_ref[0, 2:3, :] * (_rms_rows(y) * nw_ref[...])
    o_ref[...] = x_new
    h_ref[...] = _prenorm(x_new, mod_ref, nw2_ref, 3)


def _outproj_call(of2, ob2, r2, cv2, x2, mod3, hgw, nw, nw2, w_bf, layer, tm, mod_row):
    m = x2.shape[0]
    return pl.pallas_call(
        _outproj_kernel,
        grid=(m // tm,),
        in_specs=[
            pl.BlockSpec((tm, HG_WIDTH), lambda i: (i, 0)),
            pl.BlockSpec((tm, HG_WIDTH), lambda i: (i, 0)),
            pl.BlockSpec((tm, HG_WIDTH), lambda i: (i, 1)),
            pl.BlockSpec((tm, CONV_CH), lambda i: (i, 0)),
            pl.BlockSpec((tm, D_MODEL), lambda i: (i, 0)),
            pl.BlockSpec((1, MOD_ROWS, D_MODEL), lambda i: (mod_row(i), 0, 0)),
            pl.BlockSpec((1, HEAD_DIM), lambda i: (0, 0)),
            pl.BlockSpec((1, D_MODEL), lambda i: (0, 0)),
            pl.BlockSpec((1, D_MODEL), lambda i: (0, 0)),
            pl.BlockSpec((None, D_MODEL, D_MODEL), lambda i: (layer, 0, 0)),
        ],
        out_specs=[
            pl.BlockSpec((tm, D_MODEL), lambda i: (i, 0)),
            pl.BlockSpec((tm, D_MODEL), lambda i: (i, 0)),
        ],
        out_shape=[
            jax.ShapeDtypeStruct((m, D_MODEL), F32),
            jax.ShapeDtypeStruct((m, D_MODEL), BF16),
        ],
        compiler_params=_cparams(("arbitrary",)),
        name="out_proj",
    )(of2, ob2, r2, cv2, x2, mod3, hgw, nw, nw2, w_bf)


ROW_BLOCK = 16
MXU_COLS = 256
MXU_ROWS = 256
FFN_CHUNK = 512
LANE_BLOCK = 256


def _gate_chunk(gp_ref, gm_ref, gn_ref, val_ref, cw_ref, cb_ref, ext_ref, act_out, first, last, mm_tile,
                mm_tiles, *, tm, grid_mode):
    tk = FFN_CHUNK
    rows = ROW_BLOCK
    lanes = LANE_BLOCK
    n_blocks = tm // rows
    total = n_blocks * (tk // lanes)
    row_id = lax.broadcasted_iota(jnp.int32, (rows, lanes), 0)
    edge = GRID_W if grid_mode else tm
    ext_ref[0:GRID_W, :] = jnp.where(first, 0.0, gp_ref[...].astype(F32))
    ext_ref[GRID_W:GRID_W + tm, :] = gm_ref[...].astype(F32)
    ext_ref[GRID_W + tm:2 * GRID_W + tm, :] = jnp.where(last, 0.0, gn_ref[...].astype(F32))

    done = 0
    for l0 in range(0, tk, lanes):
        ls = slice(l0, l0 + lanes)
        taps = 0.5 * cw_ref[:, :, ls]
        bias = 0.5 * cb_ref[:, ls]

        def columns(r0):
            ce = ext_ref[GRID_W + r0:GRID_W + r0 + rows, ls]
            if not grid_mode:
                return [taps[1, dw:dw + 1] * ce for dw in range(3)]
            up = ext_ref[r0:r0 + rows, ls]
            dn = ext_ref[2 * GRID_W + r0:2 * GRID_W + r0 + rows, ls]
            return [taps[0, dw:dw + 1] * up + taps[1, dw:dw + 1] * ce + taps[2, dw:dw + 1] * dn
                    for dw in range(3)]

        cur = columns(0)
        prev_left_row = None
        for blk in range(n_blocks):
            r0 = blk * rows
            for n in range(done * mm_tiles // total, (done + 1) * mm_tiles // total):
                mm_tile(n)
            done += 1
            nxt = columns(r0 + rows) if blk + 1 < n_blocks else None
            left, mid, right = cur
            from_prev = pltpu.roll(left, 1, axis=0)
            if r0 % edge == 0:
                from_prev = jnp.where(row_id == 0, 0.0, from_prev)
            else:
                from_prev = jnp.where(row_id == 0, prev_left_row, from_prev)
            from_next = pltpu.roll(right, rows - 1, axis=0)
            if (r0 + rows) % edge == 0:
                from_next = jnp.where(row_id == rows - 1, 0.0, from_next)
            else:
                from_next = jnp.where(row_id == rows - 1, nxt[2][0:1, :], from_next)
            half = mid + from_prev + from_next + bias
            gelu = half * (1.0 + lax.erf(half * np.float32(np.sqrt(2.0))))
            act_out[r0:r0 + rows, ls] = gelu.astype(BF16) * val_ref[r0:r0 + rows, ls]
            prev_left_row = left[rows - 1:rows, :]
            cur = nxt


def _ffn_down_kernel(gp_ref, gm_ref, gn_ref, val_ref, cw_ref, cb_ref, w_ref, x_ref, mod_ref, nw_ref, *rest,
                     tm, tiles_per_seq, grid_mode, next_norm):
    if next_norm:
        modn_ref, nwn_ref, o_ref, h_ref, acc_ref, ext_ref, act0_ref, act1_ref = rest
    else:
        o_ref, acc_ref, ext_ref, act0_ref, act1_ref = rest
    i = pl.program_id(0)
    kk = pl.program_id(1)
    first = (i % tiles_per_seq) == 0
    last = (i % tiles_per_seq) == tiles_per_seq - 1

    @pl.when(kk == 0)
    def _():
        acc_ref[...] = jnp.zeros_like(acc_ref)
        act1_ref[...] = jnp.zeros_like(act1_ref)

    k0 = pl.multiple_of(jnp.maximum(kk - 1, 0) * FFN_CHUNK, FFN_CHUNK)

    def step(act_in, act_out):
        row_parts = tm // MXU_ROWS

        def mm_tile(n):
            rows = slice((n % row_parts) * MXU_ROWS, (n % row_parts + 1) * MXU_ROWS)
            cols = slice((n // row_parts) * MXU_COLS, (n // row_parts + 1) * MXU_COLS)
            acc_ref[rows, cols] += jnp.dot(act_in[rows, :], w_ref[pl.ds(k0, FFN_CHUNK), cols],
                                           preferred_element_type=F32)

        _gate_chunk(gp_ref, gm_ref, gn_ref, val_ref, cw_ref, cb_ref, ext_ref, act_out, first, last, mm_tile,
                    row_parts * D_MODEL // MXU_COLS, tm=tm, grid_mode=grid_mode)

    @pl.when(kk % 2 == 0)
    def _():
        step(act1_ref, act0_ref)

    @pl.when(kk % 2 == 1)
    def _():
        step(act0_ref, act1_ref)

    @pl.when(kk == pl.num_programs(1) - 1)
    def _():
        x_new = x_ref[...] + mod_ref[0, 5:6, :] * (_rms_rows(acc_ref[...]) * nw_ref[...])
        o_ref[...] = x_new
        if next_norm:
            h_ref[...] = _prenorm(x_new, modn_ref, nwn_ref, 0)


def _ffn_down_call(gv2, cw, cb, w_bf, layer, x2, mod3, nw, tm, seq_len, mod_row, grid_mode, next_norm):
    m = x2.shape[0]
    tk = FFN_CHUNK
    nk = D_FF // tk
    hb = tm // GRID_W
    n_hblk = m // GRID_W
    tiles_per_seq = seq_len // tm
    cur = lambda k: jnp.minimum(k, nk - 1)
    row_spec = pl.BlockSpec((tm, D_MODEL), lambda i, k: (i, 0))
    mod_spec = pl.BlockSpec((1, MOD_ROWS, D_MODEL), lambda i, k: (mod_row(i), 0, 0))
    vec_spec = pl.BlockSpec((1, D_MODEL), lambda i, k: (0, 0))
    extra_in, extra_specs, out_specs = (), [], row_spec
    out_shape = jax.ShapeDtypeStruct((m, D_MODEL), F32)
    if next_norm:
        extra_in, extra_specs = tuple(next_norm), [mod_spec, vec_spec]
        out_specs = [row_spec, row_spec]
        out_shape = [out_shape, jax.ShapeDtypeStruct((m, D_MODEL), BF16)]
    return pl.pallas_call(
        functools.partial(_ffn_down_kernel, tm=tm, tiles_per_seq=tiles_per_seq, grid_mode=grid_mode,
                          next_norm=bool(next_norm)),
        grid=(m // tm, nk + 1),
        in_specs=[
            pl.BlockSpec((GRID_W, tk), lambda i, k: (jnp.maximum(i * hb - 1, 0), cur(k))),
            pl.BlockSpec((tm, tk), lambda i, k: (i, cur(k))),
            pl.BlockSpec((GRID_W, tk), lambda i, k: (jnp.minimum((i + 1) * hb, n_hblk - 1), cur(k))),
            pl.BlockSpec((tm, tk), lambda i, k: (i, nk + cur(k))),
            pl.BlockSpec((3, 3, tk), lambda i, k: (0, 0, cur(k))),
            pl.BlockSpec((1, tk), lambda i, k: (0, cur(k))),
            pl.BlockSpec((None, D_FF, D_MODEL), lambda i, k: (layer, 0, 0), pipeline_mode=pl.Buffered(1)),
            row_spec,
            mod_spec,
            vec_spec,
        ] + extra_specs,
        out_specs=out_specs,
        out_shape=out_shape,
        scratch_shapes=[
            pltpu.VMEM((tm, D_MODEL), F32),
            pltpu.VMEM((tm + 2 * GRID_W, tk), F32),
            pltpu.VMEM((tm, tk), BF16),
            pltpu.VMEM((tm, tk), BF16),
        ],
        compiler_params=_cparams(("arbitrary", "arbitrary")),
        name="ffn_down",
    )(gv2, gv2, gv2, gv2, cw, cb.reshape(1, -1), w_bf, x2, mod3, nw, *extra_in)


class _Tiles(NamedTuple):
    proj: int
    proj_f32: int
    norm: int
    out: int
    down: int
    scan: int
    cols: int


def _stream(x3, h2, mod, l, params, s0, consts, tiles, row_of, grid_mode, full, emit_next):
    bsz, length, _ = x3.shape
    m = bsz * length
    x2 = x3.reshape(m, D_MODEL)
    mod3 = mod[l]
    nw = params["norm_w"][l]
    tn = tiles.cols
    w_in = params["w_in"]
    if h2 is None:
        h2 = _prenorm_call(x2, mod3, nw[0:1], tiles.norm, row_of(tiles.norm), 0, "mix_prenorm")
    q2 = _proj_call(h2, w_in, l, 0, HG_WIDTH, tiles.proj, tn, True, BF16, "in_proj_q")
    f2 = _proj_call(h2, w_in, l, HG_WIDTH, 2 * HG_WIDTH, tiles.proj_f32, tn, False, F32, "in_proj_f")
    r2 = _proj_call(h2, w_in, l, 3 * HG_WIDTH, IN_COLS - 3 * HG_WIDTH, tiles.proj, tn, False, BF16,
                    "in_proj_vgab")
    tri, level = consts
    of3, ob3, sfin = _scan_call(q2.reshape(bsz, length, -1), f2.reshape(bsz, length, -1),
                                r2.reshape(bsz, length, -1), params["lb_logits"], s0, tri, level, l,
                                tiles.scan)
    if not full:
        return None, None, sfin
    cv3 = _cconv_call(r2.reshape(bsz, length, -1), params["conv_w"][l], params["conv_b"][l],
                      params["conv_ln_w"][l], params["conv_ln_b"][l])
    x2, h2 = _outproj_call(of3.reshape(m, -1), ob3.reshape(m, -1), r2, cv3.reshape(m, -1), x2, mod3,
                           params["hg_norm_w"][l].reshape(1, -1), nw[1:2], nw[2:3], params["w_out"], l,
                           tiles.out, row_of(tiles.out))
    gv2 = _proj_call(h2, params["ffn_up"], l, 0, 2 * D_FF, tiles.proj, tn, False, BF16, "ffn_up")
    next_norm = (mod[l + 1], params["norm_w"][l + 1][0:1]) if emit_next else None
    out = _ffn_down_call(gv2, params["ffn_conv_w"][l], params["ffn_conv_b"][l], params["ffn_down"], l,
                         x2, mod3, nw[3:4], tiles.down, length, row_of(tiles.down), grid_mode, next_norm)
    x2, h_next = out if emit_next else (out, None)
    return x2.reshape(bsz, length, D_MODEL), h_next, sfin


def kernel(x, c, ctx, c_ctx, w_mod, b_mod, norm_w, w_in, lb_logits, hg_norm_w, conv_w, conv_b, conv_ln_w,
           conv_ln_b, w_out, ffn_up, ffn_conv_w, ffn_conv_b, ffn_down):
    bsz, seq, _ = x.shape
    ctx_len = ctx.shape[1]
    params = dict(norm_w=norm_w, w_in=w_in, lb_logits=lb_logits, hg_norm_w=hg_norm_w,
                  conv_w=conv_w, conv_b=conv_b, conv_ln_w=conv_ln_w, conv_ln_b=conv_ln_b,
                  w_out=w_out.astype(BF16), ffn_up=ffn_up, ffn_conv_w=ffn_conv_w,
                  ffn_conv_b=ffn_conv_b, ffn_down=ffn_down.astype(BF16))
    tri_np, level_np = _scan_constants()
    consts = (jnp.asarray(tri_np, BF16), jnp.asarray(level_np, jnp.int32))

    cond = jnp.concatenate([c, c_ctx[None, :], jnp.zeros((MOD_ROWS - bsz - 1, D_MODEL), c.dtype)], axis=0)
    mod = _mod_call(_silu(cond).astype(BF16), w_mod, b_mod)
    mod = mod[:, :bsz + 1].reshape(DEPTH, bsz + 1, N_MOD, D_MODEL)
    mod = jnp.pad(mod, ((0, 0), (0, 0), (0, MOD_ROWS - N_MOD), (0, 0)))

    lat_row = lambda tm: (lambda i: i // (seq // tm))
    ctx_row = lambda tm: (lambda i: bsz)
    zero_state = jnp.zeros((bsz, HEADS, 2, HEAD_DIM, HEAD_DIM), F32)
    ctx_rows = bsz * ctx_len
    ctx_tiles = _Tiles(proj=ctx_rows, proj_f32=ctx_rows, norm=ctx_len, out=ctx_len, down=ctx_len,
                       scan=ctx_len, cols=1024)
    lat_tiles = _Tiles(proj=2048, proj_f32=1024, norm=512, out=256, down=512, scan=1024, cols=1024)
    xc, hc, h = ctx, None, None
    for l in range(DEPTH):
        last = l == DEPTH - 1
        xc, hc, s_ctx = _stream(xc, hc, mod, l, params, zero_state, consts, ctx_tiles, ctx_row,
                                grid_mode=False, full=not last, emit_next=not last)
        x, h, _ = _stream(x, h, mod, l, params, s_ctx, consts, lat_tiles, lat_row,
                          grid_mode=True, full=True, emit_next=not last)
    return x
```

```python
import functools
from typing import NamedTuple

import numpy as np
import jax
import jax.numpy as jnp
from jax import lax
from jax.experimental import pallas as pl
from jax.experimental.pallas import tpu as pltpu

F32 = jnp.float32
BF16 = jnp.bfloat16

D_MODEL = 2048
DEPTH = 2
GRID_W = 64
HEADS = 8
HEAD_DIM = 128
HG_WIDTH = HEADS * HEAD_DIM
CONV_CH = D_MODEL - HG_WIDTH
CONV_WIDTH = 31
CONV_PAD = 16
D_FF = 5632
N_MOD = 6
MOD_ROWS = 8
EPS = 1e-6
LN_EPS = 1e-5
LOG2_E = float(np.log2(np.e))
IN_COLS = 3 * HG_WIDTH + 2 * HG_WIDTH + 2 * CONV_CH

CHUNK = 64
N_LEVELS = 6

VMEM_LIMIT = 56 * 1024 * 1024


def _cparams(sem):
    return pltpu.CompilerParams(dimension_semantics=sem, vmem_limit_bytes=VMEM_LIMIT)


def _silu(x):
    return x * jax.nn.sigmoid(x)


def _rms_rows(x):
    return x * lax.rsqrt(jnp.mean(x * x, axis=-1, keepdims=True) + EPS)


def _mod_kernel(s_ref, w_ref, b_ref, o_ref):
    w = w_ref[0].astype(BF16)
    o_ref[0] = jnp.dot(s_ref[...], w, preferred_element_type=F32) + b_ref[0]


def _mod_call(s_rows, w_mod, b_mod):
    tn = 1024
    n = w_mod.shape[-1]
    return pl.pallas_call(
        _mod_kernel,
        grid=(DEPTH, n // tn),
        in_specs=[
            pl.BlockSpec((MOD_ROWS, D_MODEL), lambda l, j: (0, 0)),
            pl.BlockSpec((1, D_MODEL, tn), lambda l, j: (l, 0, j)),
            pl.BlockSpec((1, 1, tn), lambda l, j: (l, 0, j)),
        ],
        out_specs=pl.BlockSpec((1, MOD_ROWS, tn), lambda l, j: (l, 0, j)),
        out_shape=jax.ShapeDtypeStruct((DEPTH, MOD_ROWS, n), F32),
        compiler_params=_cparams(("arbitrary", "arbitrary")),
        name="mod_matmul",
    )(s_rows, w_mod, b_mod.reshape(DEPTH, 1, n))


def _prenorm(x, mod_ref, nw_ref, shift_row):
    y = _rms_rows(x) * nw_ref[...]
    h = y * (1.0 + mod_ref[0, shift_row + 1:shift_row + 2, :]) + mod_ref[0, shift_row:shift_row + 1, :]
    return h.astype(BF16)


def _prenorm_kernel(x_ref, mod_ref, nw_ref, h_ref, *, shift_row):
    h_ref[...] = _prenorm(x_ref[...], mod_ref, nw_ref, shift_row)


def _prenorm_call(x2, mod3, nw, tm, mod_row, shift_row, name):
    m = x2.shape[0]
    return pl.pallas_call(
        functools.partial(_prenorm_kernel, shift_row=shift_row),
        grid=(m // tm,),
        in_specs=[
            pl.BlockSpec((tm, D_MODEL), lambda i: (i, 0)),
            pl.BlockSpec((1, MOD_ROWS, D_MODEL), lambda i: (mod_row(i), 0, 0)),
            pl.BlockSpec((1, D_MODEL), lambda i: (0, 0)),
        ],
        out_specs=pl.BlockSpec((tm, D_MODEL), lambda i: (i, 0)),
        out_shape=jax.ShapeDtypeStruct((m, D_MODEL), BF16),
        compiler_params=_cparams(("arbitrary",)),
        name=name,
    )(x2, mod3, nw)


def _proj_kernel(h_ref, w_ref, o_ref, wb_ref, *, act):
    @pl.when(pl.program_id(1) == 0)
    def _():
        wb_ref[...] = w_ref[...].astype(BF16)

    acc = jnp.dot(h_ref[...], wb_ref[...], preferred_element_type=F32)
    if act:
        acc = _silu(acc)
    o_ref[...] = acc.astype(o_ref.dtype)


def _proj_call(h2, w, layer, col0, n_cols, tm, tn, act, out_dtype, name):
    m = h2.shape[0]
    c0 = col0 // tn
    return pl.pallas_call(
        functools.partial(_proj_kernel, act=act),
        grid=(n_cols // tn, m // tm),
        in_specs=[
            pl.BlockSpec((tm, D_MODEL), lambda j, i: (i, 0)),
            pl.BlockSpec((None, D_MODEL, tn), lambda j, i: (layer, 0, c0 + j)),
        ],
        out_specs=pl.BlockSpec((tm, tn), lambda j, i: (i, j)),
        out_shape=jax.ShapeDtypeStruct((m, n_cols), out_dtype),
        scratch_shapes=[pltpu.VMEM((D_MODEL, tn), BF16)],
        compiler_params=_cparams(("arbitrary", "arbitrary")),
        name=name,
    )(h2, w)


def _scan_constants():
    c = CHUNK
    t = np.arange(c)[:, None]
    r = np.arange(c)[None, :]
    tri = np.stack([np.tile(r <= t, (1, 3)), np.tile(r >= t, (1, 3))]).astype(np.float32)
    level = np.full((2, c, c), -1, np.int32)
    for lvl in range(N_LEVELS):
        half = c >> (lvl + 1)
        same = (t // (2 * half)) == (r // (2 * half))
        t_late = (t % (2 * half)) >= half
        r_late = (r % (2 * half)) >= half
        level[0][same & t_late & ~r_late] = lvl
        level[1][same & ~t_late & r_late] = lvl
    level[:, np.arange(c), np.arange(c)] = N_LEVELS
    return tri, level


_NT = (((1,), (1,)), ((), ()))
_TN = (((0,), (0,)), ((), ()))


def _neg_abs(x):
    bits = lax.bitcast_convert_type(x, jnp.int32) | jnp.int32(-2 ** 31)
    return lax.bitcast_convert_type(bits, F32)


def _scan_kernel(lbl_ref, tri_ref, level_ref, qf_ref, qb_ref, vf_ref, vb_ref, ff_ref, fb_ref,
                 s0_ref, of_ref, ob_ref, sfin_ref, st_ref, *, layer, n_chunks):
    c = CHUNK
    step = pl.program_id(2)

    @pl.when(step == 0)
    def _():
        st_ref[...] = s0_ref[0, 0]

    rows = [lbl_ref[:, j, :] for j in range(DEPTH)]
    mx = functools.reduce(jnp.maximum, rows)
    es = [jnp.exp(rw - mx) for rw in rows]
    tot = functools.reduce(lambda a, b: a + b, es)
    lb = jnp.zeros_like(mx)
    for j in range(1, layer + 1):
        lb = lb + es[j] / tot

    tri = tri_ref[...]
    level = level_ref[...]
    sub = lax.broadcasted_iota(jnp.int32, (c // 8, 8, HEAD_DIM), 1)

    def ref_rows(b, half, d):
        pick = half - 1 if d == 0 else half
        if half >= 8:
            parts = []
            for p0 in range(0, c, 2 * half):
                parts.append(jnp.broadcast_to(b[p0 + pick:p0 + pick + 1, :], (2 * half, HEAD_DIM)))
            return parts[0] if len(parts) == 1 else jnp.concatenate(parts, axis=0)
        b3 = b.reshape(c // 8, 8, HEAD_DIM)
        out = None
        for g0 in range(0, 8, 2 * half):
            cand = jnp.broadcast_to(b3[:, g0 + pick:g0 + pick + 1, :], b3.shape)
            out = cand if out is None else jnp.where(sub >= g0, cand, out)
        return out.reshape(c, HEAD_DIM)

    refs = ((qf_ref, vf_ref, ff_ref, of_ref), (qb_ref, vb_ref, fb_ref, ob_ref))

    def gates(d, row0):
        raw = refs[d][2][0, row0:row0 + c, :]
        lbd = lb[d:d + 1, :]
        f = lbd + (1.0 - lbd) * jax.nn.sigmoid(raw)
        logf = jnp.log(f)
        hi = logf.astype(BF16)
        rem = logf - hi.astype(F32)
        mid = rem.astype(BF16)
        lo = (rem - mid.astype(F32)).astype(BF16)
        b = jnp.dot(tri[d], jnp.concatenate([hi, mid, lo], axis=0), preferred_element_type=F32)
        return (1.0 - f).astype(BF16), b * LOG2_E

    def intra(d, row0, k_bf, b):
        q_bf = refs[d][0][0, row0:row0 + c, :]
        att = jnp.where(level[d] == N_LEVELS,
                        lax.dot_general(q_bf, k_bf, _NT, preferred_element_type=F32), 0.0)
        for lvl in range(N_LEVELS):
            half = c >> (lvl + 1)
            el = jnp.exp2(_neg_abs(b - ref_rows(b, half, d))).astype(BF16)
            att = jnp.where(level[d] == lvl,
                            lax.dot_general(q_bf * el, k_bf * el, _NT, preferred_element_type=F32), att)
        last = c - 1 if d == 0 else 0
        blast = b[last:last + 1, :]
        q_in = q_bf * jnp.exp2(b).astype(BF16)
        k_out = k_bf * jnp.exp2(blast - b).astype(BF16)
        return att.astype(BF16), q_in, k_out, jnp.exp2(blast)

    def readout(d, row0, att, q_in, k_out, decay, st):
        v = refs[d][1][0, row0:row0 + c, :]
        o = lax.dot_general(q_in, st.astype(BF16), _NT, preferred_element_type=F32)
        refs[d][3][0, row0:row0 + c, :] = o + jnp.dot(att, v, preferred_element_type=F32)
        return st * decay + lax.dot_general(v, k_out, _TN, preferred_element_type=F32)

    work = []
    for ci in range(n_chunks):
        work.append((0, ci * c))
        work.append((1, (n_chunks - 1 - ci) * c))
    st = [st_ref[0], st_ref[1]]
    stage1 = {}
    stage2 = {}
    for n in range(len(work) + 2):
        if n >= 2:
            d, row0 = work[n - 2]
            st[d] = readout(d, row0, *stage2.pop(n - 2), st[d])
        if 1 <= n <= len(work):
            d, row0 = work[n - 1]
            stage2[n - 1] = intra(d, row0, *stage1.pop(n - 1))
        if n < len(work):
            stage1[n] = gates(*work[n])
    st_ref[0] = st[0]
    st_ref[1] = st[1]

    @pl.when(step == pl.num_programs(2) - 1)
    def _():
        sfin_ref[0, 0] = st_ref[...]


def _scan_call(q3, f3, r3, lb_logits, s0, tri, level, layer, tl):
    bsz, length, _ = q3.shape
    nl = length // tl
    fwd = lambda b, h, i: (b, i, h)
    bwd = lambda b, h, i: (b, nl - 1 - i, h)
    bwd_f = lambda b, h, i: (b, nl - 1 - i, HEADS + h)
    blk = (1, tl, HEAD_DIM)
    return pl.pallas_call(
        functools.partial(_scan_kernel, layer=layer, n_chunks=tl // CHUNK),
        grid=(bsz, HEADS, nl),
        in_specs=[
            pl.BlockSpec((2, DEPTH, HEAD_DIM), lambda b, h, i: (0, 0, h)),
            pl.BlockSpec(tri.shape, lambda b, h, i: (0, 0, 0)),
            pl.BlockSpec(level.shape, lambda b, h, i: (0, 0, 0)),
            pl.BlockSpec(blk, fwd),
            pl.BlockSpec(blk, bwd),
            pl.BlockSpec(blk, fwd),
            pl.BlockSpec(blk, bwd),
            pl.BlockSpec(blk, fwd),
            pl.BlockSpec(blk, bwd_f),
            pl.BlockSpec((1, 1, 2, HEAD_DIM, HEAD_DIM), lambda b, h, i: (b, h, 0, 0, 0)),
        ],
        out_specs=[
            pl.BlockSpec(blk, fwd),
            pl.BlockSpec(blk, bwd),
            pl.BlockSpec((1, 1, 2, HEAD_DIM, HEAD_DIM), lambda b, h, i: (b, h, 0, 0, 0)),
        ],
        out_shape=[
            jax.ShapeDtypeStruct((bsz, length, HG_WIDTH), F32),
            jax.ShapeDtypeStruct((bsz, length, HG_WIDTH), F32),
            jax.ShapeDtypeStruct((bsz, HEADS, 2, HEAD_DIM, HEAD_DIM), F32),
        ],
        scratch_shapes=[pltpu.VMEM((2, HEAD_DIM, HEAD_DIM), F32)],
        compiler_params=_cparams(("arbitrary", "arbitrary", "arbitrary")),
        name="hgrn2_scan",
    )(lb_logits, tri, level, q3, q3, r3, r3, f3, f3, s0)


def _cconv_kernel(a_ref, b_ref, cw_ref, cb_ref, lw_ref, lb_ref, o_ref, u_ref, *, length, rows):
    zeros = jnp.zeros((CONV_PAD, HEAD_DIM), F32)
    u_ref[0:CONV_PAD, :] = zeros
    u_ref[CONV_PAD + length:2 * CONV_PAD + length, :] = zeros
    n_tiles = length // rows

    def fill(ti, carry):
        r0 = pl.multiple_of(ti * rows, rows)
        a = a_ref[0, pl.ds(r0, rows), :].astype(F32)
        b = b_ref[0, pl.ds(r0, rows), :].astype(F32)
        u_ref[pl.ds(CONV_PAD + r0, rows), :] = a * jax.nn.sigmoid(b)
        return carry

    lax.fori_loop(0, n_tiles, fill, 0)

    half = CONV_WIDTH // 2

    def conv(ti, carry):
        r0 = pl.multiple_of(ti * rows, rows)
        lead = CONV_PAD - half
        span = -(-(lead + CONV_WIDTH - 1) // 8) * 8 - 8
        acc = jnp.zeros((rows, HEAD_DIM), F32)
        for phase in range(8):
            window = u_ref[pl.ds(r0 + phase, rows + span), :]
            for off in range(phase, lead + CONV_WIDTH, 8):
                if off >= lead:
                    j = off - lead
                    acc = acc + cw_ref[j:j + 1, :] * window[off - phase:off - phase + rows]
        acc = acc + cb_ref[...]
        mu = jnp.mean(acc, axis=-1, keepdims=True)
        dlt = acc - mu
        var = jnp.mean(dlt * dlt, axis=-1, keepdims=True)
        y = dlt * lax.rsqrt(var + LN_EPS) * lw_ref[...] + lb_ref[...]
        o_ref[0, pl.ds(r0, rows), :] = _silu(y).astype(BF16)
        return carry

    lax.fori_loop(0, n_tiles, conv, 0, unroll=4)


def _cconv_call(r3, cw, cb, lw, lb):
    bsz, length, _ = r3.shape
    groups = CONV_CH // HEAD_DIM
    a_col = 2 * HG_WIDTH // HEAD_DIM
    b_col = a_col + groups
    vec = pl.BlockSpec((1, HEAD_DIM), lambda b, g: (0, g))
    return pl.pallas_call(
        functools.partial(_cconv_kernel, length=length, rows=128),
        grid=(bsz, groups),
        in_specs=[
            pl.BlockSpec((1, length, HEAD_DIM), lambda b, g: (b, 0, a_col + g)),
            pl.BlockSpec((1, length, HEAD_DIM), lambda b, g: (b, 0, b_col + g)),
            pl.BlockSpec((CONV_WIDTH, HEAD_DIM), lambda b, g: (0, g)),
            vec, vec, vec,
        ],
        out_specs=pl.BlockSpec((1, length, HEAD_DIM), lambda b, g: (b, 0, g)),
        out_shape=jax.ShapeDtypeStruct((bsz, length, CONV_CH), BF16),
        scratch_shapes=[pltpu.VMEM((length + 2 * CONV_PAD, HEAD_DIM), F32)],
        compiler_params=_cparams(("arbitrary", "arbitrary")),
        name="conformer_conv",
    )(r3, r3, cw, cb.reshape(1, -1), lw.reshape(1, -1), lb.reshape(1, -1))


def _outproj_kernel(of_ref, ob_ref, g_ref, cv_ref, x_ref, mod_ref, hgw_ref, nw_ref, nw2_ref, w_ref, o_ref,
                    h_ref):
    o = of_ref[...] + ob_ref[...]
    heads = []
    for h in range(HEADS):
        oh = o[:, h * HEAD_DIM:(h + 1) * HEAD_DIM]
        heads.append(_rms_rows(oh) * hgw_ref[...])
    r = jnp.concatenate(heads, axis=-1) * _silu(g_ref[...].astype(F32))
    y = jnp.dot(r.astype(BF16), w_ref[0:HG_WIDTH, :], preferred_element_type=F32)
    y = y + jnp.dot(cv_ref[...], w_ref[HG_WIDTH:D_MODEL, :], preferred_element_type=F32)
    x_new = x_ref[...] + mod_ref[0, 2:3, :] * (_rms_rows(y) * nw_ref[...])
    o_ref[...] = x_new
    h_ref[...] = _prenorm(x_new, mod_ref, nw2_ref, 3)


def _outproj_call(of2, ob2, r2, cv2, x2, mod3, hgw, nw, nw2, w_bf, layer, tm, mod_row):
    m = x2.shape[0]
    return pl.pallas_call(
        _outproj_kernel,
        grid=(m // tm,),
        in_specs=[
            pl.BlockSpec((tm, HG_WIDTH), lambda i: (i, 0)),
            pl.BlockSpec((tm, HG_WIDTH), lambda i: (i, 0)),
            pl.BlockSpec((tm, HG_WIDTH), lambda i: (i, 1)),
            pl.BlockSpec((tm, CONV_CH), lambda i: (i, 0)),
            pl.BlockSpec((tm, D_MODEL), lambda i: (i, 0)),
            pl.BlockSpec((1, MOD_ROWS, D_MODEL), lambda i: (mod_row(i), 0, 0)),
            pl.BlockSpec((1, HEAD_DIM), lambda i: (0, 0)),
            pl.BlockSpec((1, D_MODEL), lambda i: (0, 0)),
            pl.BlockSpec((1, D_MODEL), lambda i: (0, 0)),
            pl.BlockSpec((None, D_MODEL, D_MODEL), lambda i: (layer, 0, 0)),
        ],
        out_specs=[
            pl.BlockSpec((tm, D_MODEL), lambda i: (i, 0)),
            pl.BlockSpec((tm, D_MODEL), lambda i: (i, 0)),
        ],
        out_shape=[
            jax.ShapeDtypeStruct((m, D_MODEL), F32),
            jax.ShapeDtypeStruct((m, D_MODEL), BF16),
        ],
        compiler_params=_cparams(("arbitrary",)),
        name="out_proj",
    )(of2, ob2, r2, cv2, x2, mod3, hgw, nw, nw2, w_bf)


ROW_BLOCK = 16
MXU_COLS = 256
MXU_ROWS = 256
FFN_CHUNK = 512
LANE_BLOCK = 256


def _gate_chunk(gp_ref, gm_ref, gn_ref, val_ref, cw_ref, cb_ref, act_out, first, last, mm_tile,
                mm_tiles, *, tm, grid_mode):
    tk = FFN_CHUNK
    rows = ROW_BLOCK
    lanes = LANE_BLOCK
    n_blocks = tm // rows
    total = n_blocks * (tk // lanes)
    row_id = lax.broadcasted_iota(jnp.int32, (rows, lanes), 0)
    edge = GRID_W if grid_mode else tm
    zero = jnp.zeros((), BF16)

    def gate_rows(r0, ls):
        if r0 < 0:
            return jnp.where(first, zero, gp_ref[GRID_W + r0:GRID_W + r0 + rows, ls])
        if r0 >= tm:
            return jnp.where(last, zero, gn_ref[r0 - tm:r0 - tm + rows, ls])
        return gm_ref[r0:r0 + rows, ls]

    done = 0
    for l0 in range(0, tk, lanes):
        ls = slice(l0, l0 + lanes)
        taps = (0.5 * cw_ref[:, :, ls]).astype(BF16)
        taps = [[jnp.broadcast_to(taps[dh, dw:dw + 1], (rows, lanes)) for dw in range(3)] for dh in range(3)]
        bias = 0.5 * cb_ref[:, ls]

        def columns(r0):
            ce = gate_rows(r0, ls)
            if not grid_mode:
                return [(taps[1][dw] * ce).astype(F32) for dw in range(3)]
            up = gate_rows(r0 - GRID_W, ls)
            dn = gate_rows(r0 + GRID_W, ls)
            return [(taps[0][dw] * up + taps[1][dw] * ce + taps[2][dw] * dn).astype(F32) for dw in range(3)]

        cur = columns(0)
        prev_left_row = None
        for blk in range(n_blocks):
            r0 = blk * rows
            for n in range(done * mm_tiles // total, (done + 1) * mm_tiles // total):
                mm_tile(n)
            done += 1
            nxt = columns(r0 + rows) if blk + 1 < n_blocks else None
            left, mid, right = cur
            from_prev = pltpu.roll(left, 1, axis=0)
            if r0 % edge == 0:
                from_prev = jnp.where(row_id == 0, 0.0, from_prev)
            else:
                from_prev = jnp.where(row_id == 0, prev_left_row, from_prev)
            from_next = pltpu.roll(right, rows - 1, axis=0)
            if (r0 + rows) % edge == 0:
                from_next = jnp.where(row_id == rows - 1, 0.0, from_next)
            else:
                from_next = jnp.where(row_id == rows - 1, nxt[2][0:1, :], from_next)
            half = mid + from_prev + from_next + bias
            gelu = half * (1.0 + lax.erf(half * np.float32(np.sqrt(2.0))))
            act_out[r0:r0 + rows, ls] = gelu.astype(BF16) * val_ref[r0:r0 + rows, ls]
            prev_left_row = left[rows - 1:rows, :]
            cur = nxt


def _ffn_down_kernel(gp_ref, gm_ref, gn_ref, val_ref, cw_ref, cb_ref, w_ref, x_ref, mod_ref, nw_ref, *rest,
                     tm, tiles_per_seq, grid_mode, next_norm):
    if next_norm:
        modn_ref, nwn_ref, o_ref, h_ref, acc_ref, act0_ref, act1_ref = rest
    else:
        o_ref, acc_ref, act0_ref, act1_ref = rest
    i = pl.program_id(0)
    kk = pl.program_id(1)
    first = (i % tiles_per_seq) == 0
    last = (i % tiles_per_seq) == tiles_per_seq - 1

    @pl.when(kk == 0)
    def _():
        acc_ref[...] = jnp.zeros_like(acc_ref)
        act1_ref[...] = jnp.zeros_like(act1_ref)

    def step(act_in, act_out):
        row_parts = tm // MXU_ROWS

        def mm_tile(n):
            rows = slice((n % row_parts) * MXU_ROWS, (n % row_parts + 1) * MXU_ROWS)
            cols = slice((n // row_parts) * MXU_COLS, (n // row_parts + 1) * MXU_COLS)
            acc_ref[rows, cols] += jnp.dot(act_in[rows, :], w_ref[:, cols], preferred_element_type=F32)

        _gate_chunk(gp_ref, gm_ref, gn_ref, val_ref, cw_ref, cb_ref, act_out, first, last, mm_tile,
                    row_parts * D_MODEL // MXU_COLS, tm=tm, grid_mode=grid_mode)

    @pl.when(kk % 2 == 0)
    def _():
        step(act1_ref, act0_ref)

    @pl.when(kk % 2 == 1)
    def _():
        step(act0_ref, act1_ref)

    @pl.when(kk == pl.num_programs(1) - 1)
    def _():
        x_new = x_ref[...] + mod_ref[0, 5:6, :] * (_rms_rows(acc_ref[...]) * nw_ref[...])
        o_ref[...] = x_new
        if next_norm:
            h_ref[...] = _prenorm(x_new, modn_ref, nwn_ref, 0)


def _ffn_down_call(gv2, cw, cb, w_bf, layer, x2, mod3, nw, tm, seq_len, mod_row, grid_mode, next_norm):
    m = x2.shape[0]
    tk = FFN_CHUNK
    nk = D_FF // tk
    hb = tm // GRID_W
    n_hblk = m // GRID_W
    tiles_per_seq = seq_len // tm
    cur = lambda k: jnp.minimum(k, nk - 1)
    row_spec = pl.BlockSpec((tm, D_MODEL), lambda i, k: (i, 0))
    mod_spec = pl.BlockSpec((1, MOD_ROWS, D_MODEL), lambda i, k: (mod_row(i), 0, 0))
    vec_spec = pl.BlockSpec((1, D_MODEL), lambda i, k: (0, 0))
    extra_in, extra_specs, out_specs = (), [], row_spec
    out_shape = jax.ShapeDtypeStruct((m, D_MODEL), F32)
    if next_norm:
        extra_in, extra_specs = tuple(next_norm), [mod_spec, vec_spec]
        out_specs = [row_spec, row_spec]
        out_shape = [out_shape, jax.ShapeDtypeStruct((m, D_MODEL), BF16)]
    return pl.pallas_call(
        functools.partial(_ffn_down_kernel, tm=tm, tiles_per_seq=tiles_per_seq, grid_mode=grid_mode,
                          next_norm=bool(next_norm)),
        grid=(m // tm, nk + 1),
        in_specs=[
            pl.BlockSpec((GRID_W, tk), lambda i, k: (jnp.maximum(i * hb - 1, 0), cur(k))),
            pl.BlockSpec((tm, tk), lambda i, k: (i, cur(k))),
            pl.BlockSpec((GRID_W, tk), lambda i, k: (jnp.minimum((i + 1) * hb, n_hblk - 1), cur(k))),
            pl.BlockSpec((tm, tk), lambda i, k: (i, nk + cur(k))),
            pl.BlockSpec((3, 3, tk), lambda i, k: (0, 0, cur(k))),
            pl.BlockSpec((1, tk), lambda i, k: (0, cur(k))),
            pl.BlockSpec((None, tk, D_MODEL), lambda i, k: (layer, jnp.maximum(k - 1, 0), 0)),
            row_spec,
            mod_spec,
            vec_spec,
        ] + extra_specs,
        out_specs=out_specs,
        out_shape=out_shape,
        scratch_shapes=[
            pltpu.VMEM((tm, D_MODEL), F32),
            pltpu.VMEM((tm, tk), BF16),
            pltpu.VMEM((tm, tk), BF16),
        ],
        compiler_params=_cparams(("arbitrary", "arbitrary")),
        name="ffn_down",
    )(gv2, gv2, gv2, gv2, cw, cb.reshape(1, -1), w_bf, x2, mod3, nw, *extra_in)


class _Tiles(NamedTuple):
    proj: int
    proj_f32: int
    norm: int
    out: int
    down: int
    scan: int
    cols: int


def _stream(x3, h2, mod, l, params, s0, consts, tiles, row_of, grid_mode, full, emit_next):
    bsz, length, _ = x3.shape
    m = bsz * length
    x2 = x3.reshape(m, D_MODEL)
    mod3 = mod[l]
    nw = params["norm_w"][l]
    tn = tiles.cols
    w_in = params["w_in"]
    if h2 is None:
        h2 = _prenorm_call(x2, mod3, nw[0:1], tiles.norm, row_of(tiles.norm), 0, "mix_prenorm")
    q2 = _proj_call(h2, w_in, l, 0, HG_WIDTH, tiles.proj, tn, True, BF16, "in_proj_q")
    f2 = _proj_call(h2, w_in, l, HG_WIDTH, 2 * HG_WIDTH, tiles.proj_f32, tn, False, F32, "in_proj_f")
    r2 = _proj_call(h2, w_in, l, 3 * HG_WIDTH, IN_COLS - 3 * HG_WIDTH, tiles.proj, tn, False, BF16,
                    "in_proj_vgab")
    tri, level = consts
    of3, ob3, sfin = _scan_call(q2.reshape(bsz, length, -1), f2.reshape(bsz, length, -1),
                                r2.reshape(bsz, length, -1), params["lb_logits"], s0, tri, level, l,
                                tiles.scan)
    if not full:
        return None, None, sfin
    cv3 = _cconv_call(r2.reshape(bsz, length, -1), params["conv_w"][l], params["conv_b"][l],
                      params["conv_ln_w"][l], params["conv_ln_b"][l])
    x2, h2 = _outproj_call(of3.reshape(m, -1), ob3.reshape(m, -1), r2, cv3.reshape(m, -1), x2, mod3,
                           params["hg_norm_w"][l].reshape(1, -1), nw[1:2], nw[2:3], params["w_out"], l,
                           tiles.out, row_of(tiles.out))
    gv2 = _proj_call(h2, params["ffn_up"], l, 0, 2 * D_FF, tiles.proj, tn, False, BF16, "ffn_up")
    next_norm = (mod[l + 1], params["norm_w"][l + 1][0:1]) if emit_next else None
    out = _ffn_down_call(gv2, params["ffn_conv_w"][l], params["ffn_conv_b"][l], params["ffn_down"], l,
                         x2, mod3, nw[3:4], tiles.down, length, row_of(tiles.down), grid_mode, next_norm)
    x2, h_next = out if emit_next else (out, None)
    return x2.reshape(bsz, length, D_MODEL), h_next, sfin


def kernel(x, c, ctx, c_ctx, w_mod, b_mod, norm_w, w_in, lb_logits, hg_norm_w, conv_w, conv_b, conv_ln_w,
           conv_ln_b, w_out, ffn_up, ffn_conv_w, ffn_conv_b, ffn_down):
    bsz, seq, _ = x.shape
    ctx_len = ctx.shape[1]
    params = dict(norm_w=norm_w, w_in=w_in, lb_logits=lb_logits, hg_norm_w=hg_norm_w,
                  conv_w=conv_w, conv_b=conv_b, conv_ln_w=conv_ln_w, conv_ln_b=conv_ln_b,
                  w_out=w_out.astype(BF16), ffn_up=ffn_up, ffn_conv_w=ffn_conv_w,
                  ffn_conv_b=ffn_conv_b, ffn_down=ffn_down.astype(BF16))
    tri_np, level_np = _scan_constants()
    consts = (jnp.asarray(tri_np, BF16), jnp.asarray(level_np, jnp.int32))

    cond = jnp.concatenate([c, c_ctx[None, :], jnp.zeros((MOD_ROWS - bsz - 1, D_MODEL), c.dtype)], axis=0)
    mod = _mod_call(_silu(cond).astype(BF16), w_mod, b_mod)
    mod = mod[:, :bsz + 1].reshape(DEPTH, bsz + 1, N_MOD, D_MODEL)
    mod = jnp.pad(mod, ((0, 0), (0, 0), (0, MOD_ROWS - N_MOD), (0, 0)))

    lat_row = lambda tm: (lambda i: i // (seq // tm))
    ctx_row = lambda tm: (lambda i: bsz)
    zero_state = jnp.zeros((bsz, HEADS, 2, HEAD_DIM, HEAD_DIM), F32)
    ctx_rows = bsz * ctx_len
    ctx_tiles = _Tiles(proj=ctx_rows, proj_f32=ctx_rows, norm=ctx_len, out=ctx_len, down=ctx_len,
                       scan=ctx_len, cols=1024)
    lat_tiles = _Tiles(proj=2048, proj_f32=1024, norm=512, out=256, down=512, scan=1024, cols=1024)
    xc, hc, h = ctx, None, None
    for l in range(DEPTH):
        last = l == DEPTH - 1
        xc, hc, s_ctx = _stream(xc, hc, mod, l, params, zero_state, consts, ctx_tiles, ctx_row,
                                grid_mode=False, full=not last, emit_next=not last)
        x, h, _ = _stream(x, h, mod, l, params, s_ctx, consts, lat_tiles, lat_row,
                          grid_mode=True, full=True, emit_next=not last)
    return x
```

```python
import functools
from typing import NamedTuple

import numpy as np
import jax
import jax.numpy as jnp
from jax import lax
from jax.experimental import pallas as pl
from jax.experimental.pallas import tpu as pltpu

F32 = jnp.float32
BF16 = jnp.bfloat16

D_MODEL = 2048
DEPTH = 2
GRID_W = 64
HEADS = 8
HEAD_DIM = 128
HG_WIDTH = HEADS * HEAD_DIM
CONV_CH = D_MODEL - HG_WIDTH
CONV_WIDTH = 31
CONV_PAD = 16
D_FF = 5632
N_MOD = 6
MOD_ROWS = 8
EPS = 1e-6
LN_EPS = 1e-5
LOG2_E = float(np.log2(np.e))
IN_COLS = 3 * HG_WIDTH + 2 * HG_WIDTH + 2 * CONV_CH

CHUNK = 64
N_LEVELS = 6

VMEM_LIMIT = 56 * 1024 * 1024


def _cparams(sem):
    return pltpu.CompilerParams(dimension_semantics=sem, vmem_limit_bytes=VMEM_LIMIT)


def _silu(x):
    return x * jax.nn.sigmoid(x)


def _rms_rows(x):
    return x * lax.rsqrt(jnp.mean(x * x, axis=-1, keepdims=True) + EPS)


def _mod_kernel(s_ref, w_ref, b_ref, o_ref):
    w = w_ref[0].astype(BF16)
    o_ref[0] = jnp.dot(s_ref[...], w, preferred_element_type=F32) + b_ref[0]


def _mod_call(s_rows, w_mod, b_mod):
    tn = 1024
    n = w_mod.shape[-1]
    return pl.pallas_call(
        _mod_kernel,
        grid=(DEPTH, n // tn),
        in_specs=[
            pl.BlockSpec((MOD_ROWS, D_MODEL), lambda l, j: (0, 0)),
            pl.BlockSpec((1, D_MODEL, tn), lambda l, j: (l, 0, j)),
            pl.BlockSpec((1, 1, tn), lambda l, j: (l, 0, j)),
        ],
        out_specs=pl.BlockSpec((1, MOD_ROWS, tn), lambda l, j: (l, 0, j)),
        out_shape=jax.ShapeDtypeStruct((DEPTH, MOD_ROWS, n), F32),
        compiler_params=_cparams(("arbitrary", "arbitrary")),
        name="mod_matmul",
    )(s_rows, w_mod, b_mod.reshape(DEPTH, 1, n))


def _prenorm(x, mod_ref, nw_ref, shift_row):
    y = _rms_rows(x) * nw_ref[...]
    h = y * (1.0 + mod_ref[0, shift_row + 1:shift_row + 2, :]) + mod_ref[0, shift_row:shift_row + 1, :]
    return h.astype(BF16)


def _prenorm_kernel(x_ref, mod_ref, nw_ref, h_ref, *, shift_row):
    h_ref[...] = _prenorm(x_ref[...], mod_ref, nw_ref, shift_row)


def _prenorm_call(x2, mod3, nw, tm, mod_row, shift_row, name):
    m = x2.shape[0]
    return pl.pallas_call(
        functools.partial(_prenorm_kernel, shift_row=shift_row),
        grid=(m // tm,),
        in_specs=[
            pl.BlockSpec((tm, D_MODEL), lambda i: (i, 0)),
            pl.BlockSpec((1, MOD_ROWS, D_MODEL), lambda i: (mod_row(i), 0, 0)),
            pl.BlockSpec((1, D_MODEL), lambda i: (0, 0)),
        ],
        out_specs=pl.BlockSpec((tm, D_MODEL), lambda i: (i, 0)),
        out_shape=jax.ShapeDtypeStruct((m, D_MODEL), BF16),
        compiler_params=_cparams(("arbitrary",)),
        name=name,
    )(x2, mod3, nw)


def _proj_kernel(h_ref, w_ref, o_ref, wb_ref, *, act):
    @pl.when(pl.program_id(1) == 0)
    def _():
        wb_ref[...] = w_ref[...].astype(BF16)

    acc = jnp.dot(h_ref[...], wb_ref[...], preferred_element_type=F32)
    if act:
        acc = _silu(acc)
    o_ref[...] = acc.astype(o_ref.dtype)


def _proj_call(h2, w, layer, col0, n_cols, tm, tn, act, out_dtype, name):
    m = h2.shape[0]
    c0 = col0 // tn
    return pl.pallas_call(
        functools.partial(_proj_kernel, act=act),
        grid=(n_cols // tn, m // tm),
        in_specs=[
            pl.BlockSpec((tm, D_MODEL), lambda j, i: (i, 0)),
            pl.BlockSpec((None, D_MODEL, tn), lambda j, i: (layer, 0, c0 + j)),
        ],
        out_specs=pl.BlockSpec((tm, tn), lambda j, i: (i, j)),
        out_shape=jax.ShapeDtypeStruct((m, n_cols), out_dtype),
        scratch_shapes=[pltpu.VMEM((D_MODEL, tn), BF16)],
        compiler_params=_cparams(("arbitrary", "arbitrary")),
        name=name,
    )(h2, w)


def _scan_constants():
    c = CHUNK
    t = np.arange(c)[:, None]
    r = np.arange(c)[None, :]
    tri = np.stack([np.tile(r <= t, (1, 3)), np.tile(r >= t, (1, 3))]).astype(np.float32)
    level = np.full((2, c, c), -1, np.int32)
    for lvl in range(N_LEVELS):
        half = c >> (lvl + 1)
        same = (t // (2 * half)) == (r // (2 * half))
        t_late = (t % (2 * half)) >= half
        r_late = (r % (2 * half)) >= half
        level[0][same & t_late & ~r_late] = lvl
        level[1][same & ~t_late & r_late] = lvl
    level[:, np.arange(c), np.arange(c)] = N_LEVELS
    return tri, level


_NT = (((1,), (1,)), ((), ()))
_TN = (((0,), (0,)), ((), ()))


def _neg_abs(x):
    bits = lax.bitcast_convert_type(x, jnp.int32) | jnp.int32(-2 ** 31)
    return lax.bitcast_convert_type(bits, F32)


def _scan_kernel(lbl_ref, tri_ref, level_ref, qf_ref, qb_ref, vf_ref, vb_ref, ff_ref, fb_ref,
                 s0_ref, of_ref, ob_ref, sfin_ref, st_ref, *, layer, n_chunks):
    c = CHUNK
    step = pl.program_id(2)

    @pl.when(step == 0)
    def _():
        st_ref[...] = s0_ref[0, 0]

    rows = [lbl_ref[:, j, :] for j in range(DEPTH)]
    mx = functools.reduce(jnp.maximum, rows)
    es = [jnp.exp(rw - mx) for rw in rows]
    tot = functools.reduce(lambda a, b: a + b, es)
    lb = jnp.zeros_like(mx)
    for j in range(1, layer + 1):
        lb = lb + es[j] / tot

    tri = tri_ref[...]
    level = level_ref[...]
    sub = lax.broadcasted_iota(jnp.int32, (c // 8, 8, HEAD_DIM), 1)

    def ref_rows(b, half, d):
        pick = half - 1 if d == 0 else half
        if half >= 8:
            parts = []
            for p0 in range(0, c, 2 * half):
                parts.append(jnp.broadcast_to(b[p0 + pick:p0 + pick + 1, :], (2 * half, HEAD_DIM)))
            return parts[0] if len(parts) == 1 else jnp.concatenate(parts, axis=0)
        b3 = b.reshape(c // 8, 8, HEAD_DIM)
        out = None
        for g0 in range(0, 8, 2 * half):
            cand = jnp.broadcast_to(b3[:, g0 + pick:g0 + pick + 1, :], b3.shape)
            out = cand if out is None else jnp.where(sub >= g0, cand, out)
        return out.reshape(c, HEAD_DIM)

    refs = ((qf_ref, vf_ref, ff_ref, of_ref), (qb_ref, vb_ref, fb_ref, ob_ref))

    def gates(d, row0):
        raw = refs[d][2][0, row0:row0 + c, :]
        lbd = lb[d:d + 1, :]
        f = lbd + (1.0 - lbd) * jax.nn.sigmoid(raw)
        logf = jnp.log(f)
        hi = logf.astype(BF16)
        rem = logf - hi.astype(F32)
        mid = rem.astype(BF16)
        lo = (rem - mid.astype(F32)).astype(BF16)
        b = jnp.dot(tri[d], jnp.concatenate([hi, mid, lo], axis=0), preferred_element_type=F32)
        return (1.0 - f).astype(BF16), b * LOG2_E

    def intra(d, row0, k_bf, b):
        q_bf = refs[d][0][0, row0:row0 + c, :]
        att = jnp.where(level[d] == N_LEVELS,
                        lax.dot_general(q_bf, k_bf, _NT, preferred_element_type=F32), 0.0)
        for lvl in range(N_LEVELS):
            half = c >> (lvl + 1)
            el = jnp.exp2(_neg_abs(b - ref_rows(b, half, d))).astype(BF16)
            att = jnp.where(level[d] == lvl,
                            lax.dot_general(q_bf * el, k_bf * el, _NT, preferred_element_type=F32), att)
        last = c - 1 if d == 0 else 0
        blast = b[last:last + 1, :]
        q_in = q_bf * jnp.exp2(b).astype(BF16)
        k_out = k_bf * jnp.exp2(blast - b).astype(BF16)
        return att.astype(BF16), q_in, k_out, jnp.exp2(blast)

    def readout(d, row0, att, q_in, k_out, decay, st):
        v = refs[d][1][0, row0:row0 + c, :]
        o = lax.dot_general(q_in, st.astype(BF16), _NT, preferred_element_type=F32)
        refs[d][3][0, row0:row0 + c, :] = o + jnp.dot(att, v, preferred_element_type=F32)
        return st * decay + lax.dot_general(v, k_out, _TN, preferred_element_type=F32)

    work = []
    for ci in range(n_chunks):
        work.append((0, ci * c))
        work.append((1, (n_chunks - 1 - ci) * c))
    st = [st_ref[0], st_ref[1]]
    stage1 = {}
    stage2 = {}
    for n in range(len(work) + 2):
        if n >= 2:
            d, row0 = work[n - 2]
            st[d] = readout(d, row0, *stage2.pop(n - 2), st[d])
        if 1 <= n <= len(work):
            d, row0 = work[n - 1]
            stage2[n - 1] = intra(d, row0, *stage1.pop(n - 1))
        if n < len(work):
            stage1[n] = gates(*work[n])
    st_ref[0] = st[0]
    st_ref[1] = st[1]

    @pl.when(step == pl.num_programs(2) - 1)
    def _():
        sfin_ref[0, 0] = st_ref[...]


def _scan_call(q3, f3, r3, lb_logits, s0, tri, level, layer, tl):
    bsz, length, _ = q3.shape
    nl = length // tl
    fwd = lambda b, h, i: (b, i, h)
    bwd = lambda b, h, i: (b, nl - 1 - i, h)
    bwd_f = lambda b, h, i: (b, nl - 1 - i, HEADS + h)
    blk = (1, tl, HEAD_DIM)
    return pl.pallas_call(
        functools.partial(_scan_kernel, layer=layer, n_chunks=tl // CHUNK),
        grid=(bsz, HEADS, nl),
        in_specs=[
            pl.BlockSpec((2, DEPTH, HEAD_DIM), lambda b, h, i: (0, 0, h)),
            pl.BlockSpec(tri.shape, lambda b, h, i: (0, 0, 0)),
            pl.BlockSpec(level.shape, lambda b, h, i: (0, 0, 0)),
            pl.BlockSpec(blk, fwd),
            pl.BlockSpec(blk, bwd),
            pl.BlockSpec(blk, fwd),
            pl.BlockSpec(blk, bwd),
            pl.BlockSpec(blk, fwd),
            pl.BlockSpec(blk, bwd_f),
            pl.BlockSpec((1, 1, 2, HEAD_DIM, HEAD_DIM), lambda b, h, i: (b, h, 0, 0, 0)),
        ],
        out_specs=[
            pl.BlockSpec(blk, fwd),
            pl.BlockSpec(blk, bwd),
            pl.BlockSpec((1, 1, 2, HEAD_DIM, HEAD_DIM), lambda b, h, i: (b, h, 0, 0, 0)),
        ],
        out_shape=[
            jax.ShapeDtypeStruct((bsz, length, HG_WIDTH), F32),
            jax.ShapeDtypeStruct((bsz, length, HG_WIDTH), F32),
            jax.ShapeDtypeStruct((bsz, HEADS, 2, HEAD_DIM, HEAD_DIM), F32),
        ],
        scratch_shapes=[pltpu.VMEM((2, HEAD_DIM, HEAD_DIM), F32)],
        compiler_params=_cparams(("arbitrary", "arbitrary", "arbitrary")),
        name="hgrn2_scan",
    )(lb_logits, tri, level, q3, q3, r3, r3, f3, f3, s0)


def _cconv_kernel(a_ref, b_ref, cw_ref, cb_ref, lw_ref, lb_ref, o_ref, u_ref, *, length, rows):
    zeros = jnp.zeros((CONV_PAD, HEAD_DIM), F32)
    u_ref[0:CONV_PAD, :] = zeros
    u_ref[CONV_PAD + length:2 * CONV_PAD + length, :] = zeros
    n_tiles = length // rows

    def fill(ti, carry):
        r0 = pl.multiple_of(ti * rows, rows)
        a = a_ref[0, pl.ds(r0, rows), :].astype(F32)
        b = b_ref[0, pl.ds(r0, rows), :].astype(F32)
        u_ref[pl.ds(CONV_PAD + r0, rows), :] = a * jax.nn.sigmoid(b)
        return carry

    lax.fori_loop(0, n_tiles, fill, 0)

    half = CONV_WIDTH // 2

    def conv(ti, carry):
        r0 = pl.multiple_of(ti * rows, rows)
        lead = CONV_PAD - half
        span = -(-(lead + CONV_WIDTH - 1) // 8) * 8 - 8
        acc = jnp.zeros((rows, HEAD_DIM), F32)
        for phase in range(8):
            window = u_ref[pl.ds(r0 + phase, rows + span), :]
            for off in range(phase, lead + CONV_WIDTH, 8):
                if off >= lead:
                    j = off - lead
                    acc = acc + cw_ref[j:j + 1, :] * window[off - phase:off - phase + rows]
        acc = acc + cb_ref[...]
        mu = jnp.mean(acc, axis=-1, keepdims=True)
        dlt = acc - mu
        var = jnp.mean(dlt * dlt, axis=-1, keepdims=True)
        y = dlt * lax.rsqrt(var + LN_EPS) * lw_ref[...] + lb_ref[...]
        o_ref[0, pl.ds(r0, rows), :] = _silu(y).astype(BF16)
        return carry

    lax.fori_loop(0, n_tiles, conv, 0, unroll=4)


def _cconv_call(r3, cw, cb, lw, lb):
    bsz, length, _ = r3.shape
    groups = CONV_CH // HEAD_DIM
    a_col = 2 * HG_WIDTH // HEAD_DIM
    b_col = a_col + groups
    vec = pl.BlockSpec((1, HEAD_DIM), lambda b, g: (0, g))
    return pl.pallas_call(
        functools.partial(_cconv_kernel, length=length, rows=128),
        grid=(bsz, groups),
        in_specs=[
            pl.BlockSpec((1, length, HEAD_DIM), lambda b, g: (b, 0, a_col + g)),
            pl.BlockSpec((1, length, HEAD_DIM), lambda b, g: (b, 0, b_col + g)),
            pl.BlockSpec((CONV_WIDTH, HEAD_DIM), lambda b, g: (0, g)),
            vec, vec, vec,
        ],
        out_specs=pl.BlockSpec((1, length, HEAD_DIM), lambda b, g: (b, 0, g)),
        out_shape=jax.ShapeDtypeStruct((bsz, length, CONV_CH), BF16),
        scratch_shapes=[pltpu.VMEM((length + 2 * CONV_PAD, HEAD_DIM), F32)],
        compiler_params=_cparams(("arbitrary", "arbitrary")),
        name="conformer_conv",
    )(r3, r3, cw, cb.reshape(1, -1), lw.reshape(1, -1), lb.reshape(1, -1))


def _outproj_kernel(of_ref, ob_ref, g_ref, cv_ref, x_ref, mod_ref, hgw_ref, nw_ref, nw2_ref, w_ref, o_ref,
                    h_ref):
    o = of_ref[...] + ob_ref[...]
    heads = []
    for h in range(HEADS):
        oh = o[:, h * HEAD_DIM:(h + 1) * HEAD_DIM]
        heads.append(_rms_rows(oh) * hgw_ref[...])
    r = jnp.concatenate(heads, axis=-1) * _silu(g_ref[...].astype(F32))
    y = jnp.dot(r.astype(BF16), w_ref[0:HG_WIDTH, :], preferred_element_type=F32)
    y = y + jnp.dot(cv_ref[...], w_ref[HG_WIDTH:D_MODEL, :], preferred_element_type=F32)
    x_new = x_ref[...] + mod_ref[0, 2:3, :] * (_rms_rows(y) * nw_ref[...])
    o_ref[...] = x_new
    h_ref[...] = _prenorm(x_new, mod_ref, nw2_ref, 3)


def _outproj_call(of2, ob2, r2, cv2, x2, mod3, hgw, nw, nw2, w_bf, layer, tm, mod_row):
    m = x2.shape[0]
    return pl.pallas_call(
        _outproj_kernel,
        grid=(m // tm,),
        in_specs=[
            pl.BlockSpec((tm, HG_WIDTH), lambda i: (i, 0)),
            pl.BlockSpec((tm, HG_WIDTH), lambda i: (i, 0)),
            pl.BlockSpec((tm, HG_WIDTH), lambda i: (i, 1)),
            pl.BlockSpec((tm, CONV_CH), lambda i: (i, 0)),
            pl.BlockSpec((tm, D_MODEL), lambda i: (i, 0)),
            pl.BlockSpec((1, MOD_ROWS, D_MODEL), lambda i: (mod_row(i), 0, 0)),
            pl.BlockSpec((1, HEAD_DIM), lambda i: (0, 0)),
            pl.BlockSpec((1, D_MODEL), lambda i: (0, 0)),
            pl.BlockSpec((1, D_MODEL), lambda i: (0, 0)),
            pl.BlockSpec((None, D_MODEL, D_MODEL), lambda i: (layer, 0, 0), pipeline_mode=pl.Buffered(1)),
        ],
        out_specs=[
            pl.BlockSpec((tm, D_MODEL), lambda i: (i, 0)),
            pl.BlockSpec((tm, D_MODEL), lambda i: (i, 0)),
        ],
        out_shape=[
            jax.ShapeDtypeStruct((m, D_MODEL), F32),
            jax.ShapeDtypeStruct((m, D_MODEL), BF16),
        ],
        compiler_params=_cparams(("arbitrary",)),
        name="out_proj",
    )(of2, ob2, r2, cv2, x2, mod3, hgw, nw, nw2, w_bf)


ROW_BLOCK = 16
MXU_COLS = 256
MXU_ROWS = 256
FFN_CHUNK = 512
LANE_BLOCK = 256


def _gate_chunk(gp_ref, gm_ref, gn_ref, val_ref, cw_ref, cb_ref, act_out, first, last, mm_tile,
                mm_tiles, *, tm, grid_mode):
    tk = FFN_CHUNK
    rows = ROW_BLOCK
    lanes = LANE_BLOCK
    n_blocks = tm // rows
    total = n_blocks * (tk // lanes)
    row_id = lax.broadcasted_iota(jnp.int32, (rows, lanes), 0)
    edge = GRID_W if grid_mode else tm
    zero = jnp.zeros((), BF16)

    def gate_rows(r0, ls):
        if r0 < 0:
            return jnp.where(first, zero, gp_ref[GRID_W + r0:GRID_W + r0 + rows, ls])
        if r0 >= tm:
            return jnp.where(last, zero, gn_ref[r0 - tm:r0 - tm + rows, ls])
        return gm_ref[r0:r0 + rows, ls]

    done = 0
    for l0 in range(0, tk, lanes):
        ls = slice(l0, l0 + lanes)
        taps = (0.5 * cw_ref[:, :, ls]).astype(BF16)
        taps = [[jnp.broadcast_to(taps[dh, dw:dw + 1], (rows, lanes)) for dw in range(3)] for dh in range(3)]
        bias = 0.5 * cb_ref[:, ls]

        def columns(r0):
            ce = gate_rows(r0, ls)
            if not grid_mode:
                return [(taps[1][dw] * ce).astype(F32) for dw in range(3)]
            up = gate_rows(r0 - GRID_W, ls)
            dn = gate_rows(r0 + GRID_W, ls)
            return [(taps[0][dw] * up + taps[1][dw] * ce + taps[2][dw] * dn).astype(F32) for dw in range(3)]

        cur = columns(0)
        prev_left_row = None
        for blk in range(n_blocks):
            r0 = blk * rows
            for n in range(done * mm_tiles // total, (done + 1) * mm_tiles // total):
                mm_tile(n)
            done += 1
            nxt = columns(r0 + rows) if blk + 1 < n_blocks else None
            left, mid, right = cur
            from_prev = pltpu.roll(left, 1, axis=0)
            if r0 % edge == 0:
                from_prev = jnp.where(row_id == 0, 0.0, from_prev)
            else:
                from_prev = jnp.where(row_id == 0, prev_left_row, from_prev)
            from_next = pltpu.roll(right, rows - 1, axis=0)
            if (r0 + rows) % edge == 0:
                from_next = jnp.where(row_id == rows - 1, 0.0, from_next)
            else:
                from_next = jnp.where(row_id == rows - 1, nxt[2][0:1, :], from_next)
            half = mid + from_prev + from_next + bias
            gelu = half * (1.0 + lax.erf(half * np.float32(np.sqrt(2.0))))
            act_out[r0:r0 + rows, ls] = gelu.astype(BF16) * val_ref[r0:r0 + rows, ls]
            prev_left_row = left[rows - 1:rows, :]
            cur = nxt


def _ffn_down_kernel(gp_ref, gm_ref, gn_ref, val_ref, cw_ref, cb_ref, w_ref, x_ref, mod_ref, nw_ref, *rest,
                     tm, nk, tiles_per_seq, grid_mode, next_norm):
    if next_norm:
        modn_ref, nwn_ref, o_ref, h_ref, acc_ref, act0_ref, act1_ref = rest
    else:
        o_ref, acc_ref, act0_ref, act1_ref = rest
    s = pl.program_id(0)
    i = jnp.minimum(s, pl.num_programs(0) - 2) // nk
    km = jnp.maximum(s - 1, 0) % nk
    first = (i % tiles_per_seq) == 0
    last = (i % tiles_per_seq) == tiles_per_seq - 1

    @pl.when(s == 0)
    def _():
        act1_ref[...] = jnp.zeros_like(act1_ref)

    @pl.when(km == 0)
    def _():
        acc_ref[...] = jnp.zeros_like(acc_ref)

    def step(act_in, act_out):
        row_parts = tm // MXU_ROWS

        def mm_tile(n):
            rows = slice((n % row_parts) * MXU_ROWS, (n % row_parts + 1) * MXU_ROWS)
            cols = slice((n // row_parts) * MXU_COLS, (n // row_parts + 1) * MXU_COLS)
            acc_ref[rows, cols] += jnp.dot(act_in[rows, :], w_ref[:, cols], preferred_element_type=F32)

        _gate_chunk(gp_ref, gm_ref, gn_ref, val_ref, cw_ref, cb_ref, act_out, first, last, mm_tile,
                    row_parts * D_MODEL // MXU_COLS, tm=tm, grid_mode=grid_mode)

    @pl.when(s % 2 == 0)
    def _():
        step(act1_ref, act0_ref)

    @pl.when(s % 2 == 1)
    def _():
        step(act0_ref, act1_ref)

    @pl.when((km == nk - 1) & (s > 0))
    def _():
        x_new = x_ref[...] + mod_ref[0, 5:6, :] * (_rms_rows(acc_ref[...]) * nw_ref[...])
        o_ref[...] = x_new
        if next_norm:
            h_ref[...] = _prenorm(x_new, modn_ref, nwn_ref, 0)


def _ffn_down_call(gv2, cw, cb, w_bf, layer, x2, mod3, nw, tm, seq_len, mod_row, grid_mode, next_norm):
    m = x2.shape[0]
    tk = FFN_CHUNK
    nk = D_FF // tk
    hb = tm // GRID_W
    n_hblk = m // GRID_W
    tiles_per_seq = seq_len // tm
    n_steps = (m // tm) * nk + 1
    bi = lambda s: jnp.minimum(s, n_steps - 2) // nk
    bk = lambda s: jnp.minimum(s, n_steps - 2) % nk
    mi = lambda s: jnp.maximum(s - 1, 0) // nk
    mk = lambda s: jnp.maximum(s - 1, 0) % nk
    row_spec = pl.BlockSpec((tm, D_MODEL), lambda s: (mi(s), 0))
    mod_spec = pl.BlockSpec((1, MOD_ROWS, D_MODEL), lambda s: (mod_row(mi(s)), 0, 0))
    vec_spec = pl.BlockSpec((1, D_MODEL), lambda s: (0, 0))
    extra_in, extra_specs, out_specs = (), [], row_spec
    out_shape = jax.ShapeDtypeStruct((m, D_MODEL), F32)
    if next_norm:
        extra_in, extra_specs = tuple(next_norm), [mod_spec, vec_spec]
        out_specs = [row_spec, row_spec]
        out_shape = [out_shape, jax.ShapeDtypeStruct((m, D_MODEL), BF16)]
    return pl.pallas_call(
        functools.partial(_ffn_down_kernel, tm=tm, nk=nk, tiles_per_seq=tiles_per_seq, grid_mode=grid_mode,
                          next_norm=bool(next_norm)),
        grid=(n_steps,),
        in_specs=[
            pl.BlockSpec((GRID_W, tk), lambda s: (jnp.maximum(bi(s) * hb - 1, 0), bk(s))),
            pl.BlockSpec((tm, tk), lambda s: (bi(s), bk(s))),
            pl.BlockSpec((GRID_W, tk), lambda s: (jnp.minimum((bi(s) + 1) * hb, n_hblk - 1), bk(s))),
            pl.BlockSpec((tm, tk), lambda s: (bi(s), nk + bk(s))),
            pl.BlockSpec((3, 3, tk), lambda s: (0, 0, bk(s))),
            pl.BlockSpec((1, tk), lambda s: (0, bk(s))),
            pl.BlockSpec((None, tk, D_MODEL), lambda s: (layer, mk(s), 0)),
            row_spec,
            mod_spec,
            vec_spec,
        ] + extra_specs,
        out_specs=out_specs,
        out_shape=out_shape,
        scratch_shapes=[
            pltpu.VMEM((tm, D_MODEL), F32),
            pltpu.VMEM((tm, tk), BF16),
            pltpu.VMEM((tm, tk), BF16),
        ],
        compiler_params=_cparams(("arbitrary",)),
        name="ffn_down",
    )(gv2, gv2, gv2, gv2, cw, cb.reshape(1, -1), w_bf, x2, mod3, nw, *extra_in)


class _Tiles(NamedTuple):
    proj: int
    proj_f32: int
    norm: int
    out: int
    down: int
    scan: int
    cols: int


def _stream(x3, h2, mod, l, params, s0, consts, tiles, row_of, grid_mode, full, emit_next):
    bsz, length, _ = x3.shape
    m = bsz * length
    x2 = x3.reshape(m, D_MODEL)
    mod3 = mod[l]
    nw = params["norm_w"][l]
    tn = tiles.cols
    w_in = params["w_in"]
    if h2 is None:
        h2 = _prenorm_call(x2, mod3, nw[0:1], tiles.norm, row_of(tiles.norm), 0, "mix_prenorm")
    f2 = _proj_call(h2, w_in, l, HG_WIDTH, 2 * HG_WIDTH, tiles.proj_f32, tn, False, F32, "in_proj_f")
    if full:
        q2 = _proj_call(h2, w_in, l, 0, HG_WIDTH, tiles.proj, tn, True, BF16, "in_proj_q")
        r2 = _proj_call(h2, w_in, l, 3 * HG_WIDTH, IN_COLS - 3 * HG_WIDTH, tiles.proj, tn, False, BF16,
                        "in_proj_vgab")
    else:
        r2 = _proj_call(h2, w_in, l, 3 * HG_WIDTH, HG_WIDTH, tiles.proj, tn, False, BF16, "in_proj_v")
        q2 = r2
    tri, level = consts
    of3, ob3, sfin = _scan_call(q2.reshape(bsz, length, -1), f2.reshape(bsz, length, -1),
                                r2.reshape(bsz, length, -1), params["lb_logits"], s0, tri, level, l,
                                tiles.scan)
    if not full:
        return None, None, sfin
    cv3 = _cconv_call(r2.reshape(bsz, length, -1), params["conv_w"][l], params["conv_b"][l],
                      params["conv_ln_w"][l], params["conv_ln_b"][l])
    x2, h2 = _outproj_call(of3.reshape(m, -1), ob3.reshape(m, -1), r2, cv3.reshape(m, -1), x2, mod3,
                           params["hg_norm_w"][l].reshape(1, -1), nw[1:2], nw[2:3], params["w_out"], l,
                           tiles.out, row_of(tiles.out))
    gv2 = _proj_call(h2, params["ffn_up"], l, 0, 2 * D_FF, tiles.proj, tn, False, BF16, "ffn_up")
    next_norm = (mod[l + 1], params["norm_w"][l + 1][0:1]) if emit_next else None
    out = _ffn_down_call(gv2, params["ffn_conv_w"][l], params["ffn_conv_b"][l], params["ffn_down"], l,
                         x2, mod3, nw[3:4], tiles.down, length, row_of(tiles.down), grid_mode, next_norm)
    x2, h_next = out if emit_next else (out, None)
    return x2.reshape(bsz, length, D_MODEL), h_next, sfin


def kernel(x, c, ctx, c_ctx, w_mod, b_mod, norm_w, w_in, lb_logits, hg_norm_w, conv_w, conv_b, conv_ln_w,
           conv_ln_b, w_out, ffn_up, ffn_conv_w, ffn_conv_b, ffn_down):
    bsz, seq, _ = x.shape
    ctx_len = ctx.shape[1]
    params = dict(norm_w=norm_w, w_in=w_in, lb_logits=lb_logits, hg_norm_w=hg_norm_w,
                  conv_w=conv_w, conv_b=conv_b, conv_ln_w=conv_ln_w, conv_ln_b=conv_ln_b,
                  w_out=w_out.astype(BF16), ffn_up=ffn_up, ffn_conv_w=ffn_conv_w,
                  ffn_conv_b=ffn_conv_b, ffn_down=ffn_down.astype(BF16))
    tri_np, level_np = _scan_constants()
    consts = (jnp.asarray(tri_np, BF16), jnp.asarray(level_np, jnp.int32))

    cond = jnp.concatenate([c, c_ctx[None, :], jnp.zeros((MOD_ROWS - bsz - 1, D_MODEL), c.dtype)], axis=0)
    mod = _mod_call(_silu(cond).astype(BF16), w_mod, b_mod)
    mod = mod[:, :bsz + 1].reshape(DEPTH, bsz + 1, N_MOD, D_MODEL)
    mod = jnp.pad(mod, ((0, 0), (0, 0), (0, MOD_ROWS - N_MOD), (0, 0)))

    lat_row = lambda tm: (lambda i: i // (seq // tm))
    ctx_row = lambda tm: (lambda i: bsz)
    zero_state = jnp.zeros((bsz, HEADS, 2, HEAD_DIM, HEAD_DIM), F32)
    ctx_rows = bsz * ctx_len
    ctx_tiles = _Tiles(proj=ctx_rows, proj_f32=ctx_rows, norm=ctx_len, out=ctx_len, down=ctx_len,
                       scan=ctx_len, cols=1024)
    lat_tiles = _Tiles(proj=2048, proj_f32=1024, norm=512, out=512, down=512, scan=1024, cols=1024)
    xc, hc, h = ctx, None, None
    for l in range(DEPTH):
        last = l == DEPTH - 1
        xc, hc, s_ctx = _stream(xc, hc, mod, l, params, zero_state, consts, ctx_tiles, ctx_row,
                                grid_mode=False, full=not last, emit_next=not last)
        x, h, _ = _stream(x, h, mod, l, params, s_ctx, consts, lat_tiles, lat_row,
                          grid_mode=True, full=True, emit_next=not last)
    return x
```

```python
import functools
from typing import NamedTuple

import numpy as np
import jax
import jax.numpy as jnp
from jax import lax
from jax.experimental import pallas as pl
from jax.experimental.pallas import tpu as pltpu

F32 = jnp.float32
BF16 = jnp.bfloat16

D_MODEL = 2048
DEPTH = 2
GRID_W = 64
HEADS = 8
HEAD_DIM = 128
HG_WIDTH = HEADS * HEAD_DIM
CONV_CH = D_MODEL - HG_WIDTH
CONV_WIDTH = 31
CONV_PAD = 16
D_FF = 5632
N_MOD = 6
MOD_ROWS = 8
EPS = 1e-6
LN_EPS = 1e-5
LOG2_E = float(np.log2(np.e))
IN_COLS = 3 * HG_WIDTH + 2 * HG_WIDTH + 2 * CONV_CH

CHUNK = 64
N_LEVELS = 6

VMEM_LIMIT = 56 * 1024 * 1024


def _cparams(sem):
    return pltpu.CompilerParams(dimension_semantics=sem, vmem_limit_bytes=VMEM_LIMIT)


def _silu(x):
    return x * jax.nn.sigmoid(x)


def _rms_rows(x):
    return x * lax.rsqrt(jnp.mean(x * x, axis=-1, keepdims=True) + EPS)


def _mod_kernel(s_ref, w_ref, b_ref, o_ref):
    w = w_ref[0].astype(BF16)
    o_ref[0] = jnp.dot(s_ref[...], w, preferred_element_type=F32) + b_ref[0]


def _mod_call(s_rows, w_mod, b_mod):
    tn = 1024
    n = w_mod.shape[-1]
    return pl.pallas_call(
        _mod_kernel,
        grid=(DEPTH, n // tn),
        in_specs=[
            pl.BlockSpec((MOD_ROWS, D_MODEL), lambda l, j: (0, 0)),
            pl.BlockSpec((1, D_MODEL, tn), lambda l, j: (l, 0, j)),
            pl.BlockSpec((1, 1, tn), lambda l, j: (l, 0, j)),
        ],
        out_specs=pl.BlockSpec((1, MOD_ROWS, tn), lambda l, j: (l, 0, j)),
        out_shape=jax.ShapeDtypeStruct((DEPTH, MOD_ROWS, n), F32),
        compiler_params=_cparams(("arbitrary", "arbitrary")),
        name="mod_matmul",
    )(s_rows, w_mod, b_mod.reshape(DEPTH, 1, n))


def _prenorm(x, mod_ref, nw_ref, shift_row):
    y = _rms_rows(x) * nw_ref[...]
    h = y * (1.0 + mod_ref[0, shift_row + 1:shift_row + 2, :]) + mod_ref[0, shift_row:shift_row + 1, :]
    return h.astype(BF16)


def _prenorm_kernel(x_ref, mod_ref, nw_ref, h_ref, *, shift_row):
    h_ref[...] = _prenorm(x_ref[...], mod_ref, nw_ref, shift_row)


def _prenorm_call(x2, mod3, nw, tm, mod_row, shift_row, name):
    m = x2.shape[0]
    return pl.pallas_call(
        functools.partial(_prenorm_kernel, shift_row=shift_row),
        grid=(m // tm,),
        in_specs=[
            pl.BlockSpec((tm, D_MODEL), lambda i: (i, 0)),
            pl.BlockSpec((1, MOD_ROWS, D_MODEL), lambda i: (mod_row(i), 0, 0)),
            pl.BlockSpec((1, D_MODEL), lambda i: (0, 0)),
        ],
        out_specs=pl.BlockSpec((tm, D_MODEL), lambda i: (i, 0)),
        out_shape=jax.ShapeDtypeStruct((m, D_MODEL), BF16),
        compiler_params=_cparams(("arbitrary",)),
        name=name,
    )(x2, mod3, nw)


def _proj_kernel(h_ref, w_ref, o_ref, wb_ref, *, act):
    @pl.when(pl.program_id(1) == 0)
    def _():
        wb_ref[...] = w_ref[...].astype(BF16)

    acc = jnp.dot(h_ref[...], wb_ref[...], preferred_element_type=F32)
    if act:
        acc = _silu(acc)
    o_ref[...] = acc.astype(o_ref.dtype)


def _proj_call(h2, w, layer, col0, n_cols, tm, tn, act, out_dtype, name):
    m = h2.shape[0]
    c0 = col0 // tn
    return pl.pallas_call(
        functools.partial(_proj_kernel, act=act),
        grid=(n_cols // tn, m // tm),
        in_specs=[
            pl.BlockSpec((tm, D_MODEL), lambda j, i: (i, 0)),
            pl.BlockSpec((None, D_MODEL, tn), lambda j, i: (layer, 0, c0 + j)),
        ],
        out_specs=pl.BlockSpec((tm, tn), lambda j, i: (i, j)),
        out_shape=jax.ShapeDtypeStruct((m, n_cols), out_dtype),
        scratch_shapes=[pltpu.VMEM((D_MODEL, tn), BF16)],
        compiler_params=_cparams(("arbitrary", "arbitrary")),
        name=name,
    )(h2, w)


def _scan_constants():
    c = CHUNK
    t = np.arange(c)[:, None]
    r = np.arange(c)[None, :]
    level = np.full((2, c, c), -1, np.int32)
    for lvl in range(N_LEVELS):
        half = c >> (lvl + 1)
        same = (t // (2 * half)) == (r // (2 * half))
        t_late = (t % (2 * half)) >= half
        r_late = (r % (2 * half)) >= half
        level[0][same & t_late & ~r_late] = lvl
        level[1][same & ~t_late & r_late] = lvl
    level[:, np.arange(c), np.arange(c)] = N_LEVELS
    return level


_NT = (((1,), (1,)), ((), ()))
_TN = (((0,), (0,)), ((), ()))


def _neg_abs(x):
    bits = lax.bitcast_convert_type(x, jnp.int32) | jnp.int32(-2 ** 31)
    return lax.bitcast_convert_type(bits, F32)


def _scan_kernel(lbl_ref, level_ref, qf_ref, qb_ref, vf_ref, vb_ref, ff_ref, fb_ref,
                 s0_ref, of_ref, ob_ref, sfin_ref, st_ref, *, layer, n_chunks):
    c = CHUNK
    step = pl.program_id(2)

    @pl.when(step == 0)
    def _():
        st_ref[...] = s0_ref[0, 0]

    rows = [lbl_ref[:, j, :] for j in range(DEPTH)]
    mx = functools.reduce(jnp.maximum, rows)
    es = [jnp.exp(rw - mx) for rw in rows]
    tot = functools.reduce(lambda a, b: a + b, es)
    lb = jnp.zeros_like(mx)
    for j in range(1, layer + 1):
        lb = lb + es[j] / tot

    level = level_ref[...]
    sub = lax.broadcasted_iota(jnp.int32, (c // 8, 8, HEAD_DIM), 1)

    def ref_rows(b, half, d):
        pick = half - 1 if d == 0 else half
        if half >= 8:
            parts = []
            for p0 in range(0, c, 2 * half):
                parts.append(jnp.broadcast_to(b[p0 + pick:p0 + pick + 1, :], (2 * half, HEAD_DIM)))
            return parts[0] if len(parts) == 1 else jnp.concatenate(parts, axis=0)
        b3 = b.reshape(c // 8, 8, HEAD_DIM)
        out = None
        for g0 in range(0, 8, 2 * half):
            cand = jnp.broadcast_to(b3[:, g0 + pick:g0 + pick + 1, :], b3.shape)
            out = cand if out is None else jnp.where(sub >= g0, cand, out)
        return out.reshape(c, HEAD_DIM)

    refs = ((qf_ref, vf_ref, ff_ref, of_ref), (qb_ref, vb_ref, fb_ref, ob_ref))

    def gates(d, row0):
        raw = refs[d][2][0, row0:row0 + c, :]
        lbd = lb[d:d + 1, :]
        f = lbd + (1.0 - lbd) * jax.nn.sigmoid(raw)
        x = (jnp.log(f) * LOG2_E).reshape(c // 8, 8, HEAD_DIM)
        for s in (1, 2, 4):
            if d == 0:
                x = x + jnp.where(sub >= s, pltpu.roll(x, s, axis=1), 0.0)
            else:
                x = x + jnp.where(sub < 8 - s, pltpu.roll(x, 8 - s, axis=1), 0.0)
        groups = [x[g] for g in range(c // 8)]
        edge = 7 if d == 0 else 0
        totals = [grp[edge:edge + 1, :] for grp in groups]
        order = range(1, c // 8) if d == 0 else range(c // 8 - 2, -1, -1)
        carry = None
        for g in order:
            before = totals[g - 1] if d == 0 else totals[g + 1]
            carry = before if carry is None else carry + before
            groups[g] = groups[g] + carry
        return (1.0 - f).astype(BF16), jnp.concatenate(groups, axis=0)

    def intra(d, row0, k_bf, b):
        q_bf = refs[d][0][0, row0:row0 + c, :]
        att = jnp.where(level[d] == N_LEVELS,
                        lax.dot_general(q_bf, k_bf, _NT, preferred_element_type=F32), 0.0)
        for lvl in range(N_LEVELS):
            half = c >> (lvl + 1)
            el = jnp.exp2(_neg_abs(b - ref_rows(b, half, d))).astype(BF16)
            att = jnp.where(level[d] == lvl,
                            lax.dot_general(q_bf * el, k_bf * el, _NT, preferred_element_type=F32), att)
        last = c - 1 if d == 0 else 0
        blast = b[last:last + 1, :]
        q_in = q_bf * jnp.exp2(b).astype(BF16)
        k_out = k_bf * jnp.exp2(blast - b).astype(BF16)
        return att.astype(BF16), q_in, k_out, jnp.exp2(blast)

    def readout(d, row0, att, q_in, k_out, decay, st):
        v = refs[d][1][0, row0:row0 + c, :]
        o = lax.dot_general(q_in, st.astype(BF16), _NT, preferred_element_type=F32)
        refs[d][3][0, row0:row0 + c, :] = o + jnp.dot(att, v, preferred_element_type=F32)
        return st * decay + lax.dot_general(v, k_out, _TN, preferred_element_type=F32)

    work = []
    for ci in range(n_chunks):
        work.append((0, ci * c))
        work.append((1, (n_chunks - 1 - ci) * c))
    st = [st_ref[0], st_ref[1]]
    stage1 = {}
    stage2 = {}
    for n in range(len(work) + 2):
        if n >= 2:
            d, row0 = work[n - 2]
            st[d] = readout(d, row0, *stage2.pop(n - 2), st[d])
        if 1 <= n <= len(work):
            d, row0 = work[n - 1]
            stage2[n - 1] = intra(d, row0, *stage1.pop(n - 1))
        if n < len(work):
            stage1[n] = gates(*work[n])
    st_ref[0] = st[0]
    st_ref[1] = st[1]

    @pl.when(step == pl.num_programs(2) - 1)
    def _():
        sfin_ref[0, 0] = st_ref[...]


def _scan_call(q3, f3, r3, lb_logits, s0, level, layer, tl):
    bsz, length, _ = q3.shape
    nl = length // tl
    fwd = lambda b, h, i: (b, i, h)
    bwd = lambda b, h, i: (b, nl - 1 - i, h)
    bwd_f = lambda b, h, i: (b, nl - 1 - i, HEADS + h)
    blk = (1, tl, HEAD_DIM)
    return pl.pallas_call(
        functools.partial(_scan_kernel, layer=layer, n_chunks=tl // CHUNK),
        grid=(bsz, HEADS, nl),
        in_specs=[
            pl.BlockSpec((2, DEPTH, HEAD_DIM), lambda b, h, i: (0, 0, h)),
            pl.BlockSpec(level.shape, lambda b, h, i: (0, 0, 0)),
            pl.BlockSpec(blk, fwd),
            pl.BlockSpec(blk, bwd),
            pl.BlockSpec(blk, fwd),
            pl.BlockSpec(blk, bwd),
            pl.BlockSpec(blk, fwd),
            pl.BlockSpec(blk, bwd_f),
            pl.BlockSpec((1, 1, 2, HEAD_DIM, HEAD_DIM), lambda b, h, i: (b, h, 0, 0, 0)),
        ],
        out_specs=[
            pl.BlockSpec(blk, fwd),
            pl.BlockSpec(blk, bwd),
            pl.BlockSpec((1, 1, 2, HEAD_DIM, HEAD_DIM), lambda b, h, i: (b, h, 0, 0, 0)),
        ],
        out_shape=[
            jax.ShapeDtypeStruct((bsz, length, HG_WIDTH), F32),
            jax.ShapeDtypeStruct((bsz, length, HG_WIDTH), F32),
            jax.ShapeDtypeStruct((bsz, HEADS, 2, HEAD_DIM, HEAD_DIM), F32),
        ],
        scratch_shapes=[pltpu.VMEM((2, HEAD_DIM, HEAD_DIM), F32)],
        compiler_params=_cparams(("arbitrary", "arbitrary", "arbitrary")),
        name="hgrn2_scan",
    )(lb_logits, level, q3, q3, r3, r3, f3, f3, s0)


def _cconv_kernel(a_ref, b_ref, cw_ref, cb_ref, lw_ref, lb_ref, o_ref, u_ref, *, length, rows):
    zeros = jnp.zeros((CONV_PAD, HEAD_DIM), F32)
    u_ref[0:CONV_PAD, :] = zeros
    u_ref[CONV_PAD + length:2 * CONV_PAD + length, :] = zeros
    n_tiles = length // rows

    def fill(ti, carry):
        r0 = pl.multiple_of(ti * rows, rows)
        a = a_ref[0, pl.ds(r0, rows), :].astype(F32)
        b = b_ref[0, pl.ds(r0, rows), :].astype(F32)
        u_ref[pl.ds(CONV_PAD + r0, rows), :] = a * jax.nn.sigmoid(b)
        return carry

    lax.fori_loop(0, n_tiles, fill, 0)

    half = CONV_WIDTH // 2

    def conv(ti, carry):
        r0 = pl.multiple_of(ti * rows, rows)
        lead = CONV_PAD - half
        span = -(-(lead + CONV_WIDTH - 1) // 8) * 8 - 8
        acc = jnp.zeros((rows, HEAD_DIM), F32)
        for phase in range(8):
            window = u_ref[pl.ds(r0 + phase, rows + span), :]
            for off in range(phase, lead + CONV_WIDTH, 8):
                if off >= lead:
                    j = off - lead
                    acc = acc + cw_ref[j:j + 1, :] * window[off - phase:off - phase + rows]
        acc = acc + cb_ref[...]
        mu = jnp.mean(acc, axis=-1, keepdims=True)
        dlt = acc - mu
        var = jnp.mean(dlt * dlt, axis=-1, keepdims=True)
        y = dlt * lax.rsqrt(var + LN_EPS) * lw_ref[...] + lb_ref[...]
        o_ref[0, pl.ds(r0, rows), :] = _silu(y).astype(BF16)
        return carry

    lax.fori_loop(0, n_tiles, conv, 0, unroll=4)


def _cconv_call(r3, cw, cb, lw, lb):
    bsz, length, _ = r3.shape
    groups = CONV_CH // HEAD_DIM
    a_col = 2 * HG_WIDTH // HEAD_DIM
    b_col = a_col + groups
    vec = pl.BlockSpec((1, HEAD_DIM), lambda b, g: (0, g))
    return pl.pallas_call(
        functools.partial(_cconv_kernel, length=length, rows=128),
        grid=(bsz, groups),
        in_specs=[
            pl.BlockSpec((1, length, HEAD_DIM), lambda b, g: (b, 0, a_col + g)),
            pl.BlockSpec((1, length, HEAD_DIM), lambda b, g: (b, 0, b_col + g)),
            pl.BlockSpec((CONV_WIDTH, HEAD_DIM), lambda b, g: (0, g)),
            vec, vec, vec,
        ],
        out_specs=pl.BlockSpec((1, length, HEAD_DIM), lambda b, g: (b, 0, g)),
        out_shape=jax.ShapeDtypeStruct((bsz, length, CONV_CH), BF16),
        scratch_shapes=[pltpu.VMEM((length + 2 * CONV_PAD, HEAD_DIM), F32)],
        compiler_params=_cparams(("arbitrary", "arbitrary")),
        name="conformer_conv",
    )(r3, r3, cw, cb.reshape(1, -1), lw.reshape(1, -1), lb.reshape(1, -1))


def _outproj_kernel(of_ref, ob_ref, g_ref, cv_ref, x_ref, mod_ref, hgw_ref, nw_ref, nw2_ref, w_ref, o_ref,
                    h_ref):
    o = of_ref[...] + ob_ref[...]
    heads = []
    for h in range(HEADS):
        oh = o[:, h * HEAD_DIM:(h + 1) * HEAD_DIM]
        heads.append(_rms_rows(oh) * hgw_ref[...])
    r = jnp.concatenate(heads, axis=-1) * _silu(g_ref[...].astype(F32))
    y = jnp.dot(r.astype(BF16), w_ref[0:HG_WIDTH, :], preferred_element_type=F32)
    y = y + jnp.dot(cv_ref[...], w_ref[HG_WIDTH:D_MODEL, :], preferred_element_type=F32)
    x_new = x_ref[...] + mod_ref[0, 2:3, :] * (_rms_rows(y) * nw_ref[...])
    o_ref[...] = x_new
    h_ref[...] = _prenorm(x_new, mod_ref, nw2_ref, 3)


def _outproj_call(of2, ob2, r2, cv2, x2, mod3, hgw, nw, nw2, w_bf, layer, tm, mod_row):
    m = x2.shape[0]
    return pl.pallas_call(
        _outproj_kernel,
        grid=(m // tm,),
        in_specs=[
            pl.BlockSpec((tm, HG_WIDTH), lambda i: (i, 0)),
            pl.BlockSpec((tm, HG_WIDTH), lambda i: (i, 0)),
            pl.BlockSpec((tm, HG_WIDTH), lambda i: (i, 1)),
            pl.BlockSpec((tm, CONV_CH), lambda i: (i, 0)),
            pl.BlockSpec((tm, D_MODEL), lambda i: (i, 0)),
            pl.BlockSpec((1, MOD_ROWS, D_MODEL), lambda i: (mod_row(i), 0, 0)),
            pl.BlockSpec((1, HEAD_DIM), lambda i: (0, 0)),
            pl.BlockSpec((1, D_MODEL), lambda i: (0, 0)),
            pl.BlockSpec((1, D_MODEL), lambda i: (0, 0)),
            pl.BlockSpec((None, D_MODEL, D_MODEL), lambda i: (layer, 0, 0), pipeline_mode=pl.Buffered(1)),
        ],
        out_specs=[
            pl.BlockSpec((tm, D_MODEL), lambda i: (i, 0)),
            pl.BlockSpec((tm, D_MODEL), lambda i: (i, 0)),
        ],
        out_shape=[
            jax.ShapeDtypeStruct((m, D_MODEL), F32),
            jax.ShapeDtypeStruct((m, D_MODEL), BF16),
        ],
        compiler_params=_cparams(("arbitrary",)),
        name="out_proj",
    )(of2, ob2, r2, cv2, x2, mod3, hgw, nw, nw2, w_bf)


ROW_BLOCK = 16
MXU_COLS = 256
MXU_ROWS = 256
FFN_CHUNK = 512
LANE_BLOCK = 256


def _gate_chunk(gp_ref, gm_ref, gn_ref, val_ref, cw_ref, cb_ref, act_out, first, last, mm_tile,
                mm_tiles, *, tm, grid_mode):
    tk = FFN_CHUNK
    rows = ROW_BLOCK
    lanes = LANE_BLOCK
    n_blocks = tm // rows
    total = n_blocks * (tk // lanes)
    row_id = lax.broadcasted_iota(jnp.int32, (rows, lanes), 0)
    edge = GRID_W if grid_mode else tm
    zero = jnp.zeros((), BF16)

    def gate_rows(r0, ls):
        if r0 < 0:
            return jnp.where(first, zero, gp_ref[GRID_W + r0:GRID_W + r0 + rows, ls])
        if r0 >= tm:
            return jnp.where(last, zero, gn_ref[r0 - tm:r0 - tm + rows, ls])
        return gm_ref[r0:r0 + rows, ls]

    done = 0
    for l0 in range(0, tk, lanes):
        ls = slice(l0, l0 + lanes)
        taps = (0.5 * cw_ref[:, :, ls]).astype(BF16)
        taps = [[jnp.broadcast_to(taps[dh, dw:dw + 1], (rows, lanes)) for dw in range(3)] for dh in range(3)]
        bias = 0.5 * cb_ref[:, ls]

        def columns(r0):
            ce = gate_rows(r0, ls)
            if not grid_mode:
                return [(taps[1][dw] * ce).astype(F32) for dw in range(3)]
            up = gate_rows(r0 - GRID_W, ls)
            dn = gate_rows(r0 + GRID_W, ls)
            return [(taps[0][dw] * up + taps[1][dw] * ce + taps[2][dw] * dn).astype(F32) for dw in range(3)]

        cur = columns(0)
        prev_left_row = None
        for blk in range(n_blocks):
            r0 = blk * rows
            for n in range(done * mm_tiles // total, (done + 1) * mm_tiles // total):
                mm_tile(n)
            done += 1
            nxt = columns(r0 + rows) if blk + 1 < n_blocks else None
            left, mid, right = cur
            from_prev = pltpu.roll(left, 1, axis=0)
            if r0 % edge == 0:
                from_prev = jnp.where(row_id == 0, 0.0, from_prev)
            else:
                from_prev = jnp.where(row_id == 0, prev_left_row, from_prev)
            from_next = pltpu.roll(right, rows - 1, axis=0)
            if (r0 + rows) % edge == 0:
                from_next = jnp.where(row_id == rows - 1, 0.0, from_next)
            else:
                from_next = jnp.where(row_id == rows - 1, nxt[2][0:1, :], from_next)
            half = mid + from_prev + from_next + bias
            gelu = half * (1.0 + lax.erf(half * np.float32(np.sqrt(2.0))))
            act_out[r0:r0 + rows, ls] = gelu.astype(BF16) * val_ref[r0:r0 + rows, ls]
            prev_left_row = left[rows - 1:rows, :]
            cur = nxt


def _ffn_down_kernel(gp_ref, gm_ref, gn_ref, val_ref, cw_ref, cb_ref, w_ref, x_ref, mod_ref, nw_ref, *rest,
                     tm, nk, tiles_per_seq, grid_mode, next_norm):
    if next_norm:
        modn_ref, nwn_ref, o_ref, h_ref, acc_ref, act0_ref, act1_ref = rest
    else:
        o_ref, acc_ref, act0_ref, act1_ref = rest
    s = pl.program_id(0)
    i = jnp.minimum(s, pl.num_programs(0) - 2) // nk
    km = jnp.maximum(s - 1, 0) % nk
    first = (i % tiles_per_seq) == 0
    last = (i % tiles_per_seq) == tiles_per_seq - 1

    @pl.when(s == 0)
    def _():
        act1_ref[...] = jnp.zeros_like(act1_ref)

    @pl.when(km == 0)
    def _():
        acc_ref[...] = jnp.zeros_like(acc_ref)

    def step(act_in, act_out):
        row_parts = tm // MXU_ROWS

        def mm_tile(n):
            rows = slice((n % row_parts) * MXU_ROWS, (n % row_parts + 1) * MXU_ROWS)
            cols = slice((n // row_parts) * MXU_COLS, (n // row_parts + 1) * MXU_COLS)
            acc_ref[rows, cols] += jnp.dot(act_in[rows, :], w_ref[:, cols], preferred_element_type=F32)

        _gate_chunk(gp_ref, gm_ref, gn_ref, val_ref, cw_ref, cb_ref, act_out, first, last, mm_tile,
                    row_parts * D_MODEL // MXU_COLS, tm=tm, grid_mode=grid_mode)

    @pl.when(s % 2 == 0)
    def _():
        step(act1_ref, act0_ref)

    @pl.when(s % 2 == 1)
    def _():
        step(act0_ref, act1_ref)

    @pl.when((km == nk - 1) & (s > 0))
    def _():
        x_new = x_ref[...] + mod_ref[0, 5:6, :] * (_rms_rows(acc_ref[...]) * nw_ref[...])
        o_ref[...] = x_new
        if next_norm:
            h_ref[...] = _prenorm(x_new, modn_ref, nwn_ref, 0)


def _ffn_down_call(gv2, cw, cb, w_bf, layer, x2, mod3, nw, tm, seq_len, mod_row, grid_mode, next_norm):
    m = x2.shape[0]
    tk = FFN_CHUNK
    nk = D_FF // tk
    hb = tm // GRID_W
    n_hblk = m // GRID_W
    tiles_per_seq = seq_len // tm
    n_steps = (m // tm) * nk + 1
    bi = lambda s: jnp.minimum(s, n_steps - 2) // nk
    bk = lambda s: jnp.minimum(s, n_steps - 2) % nk
    mi = lambda s: jnp.maximum(s - 1, 0) // nk
    mk = lambda s: jnp.maximum(s - 1, 0) % nk
    row_spec = pl.BlockSpec((tm, D_MODEL), lambda s: (mi(s), 0))
    mod_spec = pl.BlockSpec((1, MOD_ROWS, D_MODEL), lambda s: (mod_row(mi(s)), 0, 0))
    vec_spec = pl.BlockSpec((1, D_MODEL), lambda s: (0, 0))
    extra_in, extra_specs, out_specs = (), [], row_spec
    out_shape = jax.ShapeDtypeStruct((m, D_MODEL), F32)
    if next_norm:
        extra_in, extra_specs = tuple(next_norm), [mod_spec, vec_spec]
        out_specs = [row_spec, row_spec]
        out_shape = [out_shape, jax.ShapeDtypeStruct((m, D_MODEL), BF16)]
    return pl.pallas_call(
        functools.partial(_ffn_down_kernel, tm=tm, nk=nk, tiles_per_seq=tiles_per_seq, grid_mode=grid_mode,
                          next_norm=bool(next_norm)),
        grid=(n_steps,),
        in_specs=[
            pl.BlockSpec((GRID_W, tk), lambda s: (jnp.maximum(bi(s) * hb - 1, 0), bk(s))),
            pl.BlockSpec((tm, tk), lambda s: (bi(s), bk(s))),
            pl.BlockSpec((GRID_W, tk), lambda s: (jnp.minimum((bi(s) + 1) * hb, n_hblk - 1), bk(s))),
            pl.BlockSpec((tm, tk), lambda s: (bi(s), nk + bk(s))),
            pl.BlockSpec((3, 3, tk), lambda s: (0, 0, bk(s))),
            pl.BlockSpec((1, tk), lambda s: (0, bk(s))),
            pl.BlockSpec((None, tk, D_MODEL), lambda s: (layer, mk(s), 0)),
            row_spec,
            mod_spec,
            vec_spec,
        ] + extra_specs,
        out_specs=out_specs,
        out_shape=out_shape,
        scratch_shapes=[
            pltpu.VMEM((tm, D_MODEL), F32),
            pltpu.VMEM((tm, tk), BF16),
            pltpu.VMEM((tm, tk), BF16),
        ],
        compiler_params=_cparams(("arbitrary",)),
        name="ffn_down",
    )(gv2, gv2, gv2, gv2, cw, cb.reshape(1, -1), w_bf, x2, mod3, nw, *extra_in)


class _Tiles(NamedTuple):
    proj: int
    proj_f32: int
    norm: int
    out: int
    down: int
    scan: int
    cols: int


def _stream(x3, h2, mod, l, params, s0, consts, tiles, row_of, grid_mode, full, emit_next):
    bsz, length, _ = x3.shape
    m = bsz * length
    x2 = x3.reshape(m, D_MODEL)
    mod3 = mod[l]
    nw = params["norm_w"][l]
    tn = tiles.cols
    w_in = params["w_in"]
    if h2 is None:
        h2 = _prenorm_call(x2, mod3, nw[0:1], tiles.norm, row_of(tiles.norm), 0, "mix_prenorm")
    f2 = _proj_call(h2, w_in, l, HG_WIDTH, 2 * HG_WIDTH, tiles.proj_f32, tn, False, F32, "in_proj_f")
    if full:
        q2 = _proj_call(h2, w_in, l, 0, HG_WIDTH, tiles.proj, tn, True, BF16, "in_proj_q")
        r2 = _proj_call(h2, w_in, l, 3 * HG_WIDTH, IN_COLS - 3 * HG_WIDTH, tiles.proj, tn, False, BF16,
                        "in_proj_vgab")
    else:
        r2 = _proj_call(h2, w_in, l, 3 * HG_WIDTH, HG_WIDTH, tiles.proj, tn, False, BF16, "in_proj_v")
        q2 = r2
    of3, ob3, sfin = _scan_call(q2.reshape(bsz, length, -1), f2.reshape(bsz, length, -1),
                                r2.reshape(bsz, length, -1), params["lb_logits"], s0, consts, l,
                                tiles.scan)
    if not full:
        return None, None, sfin
    cv3 = _cconv_call(r2.reshape(bsz, length, -1), params["conv_w"][l], params["conv_b"][l],
                      params["conv_ln_w"][l], params["conv_ln_b"][l])
    x2, h2 = _outproj_call(of3.reshape(m, -1), ob3.reshape(m, -1), r2, cv3.reshape(m, -1), x2, mod3,
                           params["hg_norm_w"][l].reshape(1, -1), nw[1:2], nw[2:3], params["w_out"], l,
                           tiles.out, row_of(tiles.out))
    gv2 = _proj_call(h2, params["ffn_up"], l, 0, 2 * D_FF, tiles.proj, tn, False, BF16, "ffn_up")
    next_norm = (mod[l + 1], params["norm_w"][l + 1][0:1]) if emit_next else None
    out = _ffn_down_call(gv2, params["ffn_conv_w"][l], params["ffn_conv_b"][l], params["ffn_down"], l,
                         x2, mod3, nw[3:4], tiles.down, length, row_of(tiles.down), grid_mode, next_norm)
    x2, h_next = out if emit_next else (out, None)
    return x2.reshape(bsz, length, D_MODEL), h_next, sfin


def kernel(x, c, ctx, c_ctx, w_mod, b_mod, norm_w, w_in, lb_logits, hg_norm_w, conv_w, conv_b, conv_ln_w,
           conv_ln_b, w_out, ffn_up, ffn_conv_w, ffn_conv_b, ffn_down):
    bsz, seq, _ = x.shape
    ctx_len = ctx.shape[1]
    params = dict(norm_w=norm_w, w_in=w_in, lb_logits=lb_logits, hg_norm_w=hg_norm_w,
                  conv_w=conv_w, conv_b=conv_b, conv_ln_w=conv_ln_w, conv_ln_b=conv_ln_b,
                  w_out=w_out.astype(BF16), ffn_up=ffn_up, ffn_conv_w=ffn_conv_w,
                  ffn_conv_b=ffn_conv_b, ffn_down=ffn_down.astype(BF16))
    consts = jnp.asarray(_scan_constants(), jnp.int32)

    cond = jnp.concatenate([c, c_ctx[None, :], jnp.zeros((MOD_ROWS - bsz - 1, D_MODEL), c.dtype)], axis=0)
    mod = _mod_call(_silu(cond).astype(BF16), w_mod, b_mod)
    mod = mod[:, :bsz + 1].reshape(DEPTH, bsz + 1, N_MOD, D_MODEL)
    mod = jnp.pad(mod, ((0, 0), (0, 0), (0, MOD_ROWS - N_MOD), (0, 0)))

    lat_row = lambda tm: (lambda i: i // (seq // tm))
    ctx_row = lambda tm: (lambda i: bsz)
    zero_state = jnp.zeros((bsz, HEADS, 2, HEAD_DIM, HEAD_DIM), F32)
    ctx_rows = bsz * ctx_len
    ctx_tiles = _Tiles(proj=ctx_rows, proj_f32=ctx_rows, norm=ctx_len, out=ctx_len, down=ctx_len,
                       scan=ctx_len, cols=1024)
    lat_tiles = _Tiles(proj=2048, proj_f32=1024, norm=512, out=512, down=512, scan=1024, cols=1024)
    xc, hc, h = ctx, None, None
    for l in range(DEPTH):
        last = l == DEPTH - 1
        xc, hc, s_ctx = _stream(xc, hc, mod, l, params, zero_state, consts, ctx_tiles, ctx_row,
                                grid_mode=False, full=not last, emit_next=not last)
        x, h, _ = _stream(x, h, mod, l, params, s_ctx, consts, lat_tiles, lat_row,
                          grid_mode=True, full=True, emit_next=not last)
    return x
```

```python
import functools
from typing import NamedTuple

import numpy as np
import jax
import jax.numpy as jnp
from jax import lax
from jax.experimental import pallas as pl
from jax.experimental.pallas import tpu as pltpu

F32 = jnp.float32
BF16 = jnp.bfloat16

D_MODEL = 2048
DEPTH = 2
GRID_W = 64
HEADS = 8
HEAD_DIM = 128
HG_WIDTH = HEADS * HEAD_DIM
CONV_CH = D_MODEL - HG_WIDTH
CONV_WIDTH = 31
CONV_PAD = 16
D_FF = 5632
N_MOD = 6
MOD_ROWS = 8
EPS = 1e-6
LN_EPS = 1e-5
LOG2_E = float(np.log2(np.e))
IN_COLS = 3 * HG_WIDTH + 2 * HG_WIDTH + 2 * CONV_CH

CHUNK = 64
N_LEVELS = 6

VMEM_LIMIT = 56 * 1024 * 1024


def _cparams(sem):
    return pltpu.CompilerParams(dimension_semantics=sem, vmem_limit_bytes=VMEM_LIMIT)


def _silu(x):
    return x * jax.nn.sigmoid(x)


def _rms_rows(x):
    return x * lax.rsqrt(jnp.mean(x * x, axis=-1, keepdims=True) + EPS)


def _mod_kernel(s_ref, w_ref, b_ref, o_ref):
    w = w_ref[0].astype(BF16)
    o_ref[0] = jnp.dot(s_ref[...], w, preferred_element_type=F32) + b_ref[0]


def _mod_call(s_rows, w_mod, b_mod):
    tn = 1024
    n = w_mod.shape[-1]
    return pl.pallas_call(
        _mod_kernel,
        grid=(DEPTH, n // tn),
        in_specs=[
            pl.BlockSpec((MOD_ROWS, D_MODEL), lambda l, j: (0, 0)),
            pl.BlockSpec((1, D_MODEL, tn), lambda l, j: (l, 0, j)),
            pl.BlockSpec((1, 1, tn), lambda l, j: (l, 0, j)),
        ],
        out_specs=pl.BlockSpec((1, MOD_ROWS, tn), lambda l, j: (l, 0, j)),
        out_shape=jax.ShapeDtypeStruct((DEPTH, MOD_ROWS, n), F32),
        compiler_params=_cparams(("arbitrary", "arbitrary")),
        name="mod_matmul",
    )(s_rows, w_mod, b_mod.reshape(DEPTH, 1, n))


def _prenorm(x, mod_ref, nw_ref, shift_row):
    y = _rms_rows(x) * nw_ref[...]
    h = y * (1.0 + mod_ref[0, shift_row + 1:shift_row + 2, :]) + mod_ref[0, shift_row:shift_row + 1, :]
    return h.astype(BF16)


def _prenorm_kernel(x_ref, mod_ref, nw_ref, h_ref, *, shift_row):
    h_ref[...] = _prenorm(x_ref[...], mod_ref, nw_ref, shift_row)


def _prenorm_call(x2, mod3, nw, tm, mod_row, shift_row, name):
    m = x2.shape[0]
    return pl.pallas_call(
        functools.partial(_prenorm_kernel, shift_row=shift_row),
        grid=(m // tm,),
        in_specs=[
            pl.BlockSpec((tm, D_MODEL), lambda i: (i, 0)),
            pl.BlockSpec((1, MOD_ROWS, D_MODEL), lambda i: (mod_row(i), 0, 0)),
            pl.BlockSpec((1, D_MODEL), lambda i: (0, 0)),
        ],
        out_specs=pl.BlockSpec((tm, D_MODEL), lambda i: (i, 0)),
        out_shape=jax.ShapeDtypeStruct((m, D_MODEL), BF16),
        compiler_params=_cparams(("arbitrary",)),
        name=name,
    )(x2, mod3, nw)


def _proj_kernel(h_ref, w_ref, o_ref, wb_ref, *, act):
    @pl.when(pl.program_id(1) == 0)
    def _():
        wb_ref[...] = w_ref[...].astype(BF16)

    acc = jnp.dot(h_ref[...], wb_ref[...], preferred_element_type=F32)
    if act:
        acc = _silu(acc)
    o_ref[...] = acc.astype(o_ref.dtype)


def _proj_call(h2, w, layer, col0, n_cols, tm, tn, act, out_dtype, name):
    m = h2.shape[0]
    c0 = col0 // tn
    return pl.pallas_call(
        functools.partial(_proj_kernel, act=act),
        grid=(n_cols // tn, m // tm),
        in_specs=[
            pl.BlockSpec((tm, D_MODEL), lambda j, i: (i, 0)),
            pl.BlockSpec((None, D_MODEL, tn), lambda j, i: (layer, 0, c0 + j)),
        ],
        out_specs=pl.BlockSpec((tm, tn), lambda j, i: (i, j)),
        out_shape=jax.ShapeDtypeStruct((m, n_cols), out_dtype),
        scratch_shapes=[pltpu.VMEM((D_MODEL, tn), BF16)],
        compiler_params=_cparams(("arbitrary", "arbitrary")),
        name=name,
    )(h2, w)


def _scan_constants():
    c = CHUNK
    t = np.arange(c)[:, None]
    r = np.arange(c)[None, :]
    level = np.full((2, c, c), -1, np.int32)
    for lvl in range(N_LEVELS):
        half = c >> (lvl + 1)
        same = (t // (2 * half)) == (r // (2 * half))
        t_late = (t % (2 * half)) >= half
        r_late = (r % (2 * half)) >= half
        level[0][same & t_late & ~r_late] = lvl
        level[1][same & ~t_late & r_late] = lvl
    level[:, np.arange(c), np.arange(c)] = N_LEVELS
    return level


_NT = (((1,), (1,)), ((), ()))
_TN = (((0,), (0,)), ((), ()))


def _neg_abs(x):
    bits = lax.bitcast_convert_type(x, jnp.int32) | jnp.int32(-2 ** 31)
    return lax.bitcast_convert_type(bits, F32)


def _scan_kernel(lbl_ref, level_ref, qf_ref, qb_ref, vf_ref, vb_ref, ff_ref, fb_ref,
                 s0_ref, of_ref, ob_ref, sfin_ref, st_ref, *, layer, n_chunks):
    c = CHUNK
    step = pl.program_id(2)

    @pl.when(step == 0)
    def _():
        st_ref[...] = s0_ref[0, 0]

    rows = [lbl_ref[:, j, :] for j in range(DEPTH)]
    mx = functools.reduce(jnp.maximum, rows)
    es = [jnp.exp(rw - mx) for rw in rows]
    tot = functools.reduce(lambda a, b: a + b, es)
    lb = jnp.zeros_like(mx)
    for j in range(1, layer + 1):
        lb = lb + es[j] / tot

    level = level_ref[...]
    sub = lax.broadcasted_iota(jnp.int32, (c // 8, 8, HEAD_DIM), 1)

    def ref_rows(b, half, d):
        pick = half - 1 if d == 0 else half
        if half >= 8:
            parts = []
            for p0 in range(0, c, 2 * half):
                parts.append(jnp.broadcast_to(b[p0 + pick:p0 + pick + 1, :], (2 * half, HEAD_DIM)))
            return parts[0] if len(parts) == 1 else jnp.concatenate(parts, axis=0)
        b3 = b.reshape(c // 8, 8, HEAD_DIM)
        out = None
        for g0 in range(0, 8, 2 * half):
            cand = jnp.broadcast_to(b3[:, g0 + pick:g0 + pick + 1, :], b3.shape)
            out = cand if out is None else jnp.where(sub >= g0, cand, out)
        return out.reshape(c, HEAD_DIM)

    refs = ((qf_ref, vf_ref, ff_ref, of_ref), (qb_ref, vb_ref, fb_ref, ob_ref))

    def gates(d, row0):
        raw = refs[d][2][0, row0:row0 + c, :]
        lbd = lb[d:d + 1, :]
        f = lbd + (1.0 - lbd) * jax.nn.sigmoid(raw)
        x = (jnp.log(f) * LOG2_E).reshape(c // 8, 8, HEAD_DIM)
        for s in (1, 2, 4):
            if d == 0:
                x = x + jnp.where(sub >= s, pltpu.roll(x, s, axis=1), 0.0)
            else:
                x = x + jnp.where(sub < 8 - s, pltpu.roll(x, 8 - s, axis=1), 0.0)
        groups = [x[g] for g in range(c // 8)]
        edge = 7 if d == 0 else 0
        totals = [grp[edge:edge + 1, :] for grp in groups]
        order = range(1, c // 8) if d == 0 else range(c // 8 - 2, -1, -1)
        carry = None
        for g in order:
            before = totals[g - 1] if d == 0 else totals[g + 1]
            carry = before if carry is None else carry + before
            groups[g] = groups[g] + carry
        return f, jnp.concatenate(groups, axis=0)

    def intra(d, row0, f, b):
        q_bf = refs[d][0][0, row0:row0 + c, :]
        q = q_bf.astype(F32)
        key = 1.0 - f
        k_bf = key.astype(BF16)
        own = jnp.sum(q * key, axis=-1, keepdims=True)
        pair_key = pltpu.roll(key, 1 if d == 0 else c - 1, axis=0)
        pair = jnp.sum(q * f * pair_key, axis=-1, keepdims=True)
        att = jnp.where(level[d] == N_LEVELS, own, 0.0)
        att = jnp.where(level[d] == N_LEVELS - 1, pair, att)
        for lvl in range(N_LEVELS - 1):
            half = c >> (lvl + 1)
            el = jnp.exp2(_neg_abs(b - ref_rows(b, half, d))).astype(BF16)
            att = jnp.where(level[d] == lvl,
                            lax.dot_general(q_bf * el, k_bf * el, _NT, preferred_element_type=F32), att)
        last = c - 1 if d == 0 else 0
        blast = b[last:last + 1, :]
        q_in = q_bf * jnp.exp2(b).astype(BF16)
        k_out = k_bf * jnp.exp2(blast - b).astype(BF16)
        return att.astype(BF16), q_in, k_out, jnp.exp2(blast)

    def readout(d, row0, att, q_in, k_out, decay, st):
        v = refs[d][1][0, row0:row0 + c, :]
        o = lax.dot_general(q_in, st.astype(BF16), _NT, preferred_element_type=F32)
        refs[d][3][0, row0:row0 + c, :] = o + jnp.dot(att, v, preferred_element_type=F32)
        return st * decay + lax.dot_general(v, k_out, _TN, preferred_element_type=F32)

    work = []
    for ci in range(n_chunks):
        work.append((0, ci * c))
        work.append((1, (n_chunks - 1 - ci) * c))
    st = [st_ref[0], st_ref[1]]
    stage1 = {}
    stage2 = {}
    for n in range(len(work) + 2):
        if n >= 2:
            d, row0 = work[n - 2]
            st[d] = readout(d, row0, *stage2.pop(n - 2), st[d])
        if 1 <= n <= len(work):
            d, row0 = work[n - 1]
            stage2[n - 1] = intra(d, row0, *stage1.pop(n - 1))
        if n < len(work):
            stage1[n] = gates(*work[n])
    st_ref[0] = st[0]
    st_ref[1] = st[1]

    @pl.when(step == pl.num_programs(2) - 1)
    def _():
        sfin_ref[0, 0] = st_ref[...]


def _scan_call(q3, f3, r3, lb_logits, s0, level, layer, tl):
    bsz, length, _ = q3.shape
    nl = length // tl
    fwd = lambda b, h, i: (b, i, h)
    bwd = lambda b, h, i: (b, nl - 1 - i, h)
    bwd_f = lambda b, h, i: (b, nl - 1 - i, HEADS + h)
    blk = (1, tl, HEAD_DIM)
    return pl.pallas_call(
        functools.partial(_scan_kernel, layer=layer, n_chunks=tl // CHUNK),
        grid=(bsz, HEADS, nl),
        in_specs=[
            pl.BlockSpec((2, DEPTH, HEAD_DIM), lambda b, h, i: (0, 0, h)),
            pl.BlockSpec(level.shape, lambda b, h, i: (0, 0, 0)),
            pl.BlockSpec(blk, fwd),
            pl.BlockSpec(blk, bwd),
            pl.BlockSpec(blk, fwd),
            pl.BlockSpec(blk, bwd),
            pl.BlockSpec(blk, fwd),
            pl.BlockSpec(blk, bwd_f),
            pl.BlockSpec((1, 1, 2, HEAD_DIM, HEAD_DIM), lambda b, h, i: (b, h, 0, 0, 0)),
        ],
        out_specs=[
            pl.BlockSpec(blk, fwd),
            pl.BlockSpec(blk, bwd),
            pl.BlockSpec((1, 1, 2, HEAD_DIM, HEAD_DIM), lambda b, h, i: (b, h, 0, 0, 0)),
        ],
        out_shape=[
            jax.ShapeDtypeStruct((bsz, length, HG_WIDTH), F32),
            jax.ShapeDtypeStruct((bsz, length, HG_WIDTH), F32),
            jax.ShapeDtypeStruct((bsz, HEADS, 2, HEAD_DIM, HEAD_DIM), F32),
        ],
        scratch_shapes=[pltpu.VMEM((2, HEAD_DIM, HEAD_DIM), F32)],
        compiler_params=_cparams(("arbitrary", "arbitrary", "arbitrary")),
        name="hgrn2_scan",
    )(lb_logits, level, q3, q3, r3, r3, f3, f3, s0)


def _cconv_kernel(a_ref, b_ref, cw_ref, cb_ref, lw_ref, lb_ref, o_ref, u_ref, *, length, rows):
    zeros = jnp.zeros((CONV_PAD, HEAD_DIM), F32)
    u_ref[0:CONV_PAD, :] = zeros
    u_ref[CONV_PAD + length:2 * CONV_PAD + length, :] = zeros
    n_tiles = length // rows

    def fill(ti, carry):
        r0 = pl.multiple_of(ti * rows, rows)
        a = a_ref[0, pl.ds(r0, rows), :].astype(F32)
        b = b_ref[0, pl.ds(r0, rows), :].astype(F32)
        u_ref[pl.ds(CONV_PAD + r0, rows), :] = a * jax.nn.sigmoid(b)
        return carry

    lax.fori_loop(0, n_tiles, fill, 0)

    half = CONV_WIDTH // 2

    def conv(ti, carry):
        r0 = pl.multiple_of(ti * rows, rows)
        lead = CONV_PAD - half
        span = -(-(lead + CONV_WIDTH - 1) // 8) * 8 - 8
        acc = jnp.zeros((rows, HEAD_DIM), F32)
        for phase in range(8):
            window = u_ref[pl.ds(r0 + phase, rows + span), :]
            for off in range(phase, lead + CONV_WIDTH, 8):
                if off >= lead:
                    j = off - lead
                    acc = acc + cw_ref[j:j + 1, :] * window[off - phase:off - phase + rows]
        acc = acc + cb_ref[...]
        mu = jnp.mean(acc, axis=-1, keepdims=True)
        dlt = acc - mu
        var = jnp.mean(dlt * dlt, axis=-1, keepdims=True)
        y = dlt * lax.rsqrt(var + LN_EPS) * lw_ref[...] + lb_ref[...]
        o_ref[0, pl.ds(r0, rows), :] = _silu(y).astype(BF16)
        return carry

    lax.fori_loop(0, n_tiles, conv, 0, unroll=4)


def _cconv_call(r3, cw, cb, lw, lb):
    bsz, length, _ = r3.shape
    groups = CONV_CH // HEAD_DIM
    a_col = 2 * HG_WIDTH // HEAD_DIM
    b_col = a_col + groups
    vec = pl.BlockSpec((1, HEAD_DIM), lambda b, g: (0, g))
    return pl.pallas_call(
        functools.partial(_cconv_kernel, length=length, rows=128),
        grid=(bsz, groups),
        in_specs=[
            pl.BlockSpec((1, length, HEAD_DIM), lambda b, g: (b, 0, a_col + g)),
            pl.BlockSpec((1, length, HEAD_DIM), lambda b, g: (b, 0, b_col + g)),
            pl.BlockSpec((CONV_WIDTH, HEAD_DIM), lambda b, g: (0, g)),
            vec, vec, vec,
        ],
        out_specs=pl.BlockSpec((1, length, HEAD_DIM), lambda b, g: (b, 0, g)),
        out_shape=jax.ShapeDtypeStruct((bsz, length, CONV_CH), BF16),
        scratch_shapes=[pltpu.VMEM((length + 2 * CONV_PAD, HEAD_DIM), F32)],
        compiler_params=_cparams(("arbitrary", "arbitrary")),
        name="conformer_conv",
    )(r3, r3, cw, cb.reshape(1, -1), lw.reshape(1, -1), lb.reshape(1, -1))


def _outproj_kernel(of_ref, ob_ref, g_ref, cv_ref, x_ref, mod_ref, hgw_ref, nw_ref, nw2_ref, w_ref, o_ref,
                    h_ref):
    o = of_ref[...] + ob_ref[...]
    heads = []
    for h in range(HEADS):
        oh = o[:, h * HEAD_DIM:(h + 1) * HEAD_DIM]
        heads.append(_rms_rows(oh) * hgw_ref[...])
    r = jnp.concatenate(heads, axis=-1) * _silu(g_ref[...].astype(F32))
    y = jnp.dot(r.astype(BF16), w_ref[0:HG_WIDTH, :], preferred_element_type=F32)
    y = y + jnp.dot(cv_ref[...], w_ref[HG_WIDTH:D_MODEL, :], preferred_element_type=F32)
    x_new = x_ref[...] + mod_ref[0, 2:3, :] * (_rms_rows(y) * nw_ref[...])
    o_ref[...] = x_new
    h_ref[...] = _prenorm(x_new, mod_ref, nw2_ref, 3)


def _outproj_call(of2, ob2, r2, cv2, x2, mod3, hgw, nw, nw2, w_bf, layer, tm, mod_row):
    m = x2.shape[0]
    return pl.pallas_call(
        _outproj_kernel,
        grid=(m // tm,),
        in_specs=[
            pl.BlockSpec((tm, HG_WIDTH), lambda i: (i, 0)),
            pl.BlockSpec((tm, HG_WIDTH), lambda i: (i, 0)),
            pl.BlockSpec((tm, HG_WIDTH), lambda i: (i, 1)),
            pl.BlockSpec((tm, CONV_CH), lambda i: (i, 0)),
            pl.BlockSpec((tm, D_MODEL), lambda i: (i, 0)),
            pl.BlockSpec((1, MOD_ROWS, D_MODEL), lambda i: (mod_row(i), 0, 0)),
            pl.BlockSpec((1, HEAD_DIM), lambda i: (0, 0)),
            pl.BlockSpec((1, D_MODEL), lambda i: (0, 0)),
            pl.BlockSpec((1, D_MODEL), lambda i: (0, 0)),
            pl.BlockSpec((None, D_MODEL, D_MODEL), lambda i: (layer, 0, 0), pipeline_mode=pl.Buffered(1)),
        ],
        out_specs=[
            pl.BlockSpec((tm, D_MODEL), lambda i: (i, 0)),
            pl.BlockSpec((tm, D_MODEL), lambda i: (i, 0)),
        ],
        out_shape=[
            jax.ShapeDtypeStruct((m, D_MODEL), F32),
            jax.ShapeDtypeStruct((m, D_MODEL), BF16),
        ],
        compiler_params=_cparams(("arbitrary",)),
        name="out_proj",
    )(of2, ob2, r2, cv2, x2, mod3, hgw, nw, nw2, w_bf)


ROW_BLOCK = 16
MXU_COLS = 256
MXU_ROWS = 256
FFN_CHUNK = 512
LANE_BLOCK = 256


def _gate_chunk(gp_ref, gm_ref, gn_ref, val_ref, cw_ref, cb_ref, act_out, first, last, mm_tile,
                mm_tiles, *, tm, grid_mode):
    tk = FFN_CHUNK
    rows = ROW_BLOCK
    lanes = LANE_BLOCK
    n_blocks = tm // rows
    total = n_blocks * (tk // lanes)
    row_id = lax.broadcasted_iota(jnp.int32, (rows, lanes), 0)
    edge = GRID_W if grid_mode else tm
    zero = jnp.zeros((), BF16)

    def gate_rows(r0, ls):
        if r0 < 0:
            return jnp.where(first, zero, gp_ref[GRID_W + r0:GRID_W + r0 + rows, ls])
        if r0 >= tm:
            return jnp.where(last, zero, gn_ref[r0 - tm:r0 - tm + rows, ls])
        return gm_ref[r0:r0 + rows, ls]

    done = 0
    for l0 in range(0, tk, lanes):
        ls = slice(l0, l0 + lanes)
        taps = (0.5 * cw_ref[:, :, ls]).astype(BF16)
        taps = [[jnp.broadcast_to(taps[dh, dw:dw + 1], (rows, lanes)) for dw in range(3)] for dh in range(3)]
        bias = 0.5 * cb_ref[:, ls]

        def columns(r0):
            ce = gate_rows(r0, ls)
            if not grid_mode:
                return [(taps[1][dw] * ce).astype(F32) for dw in range(3)]
            up = gate_rows(r0 - GRID_W, ls)
            dn = gate_rows(r0 + GRID_W, ls)
            return [(taps[0][dw] * up + taps[1][dw] * ce + taps[2][dw] * dn).astype(F32) for dw in range(3)]

        cur = columns(0)
        prev_left_row = None
        for blk in range(n_blocks):
            r0 = blk * rows
            for n in range(done * mm_tiles // total, (done + 1) * mm_tiles // total):
                mm_tile(n)
            done += 1
            nxt = columns(r0 + rows) if blk + 1 < n_blocks else None
            left, mid, right = cur
            from_prev = pltpu.roll(left, 1, axis=0)
            if r0 % edge == 0:
                from_prev = jnp.where(row_id == 0, 0.0, from_prev)
            else:
                from_prev = jnp.where(row_id == 0, prev_left_row, from_prev)
            from_next = pltpu.roll(right, rows - 1, axis=0)
            if (r0 + rows) % edge == 0:
                from_next = jnp.where(row_id == rows - 1, 0.0, from_next)
            else:
                from_next = jnp.where(row_id == rows - 1, nxt[2][0:1, :], from_next)
            half = mid + from_prev + from_next + bias
            gelu = half * (1.0 + lax.erf(half * np.float32(np.sqrt(2.0))))
            act_out[r0:r0 + rows, ls] = gelu.astype(BF16) * val_ref[r0:r0 + rows, ls]
            prev_left_row = left[rows - 1:rows, :]
            cur = nxt


def _ffn_down_kernel(gp_ref, gm_ref, gn_ref, val_ref, cw_ref, cb_ref, w_ref, x_ref, mod_ref, nw_ref, *rest,
                     tm, nk, tiles_per_seq, grid_mode, next_norm):
    if next_norm:
        modn_ref, nwn_ref, o_ref, h_ref, acc_ref, act0_ref, act1_ref = rest
    else:
        o_ref, acc_ref, act0_ref, act1_ref = rest
    s = pl.program_id(0)
    i = jnp.minimum(s, pl.num_programs(0) - 2) // nk
    km = jnp.maximum(s - 1, 0) % nk
    first = (i % tiles_per_seq) == 0
    last = (i % tiles_per_seq) == tiles_per_seq - 1

    @pl.when(s == 0)
    def _():
        act1_ref[...] = jnp.zeros_like(act1_ref)

    @pl.when(km == 0)
    def _():
        acc_ref[...] = jnp.zeros_like(acc_ref)

    def step(act_in, act_out):
        row_parts = tm // MXU_ROWS

        def mm_tile(n):
            rows = slice((n % row_parts) * MXU_ROWS, (n % row_parts + 1) * MXU_ROWS)
            cols = slice((n // row_parts) * MXU_COLS, (n // row_parts + 1) * MXU_COLS)
            acc_ref[rows, cols] += jnp.dot(act_in[rows, :], w_ref[:, cols], preferred_element_type=F32)

        _gate_chunk(gp_ref, gm_ref, gn_ref, val_ref, cw_ref, cb_ref, act_out, first, last, mm_tile,
                    row_parts * D_MODEL // MXU_COLS, tm=tm, grid_mode=grid_mode)

    @pl.when(s % 2 == 0)
    def _():
        step(act1_ref, act0_ref)

    @pl.when(s % 2 == 1)
    def _():
        step(act0_ref, act1_ref)

    @pl.when((km == nk - 1) & (s > 0))
    def _():
        x_new = x_ref[...] + mod_ref[0, 5:6, :] * (_rms_rows(acc_ref[...]) * nw_ref[...])
        o_ref[...] = x_new
        if next_norm:
            h_ref[...] = _prenorm(x_new, modn_ref, nwn_ref, 0)


def _ffn_down_call(gv2, cw, cb, w_bf, layer, x2, mod3, nw, tm, seq_len, mod_row, grid_mode, next_norm):
    m = x2.shape[0]
    tk = FFN_CHUNK
    nk = D_FF // tk
    hb = tm // GRID_W
    n_hblk = m // GRID_W
    tiles_per_seq = seq_len // tm
    n_steps = (m // tm) * nk + 1
    bi = lambda s: jnp.minimum(s, n_steps - 2) // nk
    bk = lambda s: jnp.minimum(s, n_steps - 2) % nk
    mi = lambda s: jnp.maximum(s - 1, 0) // nk
    mk = lambda s: jnp.maximum(s - 1, 0) % nk
    row_spec = pl.BlockSpec((tm, D_MODEL), lambda s: (mi(s), 0))
    mod_spec = pl.BlockSpec((1, MOD_ROWS, D_MODEL), lambda s: (mod_row(mi(s)), 0, 0))
    vec_spec = pl.BlockSpec((1, D_MODEL), lambda s: (0, 0))
    extra_in, extra_specs, out_specs = (), [], row_spec
    out_shape = jax.ShapeDtypeStruct((m, D_MODEL), F32)
    if next_norm:
        extra_in, extra_specs = tuple(next_norm), [mod_spec, vec_spec]
        out_specs = [row_spec, row_spec]
        out_shape = [out_shape, jax.ShapeDtypeStruct((m, D_MODEL), BF16)]
    return pl.pallas_call(
        functools.partial(_ffn_down_kernel, tm=tm, nk=nk, tiles_per_seq=tiles_per_seq, grid_mode=grid_mode,
                          next_norm=bool(next_norm)),
        grid=(n_steps,),
        in_specs=[
            pl.BlockSpec((GRID_W, tk), lambda s: (jnp.maximum(bi(s) * hb - 1, 0), bk(s))),
            pl.BlockSpec((tm, tk), lambda s: (bi(s), bk(s))),
            pl.BlockSpec((GRID_W, tk), lambda s: (jnp.minimum((bi(s) + 1) * hb, n_hblk - 1), bk(s))),
            pl.BlockSpec((tm, tk), lambda s: (bi(s), nk + bk(s))),
            pl.BlockSpec((3, 3, tk), lambda s: (0, 0, bk(s))),
            pl.BlockSpec((1, tk), lambda s: (0, bk(s))),
            pl.BlockSpec((None, tk, D_MODEL), lambda s: (layer, mk(s), 0)),
            row_spec,
            mod_spec,
            vec_spec,
        ] + extra_specs,
        out_specs=out_specs,
        out_shape=out_shape,
        scratch_shapes=[
            pltpu.VMEM((tm, D_MODEL), F32),
            pltpu.VMEM((tm, tk), BF16),
            pltpu.VMEM((tm, tk), BF16),
        ],
        compiler_params=_cparams(("arbitrary",)),
        name="ffn_down",
    )(gv2, gv2, gv2, gv2, cw, cb.reshape(1, -1), w_bf, x2, mod3, nw, *extra_in)


class _Tiles(NamedTuple):
    proj: int
    proj_f32: int
    norm: int
    out: int
    down: int
    scan: int
    cols: int


def _stream(x3, h2, mod, l, params, s0, consts, tiles, row_of, grid_mode, full, emit_next):
    bsz, length, _ = x3.shape
    m = bsz * length
    x2 = x3.reshape(m, D_MODEL)
    mod3 = mod[l]
    nw = params["norm_w"][l]
    tn = tiles.cols
    w_in = params["w_in"]
    if h2 is None:
        h2 = _prenorm_call(x2, mod3, nw[0:1], tiles.norm, row_of(tiles.norm), 0, "mix_prenorm")
    f2 = _proj_call(h2, w_in, l, HG_WIDTH, 2 * HG_WIDTH, tiles.proj_f32, tn, False, F32, "in_proj_f")
    if full:
        q2 = _proj_call(h2, w_in, l, 0, HG_WIDTH, tiles.proj, tn, True, BF16, "in_proj_q")
        r2 = _proj_call(h2, w_in, l, 3 * HG_WIDTH, IN_COLS - 3 * HG_WIDTH, tiles.proj, tn, False, BF16,
                        "in_proj_vgab")
    else:
        r2 = _proj_call(h2, w_in, l, 3 * HG_WIDTH, HG_WIDTH, tiles.proj, tn, False, BF16, "in_proj_v")
        q2 = r2
    of3, ob3, sfin = _scan_call(q2.reshape(bsz, length, -1), f2.reshape(bsz, length, -1),
                                r2.reshape(bsz, length, -1), params["lb_logits"], s0, consts, l,
                                tiles.scan)
    if not full:
        return None, None, sfin
    cv3 = _cconv_call(r2.reshape(bsz, length, -1), params["conv_w"][l], params["conv_b"][l],
                      params["conv_ln_w"][l], params["conv_ln_b"][l])
    x2, h2 = _outproj_call(of3.reshape(m, -1), ob3.reshape(m, -1), r2, cv3.reshape(m, -1), x2, mod3,
                           params["hg_norm_w"][l].reshape(1, -1), nw[1:2], nw[2:3], params["w_out"], l,
                           tiles.out, row_of(tiles.out))
    gv2 = _proj_call(h2, params["ffn_up"], l, 0, 2 * D_FF, tiles.proj, tn, False, BF16, "ffn_up")
    next_norm = (mod[l + 1], params["norm_w"][l + 1][0:1]) if emit_next else None
    out = _ffn_down_call(gv2, params["ffn_conv_w"][l], params["ffn_conv_b"][l], params["ffn_down"], l,
                         x2, mod3, nw[3:4], tiles.down, length, row_of(tiles.down), grid_mode, next_norm)
    x2, h_next = out if emit_next else (out, None)
    return x2.reshape(bsz, length, D_MODEL), h_next, sfin


def kernel(x, c, ctx, c_ctx, w_mod, b_mod, norm_w, w_in, lb_logits, hg_norm_w, conv_w, conv_b, conv_ln_w,
           conv_ln_b, w_out, ffn_up, ffn_conv_w, ffn_conv_b, ffn_down):
    bsz, seq, _ = x.shape
    ctx_len = ctx.shape[1]
    params = dict(norm_w=norm_w, w_in=w_in, lb_logits=lb_logits, hg_norm_w=hg_norm_w,
                  conv_w=conv_w, conv_b=conv_b, conv_ln_w=conv_ln_w, conv_ln_b=conv_ln_b,
                  w_out=w_out.astype(BF16), ffn_up=ffn_up, ffn_conv_w=ffn_conv_w,
                  ffn_conv_b=ffn_conv_b, ffn_down=ffn_down.astype(BF16))
    consts = jnp.asarray(_scan_constants(), jnp.int32)

    cond = jnp.concatenate([c, c_ctx[None, :], jnp.zeros((MOD_ROWS - bsz - 1, D_MODEL), c.dtype)], axis=0)
    mod = _mod_call(_silu(cond).astype(BF16), w_mod, b_mod)
    mod = mod[:, :bsz + 1].reshape(DEPTH, bsz + 1, N_MOD, D_MODEL)
    mod = jnp.pad(mod, ((0, 0), (0, 0), (0, MOD_ROWS - N_MOD), (0, 0)))

    lat_row = lambda tm: (lambda i: i // (seq // tm))
    ctx_row = lambda tm: (lambda i: bsz)
    zero_state = jnp.zeros((bsz, HEADS, 2, HEAD_DIM, HEAD_DIM), F32)
    ctx_rows = bsz * ctx_len
    ctx_tiles = _Tiles(proj=ctx_rows, proj_f32=ctx_rows, norm=ctx_len, out=ctx_len, down=ctx_len,
                       scan=ctx_len, cols=1024)
    lat_tiles = _Tiles(proj=2048, proj_f32=1024, norm=512, out=512, down=512, scan=1024, cols=1024)
    xc, hc, h = ctx, None, None
    for l in range(DEPTH):
        last = l == DEPTH - 1
        xc, hc, s_ctx = _stream(xc, hc, mod, l, params, zero_state, consts, ctx_tiles, ctx_row,
                                grid_mode=False, full=not last, emit_next=not last)
        x, h, _ = _stream(x, h, mod, l, params, s_ctx, consts, lat_tiles, lat_row,
                          grid_mode=True, full=True, emit_next=not last)
    return x
```

```python
import functools
from typing import NamedTuple

import numpy as np
import jax
import jax.numpy as jnp
from jax import lax
from jax.experimental import pallas as pl
from jax.experimental.pallas import tpu as pltpu

F32 = jnp.float32
BF16 = jnp.bfloat16

D_MODEL = 2048
DEPTH = 2
GRID_W = 64
HEADS = 8
HEAD_DIM = 128
HG_WIDTH = HEADS * HEAD_DIM
CONV_CH = D_MODEL - HG_WIDTH
CONV_WIDTH = 31
CONV_PAD = 16
D_FF = 5632
N_MOD = 6
MOD_ROWS = 8
EPS = 1e-6
LN_EPS = 1e-5
LOG2_E = float(np.log2(np.e))
IN_COLS = 3 * HG_WIDTH + 2 * HG_WIDTH + 2 * CONV_CH

CHUNK = 64
N_LEVELS = 6

VMEM_LIMIT = 56 * 1024 * 1024


def _cparams(sem):
    return pltpu.CompilerParams(dimension_semantics=sem, vmem_limit_bytes=VMEM_LIMIT)


def _silu(x):
    return x * jax.nn.sigmoid(x)


def _rms_rows(x):
    return x * lax.rsqrt(jnp.mean(x * x, axis=-1, keepdims=True) + EPS)


def _mod_kernel(s_ref, w_ref, b_ref, o_ref):
    w = w_ref[0].astype(BF16)
    o_ref[0] = jnp.dot(s_ref[...], w, preferred_element_type=F32) + b_ref[0]


def _mod_call(s_rows, w_mod, b_mod):
    tn = 1024
    n = w_mod.shape[-1]
    return pl.pallas_call(
        _mod_kernel,
        grid=(DEPTH, n // tn),
        in_specs=[
            pl.BlockSpec((MOD_ROWS, D_MODEL), lambda l, j: (0, 0)),
            pl.BlockSpec((1, D_MODEL, tn), lambda l, j: (l, 0, j)),
            pl.BlockSpec((1, 1, tn), lambda l, j: (l, 0, j)),
        ],
        out_specs=pl.BlockSpec((1, MOD_ROWS, tn), lambda l, j: (l, 0, j)),
        out_shape=jax.ShapeDtypeStruct((DEPTH, MOD_ROWS, n), F32),
        compiler_params=_cparams(("arbitrary", "arbitrary")),
        name="mod_matmul",
    )(s_rows, w_mod, b_mod.reshape(DEPTH, 1, n))


def _prenorm(x, mod_ref, nw_ref, shift_row):
    y = _rms_rows(x) * nw_ref[...]
    h = y * (1.0 + mod_ref[0, shift_row + 1:shift_row + 2, :]) + mod_ref[0, shift_row:shift_row + 1, :]
    return h.astype(BF16)


def _prenorm_kernel(x_ref, mod_ref, nw_ref, h_ref, *, shift_row):
    h_ref[...] = _prenorm(x_ref[...], mod_ref, nw_ref, shift_row)


def _prenorm_call(x2, mod3, nw, tm, mod_row, shift_row, name):
    m = x2.shape[0]
    return pl.pallas_call(
        functools.partial(_prenorm_kernel, shift_row=shift_row),
        grid=(m // tm,),
        in_specs=[
            pl.BlockSpec((tm, D_MODEL), lambda i: (i, 0)),
            pl.BlockSpec((1, MOD_ROWS, D_MODEL), lambda i: (mod_row(i), 0, 0)),
            pl.BlockSpec((1, D_MODEL), lambda i: (0, 0)),
        ],
        out_specs=pl.BlockSpec((tm, D_MODEL), lambda i: (i, 0)),
        out_shape=jax.ShapeDtypeStruct((m, D_MODEL), BF16),
        compiler_params=_cparams(("arbitrary",)),
        name=name,
    )(x2, mod3, nw)


def _proj_kernel(h_ref, w_ref, o_ref, wb_ref, *, act):
    @pl.when(pl.program_id(1) == 0)
    def _():
        wb_ref[...] = w_ref[...].astype(BF16)

    acc = jnp.dot(h_ref[...], wb_ref[...], preferred_element_type=F32)
    if act:
        acc = _silu(acc)
    o_ref[...] = acc.astype(o_ref.dtype)


def _proj_call(h2, w, layer, col0, n_cols, tm, tn, act, out_dtype, name):
    m = h2.shape[0]
    c0 = col0 // tn
    return pl.pallas_call(
        functools.partial(_proj_kernel, act=act),
        grid=(n_cols // tn, m // tm),
        in_specs=[
            pl.BlockSpec((tm, D_MODEL), lambda j, i: (i, 0)),
            pl.BlockSpec((None, D_MODEL, tn), lambda j, i: (layer, 0, c0 + j)),
        ],
        out_specs=pl.BlockSpec((tm, tn), lambda j, i: (i, j)),
        out_shape=jax.ShapeDtypeStruct((m, n_cols), out_dtype),
        scratch_shapes=[pltpu.VMEM((D_MODEL, tn), BF16)],
        compiler_params=_cparams(("arbitrary", "arbitrary")),
        name=name,
    )(h2, w)


def _scan_constants():
    c = CHUNK
    t = np.arange(c)[:, None]
    r = np.arange(c)[None, :]
    level = np.full((2, c, c), -1, np.int32)
    for lvl in range(N_LEVELS):
        half = c >> (lvl + 1)
        same = (t // (2 * half)) == (r // (2 * half))
        t_late = (t % (2 * half)) >= half
        r_late = (r % (2 * half)) >= half
        level[0][same & t_late & ~r_late] = lvl
        level[1][same & ~t_late & r_late] = lvl
    level[:, np.arange(c), np.arange(c)] = N_LEVELS
    return level


_NT = (((1,), (1,)), ((), ()))
_TN = (((0,), (0,)), ((), ()))


def _neg_abs(x):
    bits = lax.bitcast_convert_type(x, jnp.int32) | jnp.int32(-2 ** 31)
    return lax.bitcast_convert_type(bits, F32)


def _scan_kernel(lbl_ref, level_ref, qf_ref, qb_ref, vf_ref, vb_ref, ff_ref, fb_ref,
                 s0_ref, of_ref, ob_ref, sfin_ref, st_ref, *, layer, n_chunks):
    c = CHUNK
    step = pl.program_id(2)

    @pl.when(step == 0)
    def _():
        st_ref[...] = s0_ref[0, 0]

    rows = [lbl_ref[:, j, :] for j in range(DEPTH)]
    mx = functools.reduce(jnp.maximum, rows)
    es = [jnp.exp(rw - mx) for rw in rows]
    tot = functools.reduce(lambda a, b: a + b, es)
    lb = jnp.zeros_like(mx)
    for j in range(1, layer + 1):
        lb = lb + es[j] / tot

    level = level_ref[...]
    sub = lax.broadcasted_iota(jnp.int32, (c // 8, 8, HEAD_DIM), 1)

    def ref_rows(b, half, d):
        pick = half - 1 if d == 0 else half
        if half >= 8:
            parts = []
            for p0 in range(0, c, 2 * half):
                parts.append(jnp.broadcast_to(b[p0 + pick:p0 + pick + 1, :], (2 * half, HEAD_DIM)))
            return parts[0] if len(parts) == 1 else jnp.concatenate(parts, axis=0)
        b3 = b.reshape(c // 8, 8, HEAD_DIM)
        out = None
        for g0 in range(0, 8, 2 * half):
            cand = jnp.broadcast_to(b3[:, g0 + pick:g0 + pick + 1, :], b3.shape)
            out = cand if out is None else jnp.where(sub >= g0, cand, out)
        return out.reshape(c, HEAD_DIM)

    refs = ((qf_ref, vf_ref, ff_ref, of_ref), (qb_ref, vb_ref, fb_ref, ob_ref))

    def gates(d, row0):
        raw = refs[d][2][0, row0:row0 + c, :]
        lbd = lb[d:d + 1, :]
        f = lbd + (1.0 - lbd) * jax.nn.sigmoid(raw)
        x = (jnp.log(f) * LOG2_E).reshape(c // 8, 8, HEAD_DIM)
        for s in (1, 2, 4):
            if d == 0:
                x = x + jnp.where(sub >= s, pltpu.roll(x, s, axis=1), 0.0)
            else:
                x = x + jnp.where(sub < 8 - s, pltpu.roll(x, 8 - s, axis=1), 0.0)
        groups = [x[g] for g in range(c // 8)]
        edge = 7 if d == 0 else 0
        totals = [grp[edge:edge + 1, :] for grp in groups]
        order = range(1, c // 8) if d == 0 else range(c // 8 - 2, -1, -1)
        carry = None
        for g in order:
            before = totals[g - 1] if d == 0 else totals[g + 1]
            carry = before if carry is None else carry + before
            groups[g] = groups[g] + carry
        return f, jnp.concatenate(groups, axis=0)

    def intra(d, row0, f, b):
        q_bf = refs[d][0][0, row0:row0 + c, :]
        q = q_bf.astype(F32)
        key = 1.0 - f
        k_bf = key.astype(BF16)
        own = jnp.sum(q * key, axis=-1, keepdims=True)
        pair_key = pltpu.roll(key, 1 if d == 0 else c - 1, axis=0)
        pair = jnp.sum(q * f * pair_key, axis=-1, keepdims=True)
        att = jnp.where(level[d] == N_LEVELS, own, 0.0)
        att = jnp.where(level[d] == N_LEVELS - 1, pair, att)
        for lvl in range(N_LEVELS - 1):
            half = c >> (lvl + 1)
            el = jnp.exp2(_neg_abs(b - ref_rows(b, half, d))).astype(BF16)
            att = jnp.where(level[d] == lvl,
                            lax.dot_general(q_bf * el, k_bf * el, _NT, preferred_element_type=F32), att)
        last = c - 1 if d == 0 else 0
        blast = b[last:last + 1, :]
        q_in = q_bf * jnp.exp2(b).astype(BF16)
        k_out = k_bf * jnp.exp2(blast - b).astype(BF16)
        decay = jnp.transpose(jnp.broadcast_to(jnp.exp2(blast), (8, HEAD_DIM)))[:, 0:1]
        return jnp.concatenate([q_in, att.astype(BF16)], axis=1), k_out, decay

    def readout(d, row0, q_att, k_out, decay, st):
        v = refs[d][1][0, row0:row0 + c, :]
        rhs = jnp.concatenate([st.astype(BF16), v], axis=0)
        refs[d][3][0, row0:row0 + c, :] = jnp.dot(q_att, rhs, preferred_element_type=F32)
        return st * decay + lax.dot_general(k_out, v, _TN, preferred_element_type=F32)

    work = []
    for ci in range(n_chunks):
        work.append((0, ci * c))
        work.append((1, (n_chunks - 1 - ci) * c))
    st = [st_ref[0], st_ref[1]]
    stage1 = {}
    stage2 = {}
    for n in range(len(work) + 2):
        if n >= 2:
            d, row0 = work[n - 2]
            st[d] = readout(d, row0, *stage2.pop(n - 2), st[d])
        if 1 <= n <= len(work):
            d, row0 = work[n - 1]
            stage2[n - 1] = intra(d, row0, *stage1.pop(n - 1))
        if n < len(work):
            stage1[n] = gates(*work[n])
    st_ref[0] = st[0]
    st_ref[1] = st[1]

    @pl.when(step == pl.num_programs(2) - 1)
    def _():
        sfin_ref[0, 0] = st_ref[...]


def _scan_call(q3, f3, r3, lb_logits, s0, level, layer, tl):
    bsz, length, _ = q3.shape
    nl = length // tl
    fwd = lambda b, h, i: (b, i, h)
    bwd = lambda b, h, i: (b, nl - 1 - i, h)
    bwd_f = lambda b, h, i: (b, nl - 1 - i, HEADS + h)
    blk = (1, tl, HEAD_DIM)
    return pl.pallas_call(
        functools.partial(_scan_kernel, layer=layer, n_chunks=tl // CHUNK),
        grid=(bsz, HEADS, nl),
        in_specs=[
            pl.BlockSpec((2, DEPTH, HEAD_DIM), lambda b, h, i: (0, 0, h)),
            pl.BlockSpec(level.shape, lambda b, h, i: (0, 0, 0)),
            pl.BlockSpec(blk, fwd),
            pl.BlockSpec(blk, bwd),
            pl.BlockSpec(blk, fwd),
            pl.BlockSpec(blk, bwd),
            pl.BlockSpec(blk, fwd),
            pl.BlockSpec(blk, bwd_f),
            pl.BlockSpec((1, 1, 2, HEAD_DIM, HEAD_DIM), lambda b, h, i: (b, h, 0, 0, 0)),
        ],
        out_specs=[
            pl.BlockSpec(blk, fwd),
            pl.BlockSpec(blk, bwd),
            pl.BlockSpec((1, 1, 2, HEAD_DIM, HEAD_DIM), lambda b, h, i: (b, h, 0, 0, 0)),
        ],
        out_shape=[
            jax.ShapeDtypeStruct((bsz, length, HG_WIDTH), F32),
            jax.ShapeDtypeStruct((bsz, length, HG_WIDTH), F32),
            jax.ShapeDtypeStruct((bsz, HEADS, 2, HEAD_DIM, HEAD_DIM), F32),
        ],
        scratch_shapes=[pltpu.VMEM((2, HEAD_DIM, HEAD_DIM), F32)],
        compiler_params=_cparams(("arbitrary", "arbitrary", "arbitrary")),
        name="hgrn2_scan",
    )(lb_logits, level, q3, q3, r3, r3, f3, f3, s0)


def _cconv_kernel(a_ref, b_ref, cw_ref, cb_ref, lw_ref, lb_ref, o_ref, u_ref, *, length, rows):
    zeros = jnp.zeros((CONV_PAD, HEAD_DIM), F32)
    u_ref[0:CONV_PAD, :] = zeros
    u_ref[CONV_PAD + length:2 * CONV_PAD + length, :] = zeros
    n_tiles = length // rows

    def fill(ti, carry):
        r0 = pl.multiple_of(ti * rows, rows)
        a = a_ref[0, pl.ds(r0, rows), :].astype(F32)
        b = b_ref[0, pl.ds(r0, rows), :].astype(F32)
        u_ref[pl.ds(CONV_PAD + r0, rows), :] = a * jax.nn.sigmoid(b)
        return carry

    lax.fori_loop(0, n_tiles, fill, 0)

    half = CONV_WIDTH // 2

    def conv(ti, carry):
        r0 = pl.multiple_of(ti * rows, rows)
        lead = CONV_PAD - half
        span = -(-(lead + CONV_WIDTH - 1) // 8) * 8 - 8
        acc = jnp.zeros((rows, HEAD_DIM), F32)
        for phase in range(8):
            window = u_ref[pl.ds(r0 + phase, rows + span), :]
            for off in range(phase, lead + CONV_WIDTH, 8):
                if off >= lead:
                    j = off - lead
                    acc = acc + cw_ref[j:j + 1, :] * window[off - phase:off - phase + rows]
        acc = acc + cb_ref[...]
        mu = jnp.mean(acc, axis=-1, keepdims=True)
        dlt = acc - mu
        var = jnp.mean(dlt * dlt, axis=-1, keepdims=True)
        y = dlt * lax.rsqrt(var + LN_EPS) * lw_ref[...] + lb_ref[...]
        o_ref[0, pl.ds(r0, rows), :] = _silu(y).astype(BF16)
        return carry

    lax.fori_loop(0, n_tiles, conv, 0, unroll=4)


def _cconv_call(r3, cw, cb, lw, lb):
    bsz, length, _ = r3.shape
    groups = CONV_CH // HEAD_DIM
    a_col = 2 * HG_WIDTH // HEAD_DIM
    b_col = a_col + groups
    vec = pl.BlockSpec((1, HEAD_DIM), lambda b, g: (0, g))
    return pl.pallas_call(
        functools.partial(_cconv_kernel, length=length, rows=128),
        grid=(bsz, groups),
        in_specs=[
            pl.BlockSpec((1, length, HEAD_DIM), lambda b, g: (b, 0, a_col + g)),
            pl.BlockSpec((1, length, HEAD_DIM), lambda b, g: (b, 0, b_col + g)),
            pl.BlockSpec((CONV_WIDTH, HEAD_DIM), lambda b, g: (0, g)),
            vec, vec, vec,
        ],
        out_specs=pl.BlockSpec((1, length, HEAD_DIM), lambda b, g: (b, 0, g)),
        out_shape=jax.ShapeDtypeStruct((bsz, length, CONV_CH), BF16),
        scratch_shapes=[pltpu.VMEM((length + 2 * CONV_PAD, HEAD_DIM), F32)],
        compiler_params=_cparams(("arbitrary", "arbitrary")),
        name="conformer_conv",
    )(r3, r3, cw, cb.reshape(1, -1), lw.reshape(1, -1), lb.reshape(1, -1))


def _outproj_kernel(of_ref, ob_ref, g_ref, cv_ref, x_ref, mod_ref, hgw_ref, nw_ref, nw2_ref, w_ref, o_ref,
                    h_ref):
    o = of_ref[...] + ob_ref[...]
    heads = []
    for h in range(HEADS):
        oh = o[:, h * HEAD_DIM:(h + 1) * HEAD_DIM]
        heads.append(_rms_rows(oh) * hgw_ref[...])
    r = jnp.concatenate(heads, axis=-1) * _silu(g_ref[...].astype(F32))
    y = jnp.dot(r.astype(BF16), w_ref[0:HG_WIDTH, :], preferred_element_type=F32)
    y = y + jnp.dot(cv_ref[...], w_ref[HG_WIDTH:D_MODEL, :], preferred_element_type=F32)
    x_new = x_ref[...] + mod_ref[0, 2:3, :] * (_rms_rows(y) * nw_ref[...])
    o_ref[...] = x_new
    h_ref[...] = _prenorm(x_new, mod_ref, nw2_ref, 3)


def _outproj_call(of2, ob2, r2, cv2, x2, mod3, hgw, nw, nw2, w_bf, layer, tm, mod_row):
    m = x2.shape[0]
    return pl.pallas_call(
        _outproj_kernel,
        grid=(m // tm,),
        in_specs=[
            pl.BlockSpec((tm, HG_WIDTH), lambda i: (i, 0)),
            pl.BlockSpec((tm, HG_WIDTH), lambda i: (i, 0)),
            pl.BlockSpec((tm, HG_WIDTH), lambda i: (i, 1)),
            pl.BlockSpec((tm, CONV_CH), lambda i: (i, 0)),
            pl.BlockSpec((tm, D_MODEL), lambda i: (i, 0)),
            pl.BlockSpec((1, MOD_ROWS, D_MODEL), lambda i: (mod_row(i), 0, 0)),
            pl.BlockSpec((1, HEAD_DIM), lambda i: (0, 0)),
            pl.BlockSpec((1, D_MODEL), lambda i: (0, 0)),
            pl.BlockSpec((1, D_MODEL), lambda i: (0, 0)),
            pl.BlockSpec((None, D_MODEL, D_MODEL), lambda i: (layer, 0, 0), pipeline_mode=pl.Buffered(1)),
        ],
        out_specs=[
            pl.BlockSpec((tm, D_MODEL), lambda i: (i, 0)),
            pl.BlockSpec((tm, D_MODEL), lambda i: (i, 0)),
        ],
        out_shape=[
            jax.ShapeDtypeStruct((m, D_MODEL), F32),
            jax.ShapeDtypeStruct((m, D_MODEL), BF16),
        ],
        compiler_params=_cparams(("arbitrary",)),
        name="out_proj",
    )(of2, ob2, r2, cv2, x2, mod3, hgw, nw, nw2, w_bf)


ROW_BLOCK = 16
MXU_COLS = 256
MXU_ROWS = 256
FFN_CHUNK = 512
LANE_BLOCK = 256


def _gate_chunk(gp_ref, gm_ref, gn_ref, val_ref, cw_ref, cb_ref, act_out, first, last, mm_tile,
                mm_tiles, *, tm, grid_mode):
    tk = FFN_CHUNK
    rows = ROW_BLOCK
    lanes = LANE_BLOCK
    n_blocks = tm // rows
    total = n_blocks * (tk // lanes)
    row_id = lax.broadcasted_iota(jnp.int32, (rows, lanes), 0)
    edge = GRID_W if grid_mode else tm
    zero = jnp.zeros((), BF16)

    def gate_rows(r0, ls):
        if r0 < 0:
            return jnp.where(first, zero, gp_ref[GRID_W + r0:GRID_W + r0 + rows, ls])
        if r0 >= tm:
            return jnp.where(last, zero, gn_ref[r0 - tm:r0 - tm + rows, ls])
        return gm_ref[r0:r0 + rows, ls]

    done = 0
    for l0 in range(0, tk, lanes):
        ls = slice(l0, l0 + lanes)
        taps = (0.5 * cw_ref[:, :, ls]).astype(BF16)
        taps = [[jnp.broadcast_to(taps[dh, dw:dw + 1], (rows, lanes)) for dw in range(3)] for dh in range(3)]
        bias = 0.5 * cb_ref[:, ls]

        def columns(r0):
            ce = gate_rows(r0, ls)
            if not grid_mode:
                return [(taps[1][dw] * ce).astype(F32) for dw in range(3)]
            up = gate_rows(r0 - GRID_W, ls)
            dn = gate_rows(r0 + GRID_W, ls)
            return [(taps[0][dw] * up + taps[1][dw] * ce + taps[2][dw] * dn).astype(F32) for dw in range(3)]

        cur = columns(0)
        prev_left_row = None
        for blk in range(n_blocks):
            r0 = blk * rows
            for n in range(done * mm_tiles // total, (done + 1) * mm_tiles // total):
                mm_tile(n)
            done += 1
            nxt = columns(r0 + rows) if blk + 1 < n_blocks else None
            left, mid, right = cur
            from_prev = pltpu.roll(left, 1, axis=0)
            if r0 % edge == 0:
                from_prev = jnp.where(row_id == 0, 0.0, from_prev)
            else:
                from_prev = jnp.where(row_id == 0, prev_left_row, from_prev)
            from_next = pltpu.roll(right, rows - 1, axis=0)
            if (r0 + rows) % edge == 0:
                from_next = jnp.where(row_id == rows - 1, 0.0, from_next)
            else:
                from_next = jnp.where(row_id == rows - 1, nxt[2][0:1, :], from_next)
            half = mid + from_prev + from_next + bias
            gelu = half * (1.0 + lax.erf(half * np.float32(np.sqrt(2.0))))
            act_out[r0:r0 + rows, ls] = gelu.astype(BF16) * val_ref[r0:r0 + rows, ls]
            prev_left_row = left[rows - 1:rows, :]
            cur = nxt


def _ffn_down_kernel(gp_ref, gm_ref, gn_ref, val_ref, cw_ref, cb_ref, w_ref, x_ref, mod_ref, nw_ref, *rest,
                     tm, nk, tiles_per_seq, grid_mode, next_norm):
    if next_norm:
        modn_ref, nwn_ref, o_ref, h_ref, acc_ref, act0_ref, act1_ref = rest
    else:
        o_ref, acc_ref, act0_ref, act1_ref = rest
    s = pl.program_id(0)
    i = jnp.minimum(s, pl.num_programs(0) - 2) // nk
    km = jnp.maximum(s - 1, 0) % nk
    first = (i % tiles_per_seq) == 0
    last = (i % tiles_per_seq) == tiles_per_seq - 1

    @pl.when(s == 0)
    def _():
        act1_ref[...] = jnp.zeros_like(act1_ref)

    @pl.when(km == 0)
    def _():
        acc_ref[...] = jnp.zeros_like(acc_ref)

    def step(act_in, act_out):
        row_parts = tm // MXU_ROWS

        def mm_tile(n):
            rows = slice((n % row_parts) * MXU_ROWS, (n % row_parts + 1) * MXU_ROWS)
            cols = slice((n // row_parts) * MXU_COLS, (n // row_parts + 1) * MXU_COLS)
            acc_ref[rows, cols] += jnp.dot(act_in[rows, :], w_ref[:, cols], preferred_element_type=F32)

        _gate_chunk(gp_ref, gm_ref, gn_ref, val_ref, cw_ref, cb_ref, act_out, first, last, mm_tile,
                    row_parts * D_MODEL // MXU_COLS, tm=tm, grid_mode=grid_mode)

    @pl.when(s % 2 == 0)
    def _():
        step(act1_ref, act0_ref)

    @pl.when(s % 2 == 1)
    def _():
        step(act0_ref, act1_ref)

    @pl.when((km == nk - 1) & (s > 0))
    def _():
        x_new = x_ref[...] + mod_ref[0, 5:6, :] * (_rms_rows(acc_ref[...]) * nw_ref[...])
        o_ref[...] = x_new
        if next_norm:
            h_ref[...] = _prenorm(x_new, modn_ref, nwn_ref, 0)


def _ffn_down_call(gv2, cw, cb, w_bf, layer, x2, mod3, nw, tm, seq_len, mod_row, grid_mode, next_norm):
    m = x2.shape[0]
    tk = FFN_CHUNK
    nk = D_FF // tk
    hb = tm // GRID_W
    n_hblk = m // GRID_W
    tiles_per_seq = seq_len // tm
    n_steps = (m // tm) * nk + 1
    bi = lambda s: jnp.minimum(s, n_steps - 2) // nk
    bk = lambda s: jnp.minimum(s, n_steps - 2) % nk
    mi = lambda s: jnp.maximum(s - 1, 0) // nk
    mk = lambda s: jnp.maximum(s - 1, 0) % nk
    row_spec = pl.BlockSpec((tm, D_MODEL), lambda s: (mi(s), 0))
    mod_spec = pl.BlockSpec((1, MOD_ROWS, D_MODEL), lambda s: (mod_row(mi(s)), 0, 0))
    vec_spec = pl.BlockSpec((1, D_MODEL), lambda s: (0, 0))
    extra_in, extra_specs, out_specs = (), [], row_spec
    out_shape = jax.ShapeDtypeStruct((m, D_MODEL), F32)
    if next_norm:
        extra_in, extra_specs = tuple(next_norm), [mod_spec, vec_spec]
        out_specs = [row_spec, row_spec]
        out_shape = [out_shape, jax.ShapeDtypeStruct((m, D_MODEL), BF16)]
    return pl.pallas_call(
        functools.partial(_ffn_down_kernel, tm=tm, nk=nk, tiles_per_seq=tiles_per_seq, grid_mode=grid_mode,
                          next_norm=bool(next_norm)),
        grid=(n_steps,),
        in_specs=[
            pl.BlockSpec((GRID_W, tk), lambda s: (jnp.maximum(bi(s) * hb - 1, 0), bk(s))),
            pl.BlockSpec((tm, tk), lambda s: (bi(s), bk(s))),
            pl.BlockSpec((GRID_W, tk), lambda s: (jnp.minimum((bi(s) + 1) * hb, n_hblk - 1), bk(s))),
            pl.BlockSpec((tm, tk), lambda s: (bi(s), nk + bk(s))),
            pl.BlockSpec((3, 3, tk), lambda s: (0, 0, bk(s))),
            pl.BlockSpec((1, tk), lambda s: (0, bk(s))),
            pl.BlockSpec((None, tk, D_MODEL), lambda s: (layer, mk(s), 0)),
            row_spec,
            mod_spec,
            vec_spec,
        ] + extra_specs,
        out_specs=out_specs,
        out_shape=out_shape,
        scratch_shapes=[
            pltpu.VMEM((tm, D_MODEL), F32),
            pltpu.VMEM((tm, tk), BF16),
            pltpu.VMEM((tm, tk), BF16),
        ],
        compiler_params=_cparams(("arbitrary",)),
        name="ffn_down",
    )(gv2, gv2, gv2, gv2, cw, cb.reshape(1, -1), w_bf, x2, mod3, nw, *extra_in)


class _Tiles(NamedTuple):
    proj: int
    proj_f32: int
    norm: int
    out: int
    down: int
    scan: int
    cols: int


def _stream(x3, h2, mod, l, params, s0, consts, tiles, row_of, grid_mode, full, emit_next):
    bsz, length, _ = x3.shape
    m = bsz * length
    x2 = x3.reshape(m, D_MODEL)
    mod3 = mod[l]
    nw = params["norm_w"][l]
    tn = tiles.cols
    w_in = params["w_in"]
    if h2 is None:
        h2 = _prenorm_call(x2, mod3, nw[0:1], tiles.norm, row_of(tiles.norm), 0, "mix_prenorm")
    f2 = _proj_call(h2, w_in, l, HG_WIDTH, 2 * HG_WIDTH, tiles.proj_f32, tn, False, F32, "in_proj_f")
    if full:
        q2 = _proj_call(h2, w_in, l, 0, HG_WIDTH, tiles.proj, tn, True, BF16, "in_proj_q")
        r2 = _proj_call(h2, w_in, l, 3 * HG_WIDTH, IN_COLS - 3 * HG_WIDTH, tiles.proj, tn, False, BF16,
                        "in_proj_vgab")
    else:
        r2 = _proj_call(h2, w_in, l, 3 * HG_WIDTH, HG_WIDTH, tiles.proj, tn, False, BF16, "in_proj_v")
        q2 = r2
    of3, ob3, sfin = _scan_call(q2.reshape(bsz, length, -1), f2.reshape(bsz, length, -1),
                                r2.reshape(bsz, length, -1), params["lb_logits"], s0, consts, l,
                                tiles.scan)
    if not full:
        return None, None, sfin
    cv3 = _cconv_call(r2.reshape(bsz, length, -1), params["conv_w"][l], params["conv_b"][l],
                      params["conv_ln_w"][l], params["conv_ln_b"][l])
    x2, h2 = _outproj_call(of3.reshape(m, -1), ob3.reshape(m, -1), r2, cv3.reshape(m, -1), x2, mod3,
                           params["hg_norm_w"][l].reshape(1, -1), nw[1:2], nw[2:3], params["w_out"], l,
                           tiles.out, row_of(tiles.out))
    gv2 = _proj_call(h2, params["ffn_up"], l, 0, 2 * D_FF, tiles.proj, tn, False, BF16, "ffn_up")
    next_norm = (mod[l + 1], params["norm_w"][l + 1][0:1]) if emit_next else None
    out = _ffn_down_call(gv2, params["ffn_conv_w"][l], params["ffn_conv_b"][l], params["ffn_down"], l,
                         x2, mod3, nw[3:4], tiles.down, length, row_of(tiles.down), grid_mode, next_norm)
    x2, h_next = out if emit_next else (out, None)
    return x2.reshape(bsz, length, D_MODEL), h_next, sfin


def kernel(x, c, ctx, c_ctx, w_mod, b_mod, norm_w, w_in, lb_logits, hg_norm_w, conv_w, conv_b, conv_ln_w,
           conv_ln_b, w_out, ffn_up, ffn_conv_w, ffn_conv_b, ffn_down):
    bsz, seq, _ = x.shape
    ctx_len = ctx.shape[1]
    params = dict(norm_w=norm_w, w_in=w_in, lb_logits=lb_logits, hg_norm_w=hg_norm_w,
                  conv_w=conv_w, conv_b=conv_b, conv_ln_w=conv_ln_w, conv_ln_b=conv_ln_b,
                  w_out=w_out.astype(BF16), ffn_up=ffn_up, ffn_conv_w=ffn_conv_w,
                  ffn_conv_b=ffn_conv_b, ffn_down=ffn_down.astype(BF16))
    consts = jnp.asarray(_scan_constants(), jnp.int32)

    cond = jnp.concatenate([c, c_ctx[None, :], jnp.zeros((MOD_ROWS - bsz - 1, D_MODEL), c.dtype)], axis=0)
    mod = _mod_call(_silu(cond).astype(BF16), w_mod, b_mod)
    mod = mod[:, :bsz + 1].reshape(DEPTH, bsz + 1, N_MOD, D_MODEL)
    mod = jnp.pad(mod, ((0, 0), (0, 0), (0, MOD_ROWS - N_MOD), (0, 0)))

    lat_row = lambda tm: (lambda i: i // (seq // tm))
    ctx_row = lambda tm: (lambda i: bsz)
    zero_state = jnp.zeros((bsz, HEADS, 2, HEAD_DIM, HEAD_DIM), F32)
    ctx_rows = bsz * ctx_len
    ctx_tiles = _Tiles(proj=ctx_rows, proj_f32=ctx_rows, norm=ctx_len, out=ctx_len, down=ctx_len,
                       scan=ctx_len, cols=1024)
    lat_tiles = _Tiles(proj=2048, proj_f32=1024, norm=512, out=512, down=512, scan=1024, cols=1024)
    xc, hc, h = ctx, None, None
    for l in range(DEPTH):
        last = l == DEPTH - 1
        xc, hc, s_ctx = _stream(xc, hc, mod, l, params, zero_state, consts, ctx_tiles, ctx_row,
                                grid_mode=False, full=not last, emit_next=not last)
        x, h, _ = _stream(x, h, mod, l, params, s_ctx, consts, lat_tiles, lat_row,
                          grid_mode=True, full=True, emit_next=not last)
    return x
```

```python
import functools
from typing import NamedTuple

import numpy as np
import jax
import jax.numpy as jnp
from jax import lax
from jax.experimental import pallas as pl
from jax.experimental.pallas import tpu as pltpu

F32 = jnp.float32
BF16 = jnp.bfloat16

D_MODEL = 2048
DEPTH = 2
GRID_W = 64
HEADS = 8
HEAD_DIM = 128
HG_WIDTH = HEADS * HEAD_DIM
CONV_CH = D_MODEL - HG_WIDTH
CONV_WIDTH = 31
CONV_PAD = 16
D_FF = 5632
N_MOD = 6
MOD_ROWS = 8
EPS = 1e-6
LN_EPS = 1e-5
LOG2_E = float(np.log2(np.e))
IN_COLS = 3 * HG_WIDTH + 2 * HG_WIDTH + 2 * CONV_CH

CHUNK = 64
N_LEVELS = 6

VMEM_LIMIT = 56 * 1024 * 1024


def _cparams(sem):
    return pltpu.CompilerParams(dimension_semantics=sem, vmem_limit_bytes=VMEM_LIMIT)


def _silu(x):
    return x * jax.nn.sigmoid(x)


def _rms_rows(x):
    return x * lax.rsqrt(jnp.mean(x * x, axis=-1, keepdims=True) + EPS)


def _mod_kernel(s_ref, w_ref, b_ref, o_ref):
    w = w_ref[0].astype(BF16)
    o_ref[0] = jnp.dot(s_ref[...], w, preferred_element_type=F32) + b_ref[0]


def _mod_call(s_rows, w_mod, b_mod):
    tn = 1024
    n = w_mod.shape[-1]
    return pl.pallas_call(
        _mod_kernel,
        grid=(DEPTH, n // tn),
        in_specs=[
            pl.BlockSpec((MOD_ROWS, D_MODEL), lambda l, j: (0, 0)),
            pl.BlockSpec((1, D_MODEL, tn), lambda l, j: (l, 0, j)),
            pl.BlockSpec((1, 1, tn), lambda l, j: (l, 0, j)),
        ],
        out_specs=pl.BlockSpec((1, MOD_ROWS, tn), lambda l, j: (l, 0, j)),
        out_shape=jax.ShapeDtypeStruct((DEPTH, MOD_ROWS, n), F32),
        compiler_params=_cparams(("arbitrary", "arbitrary")),
        name="mod_matmul",
    )(s_rows, w_mod, b_mod.reshape(DEPTH, 1, n))


def _prenorm(x, mod_ref, nw_ref, shift_row):
    y = _rms_rows(x) * nw_ref[...]
    h = y * (1.0 + mod_ref[0, shift_row + 1:shift_row + 2, :]) + mod_ref[0, shift_row:shift_row + 1, :]
    return h.astype(BF16)


def _prenorm_kernel(x_ref, mod_ref, nw_ref, h_ref, *, shift_row):
    h_ref[...] = _prenorm(x_ref[...], mod_ref, nw_ref, shift_row)


def _prenorm_call(x2, mod3, nw, tm, mod_row, shift_row, name):
    m = x2.shape[0]
    return pl.pallas_call(
        functools.partial(_prenorm_kernel, shift_row=shift_row),
        grid=(m // tm,),
        in_specs=[
            pl.BlockSpec((tm, D_MODEL), lambda i: (i, 0)),
            pl.BlockSpec((1, MOD_ROWS, D_MODEL), lambda i: (mod_row(i), 0, 0)),
            pl.BlockSpec((1, D_MODEL), lambda i: (0, 0)),
        ],
        out_specs=pl.BlockSpec((tm, D_MODEL), lambda i: (i, 0)),
        out_shape=jax.ShapeDtypeStruct((m, D_MODEL), BF16),
        compiler_params=_cparams(("arbitrary",)),
        name=name,
    )(x2, mod3, nw)


def _proj_kernel(h_ref, w_ref, o_ref, wb_ref, *, act):
    @pl.when(pl.program_id(1) == 0)
    def _():
        wb_ref[...] = w_ref[...].astype(BF16)

    acc = jnp.dot(h_ref[...], wb_ref[...], preferred_element_type=F32)
    if act:
        acc = _silu(acc)
    o_ref[...] = acc.astype(o_ref.dtype)


def _proj_call(h2, w, layer, col0, n_cols, tm, tn, act, out_dtype, name):
    m = h2.shape[0]
    c0 = col0 // tn
    return pl.pallas_call(
        functools.partial(_proj_kernel, act=act),
        grid=(n_cols // tn, m // tm),
        in_specs=[
            pl.BlockSpec((tm, D_MODEL), lambda j, i: (i, 0)),
            pl.BlockSpec((None, D_MODEL, tn), lambda j, i: (layer, 0, c0 + j)),
        ],
        out_specs=pl.BlockSpec((tm, tn), lambda j, i: (i, j)),
        out_shape=jax.ShapeDtypeStruct((m, n_cols), out_dtype),
        scratch_shapes=[pltpu.VMEM((D_MODEL, tn), BF16)],
        compiler_params=_cparams(("arbitrary", "arbitrary")),
        name=name,
    )(h2, w)


def _scan_constants():
    c = CHUNK
    t = np.arange(c)[:, None]
    r = np.arange(c)[None, :]
    level = np.full((2, c, c), -1, np.int32)
    for lvl in range(N_LEVELS):
        half = c >> (lvl + 1)
        same = (t // (2 * half)) == (r // (2 * half))
        t_late = (t % (2 * half)) >= half
        r_late = (r % (2 * half)) >= half
        level[0][same & t_late & ~r_late] = lvl
        level[1][same & ~t_late & r_late] = lvl
    level[:, np.arange(c), np.arange(c)] = N_LEVELS
    return level


_NT = (((1,), (1,)), ((), ()))
_TN = (((0,), (0,)), ((), ()))


def _neg_abs(x):
    bits = lax.bitcast_convert_type(x, jnp.int32) | jnp.int32(-2 ** 31)
    return lax.bitcast_convert_type(bits, F32)


def _scan_kernel(lbl_ref, level_ref, qf_ref, qb_ref, vf_ref, vb_ref, ff_ref, fb_ref,
                 s0_ref, of_ref, ob_ref, sfin_ref, st_ref, *, layer, n_chunks):
    c = CHUNK
    step = pl.program_id(2)

    @pl.when(step == 0)
    def _():
        st_ref[...] = s0_ref[0, 0]

    rows = [lbl_ref[:, j, :] for j in range(DEPTH)]
    mx = functools.reduce(jnp.maximum, rows)
    es = [jnp.exp(rw - mx) for rw in rows]
    tot = functools.reduce(lambda a, b: a + b, es)
    lb = jnp.zeros_like(mx)
    for j in range(1, layer + 1):
        lb = lb + es[j] / tot

    level = level_ref[...]
    sub = lax.broadcasted_iota(jnp.int32, (c // 8, 8, HEAD_DIM), 1)

    def ref_rows(b, half, d):
        pick = half - 1 if d == 0 else half
        if half >= 8:
            parts = []
            for p0 in range(0, c, 2 * half):
                parts.append(jnp.broadcast_to(b[p0 + pick:p0 + pick + 1, :], (2 * half, HEAD_DIM)))
            return parts[0] if len(parts) == 1 else jnp.concatenate(parts, axis=0)
        b3 = b.reshape(c // 8, 8, HEAD_DIM)
        out = None
        for g0 in range(0, 8, 2 * half):
            cand = jnp.broadcast_to(b3[:, g0 + pick:g0 + pick + 1, :], b3.shape)
            out = cand if out is None else jnp.where(sub >= g0, cand, out)
        return out.reshape(c, HEAD_DIM)

    refs = ((qf_ref, vf_ref, ff_ref, of_ref), (qb_ref, vb_ref, fb_ref, ob_ref))

    def gates(d, row0):
        raw = refs[d][2][0, row0:row0 + c, :]
        lbd = lb[d:d + 1, :]
        f = lbd + (1.0 - lbd) * jax.nn.sigmoid(raw)
        x = (jnp.log(f) * LOG2_E).reshape(c // 8, 8, HEAD_DIM)
        for s in (1, 2, 4):
            if d == 0:
                x = x + jnp.where(sub >= s, pltpu.roll(x, s, axis=1), 0.0)
            else:
                x = x + jnp.where(sub < 8 - s, pltpu.roll(x, 8 - s, axis=1), 0.0)
        groups = [x[g] for g in range(c // 8)]
        edge = 7 if d == 0 else 0
        totals = [grp[edge:edge + 1, :] for grp in groups]
        order = range(1, c // 8) if d == 0 else range(c // 8 - 2, -1, -1)
        carry = None
        for g in order:
            before = totals[g - 1] if d == 0 else totals[g + 1]
            carry = before if carry is None else carry + before
            groups[g] = groups[g] + carry
        return f, jnp.concatenate(groups, axis=0)

    def intra(d, row0, f, b):
        q_bf = refs[d][0][0, row0:row0 + c, :]
        q = q_bf.astype(F32)
        key = 1.0 - f
        k_bf = key.astype(BF16)
        own = jnp.sum(q * key, axis=-1, keepdims=True)
        pair_key = pltpu.roll(key, 1 if d == 0 else c - 1, axis=0)
        pair = jnp.sum(q * f * pair_key, axis=-1, keepdims=True)
        att = jnp.where(level[d] == N_LEVELS, own, 0.0)
        att = jnp.where(level[d] == N_LEVELS - 1, pair, att)
        for lvl in range(N_LEVELS - 1):
            half = c >> (lvl + 1)
            el = jnp.exp2(_neg_abs(b - ref_rows(b, half, d))).astype(BF16)
            att = jnp.where(level[d] == lvl,
                            lax.dot_general(q_bf * el, k_bf * el, _NT, preferred_element_type=F32), att)
        last = c - 1 if d == 0 else 0
        blast = b[last:last + 1, :]
        q_in = q_bf * jnp.exp2(b).astype(BF16)
        k_out = k_bf * jnp.exp2(blast - b).astype(BF16)
        decay = jnp.transpose(jnp.broadcast_to(jnp.exp2(blast), (8, HEAD_DIM)))[:, 0:1]
        return jnp.concatenate([q_in, att.astype(BF16)], axis=1), k_out, decay

    def readout(d, row0, q_att, k_out, decay, st):
        v = refs[d][1][0, row0:row0 + c, :]
        rhs = jnp.concatenate([st.astype(BF16), v], axis=0)
        refs[d][3][0, row0:row0 + c, :] = jnp.dot(q_att, rhs, preferred_element_type=F32)
        return st * decay + lax.dot_general(k_out, v, _TN, preferred_element_type=F32)

    work = []
    for ci in range(n_chunks):
        work.append((0, ci * c))
        work.append((1, (n_chunks - 1 - ci) * c))
    st = [st_ref[0], st_ref[1]]
    stage1 = {}
    stage2 = {}
    for n in range(len(work) + 2):
        if n >= 2:
            d, row0 = work[n - 2]
            st[d] = readout(d, row0, *stage2.pop(n - 2), st[d])
        if 1 <= n <= len(work):
            d, row0 = work[n - 1]
            stage2[n - 1] = intra(d, row0, *stage1.pop(n - 1))
        if n < len(work):
            stage1[n] = gates(*work[n])
    st_ref[0] = st[0]
    st_ref[1] = st[1]

    @pl.when(step == pl.num_programs(2) - 1)
    def _():
        sfin_ref[0, 0] = st_ref[...]


def _scan_call(q3, f3, r3, lb_logits, s0, level, layer, tl):
    bsz, length, _ = q3.shape
    nl = length // tl
    fwd = lambda b, h, i: (b, i, h)
    bwd = lambda b, h, i: (b, nl - 1 - i, h)
    bwd_f = lambda b, h, i: (b, nl - 1 - i, HEADS + h)
    blk = (1, tl, HEAD_DIM)
    return pl.pallas_call(
        functools.partial(_scan_kernel, layer=layer, n_chunks=tl // CHUNK),
        grid=(bsz, HEADS, nl),
        in_specs=[
            pl.BlockSpec((2, DEPTH, HEAD_DIM), lambda b, h, i: (0, 0, h)),
            pl.BlockSpec(level.shape, lambda b, h, i: (0, 0, 0)),
            pl.BlockSpec(blk, fwd),
            pl.BlockSpec(blk, bwd),
            pl.BlockSpec(blk, fwd),
            pl.BlockSpec(blk, bwd),
            pl.BlockSpec(blk, fwd),
            pl.BlockSpec(blk, bwd_f),
            pl.BlockSpec((1, 1, 2, HEAD_DIM, HEAD_DIM), lambda b, h, i: (b, h, 0, 0, 0)),
        ],
        out_specs=[
            pl.BlockSpec(blk, fwd),
            pl.BlockSpec(blk, bwd),
            pl.BlockSpec((1, 1, 2, HEAD_DIM, HEAD_DIM), lambda b, h, i: (b, h, 0, 0, 0)),
        ],
        out_shape=[
            jax.ShapeDtypeStruct((bsz, length, HG_WIDTH), F32),
            jax.ShapeDtypeStruct((bsz, length, HG_WIDTH), F32),
            jax.ShapeDtypeStruct((bsz, HEADS, 2, HEAD_DIM, HEAD_DIM), F32),
        ],
        scratch_shapes=[pltpu.VMEM((2, HEAD_DIM, HEAD_DIM), F32)],
        compiler_params=_cparams(("arbitrary", "arbitrary", "arbitrary")),
        name="hgrn2_scan",
    )(lb_logits, level, q3, q3, r3, r3, f3, f3, s0)


def _cconv_kernel(a_ref, b_ref, cw_ref, cb_ref, lw_ref, lb_ref, o_ref, u_ref, *, length, rows):
    zeros = jnp.zeros((CONV_PAD, HEAD_DIM), F32)
    u_ref[0:CONV_PAD, :] = zeros
    u_ref[CONV_PAD + length:2 * CONV_PAD + length, :] = zeros
    n_tiles = length // rows

    def fill(ti, carry):
        r0 = pl.multiple_of(ti * rows, rows)
        a = a_ref[0, pl.ds(r0, rows), :].astype(F32)
        b = b_ref[0, pl.ds(r0, rows), :].astype(F32)
        u_ref[pl.ds(CONV_PAD + r0, rows), :] = a * jax.nn.sigmoid(b)
        return carry

    lax.fori_loop(0, n_tiles, fill, 0)

    half = CONV_WIDTH // 2

    def conv(ti, carry):
        r0 = pl.multiple_of(ti * rows, rows)
        lead = CONV_PAD - half
        span = -(-(lead + CONV_WIDTH - 1) // 8) * 8 - 8
        acc = jnp.zeros((rows, HEAD_DIM), F32)
        for phase in range(8):
            window = u_ref[pl.ds(r0 + phase, rows + span), :]
            for off in range(phase, lead + CONV_WIDTH, 8):
                if off >= lead:
                    j = off - lead
                    acc = acc + cw_ref[j:j + 1, :] * window[off - phase:off - phase + rows]
        acc = acc + cb_ref[...]
        mu = jnp.mean(acc, axis=-1, keepdims=True)
        dlt = acc - mu
        var = jnp.mean(dlt * dlt, axis=-1, keepdims=True)
        y = dlt * lax.rsqrt(var + LN_EPS) * lw_ref[...] + lb_ref[...]
        o_ref[0, pl.ds(r0, rows), :] = _silu(y).astype(BF16)
        return carry

    lax.fori_loop(0, n_tiles, conv, 0, unroll=4)


def _cconv_call(r3, cw, cb, lw, lb):
    bsz, length, _ = r3.shape
    groups = CONV_CH // HEAD_DIM
    a_col = 2 * HG_WIDTH // HEAD_DIM
    b_col = a_col + groups
    vec = pl.BlockSpec((1, HEAD_DIM), lambda b, g: (0, g))
    return pl.pallas_call(
        functools.partial(_cconv_kernel, length=length, rows=128),
        grid=(bsz, groups),
        in_specs=[
            pl.BlockSpec((1, length, HEAD_DIM), lambda b, g: (b, 0, a_col + g)),
            pl.BlockSpec((1, length, HEAD_DIM), lambda b, g: (b, 0, b_col + g)),
            pl.BlockSpec((CONV_WIDTH, HEAD_DIM), lambda b, g: (0, g)),
            vec, vec, vec,
        ],
        out_specs=pl.BlockSpec((1, length, HEAD_DIM), lambda b, g: (b, 0, g)),
        out_shape=jax.ShapeDtypeStruct((bsz, length, CONV_CH), BF16),
        scratch_shapes=[pltpu.VMEM((length + 2 * CONV_PAD, HEAD_DIM), F32)],
        compiler_params=_cparams(("arbitrary", "arbitrary")),
        name="conformer_conv",
    )(r3, r3, cw, cb.reshape(1, -1), lw.reshape(1, -1), lb.reshape(1, -1))


def _outproj_kernel(of_ref, ob_ref, g_ref, cv_ref, x_ref, mod_ref, hgw_ref, nw_ref, nw2_ref, w_ref, o_ref,
                    h_ref):
    o = of_ref[...] + ob_ref[...]
    heads = []
    for h in range(HEADS):
        oh = o[:, h * HEAD_DIM:(h + 1) * HEAD_DIM]
        heads.append(_rms_rows(oh) * hgw_ref[...])
    r = jnp.concatenate(heads, axis=-1) * _silu(g_ref[...].astype(F32))
    y = jnp.dot(r.astype(BF16), w_ref[0:HG_WIDTH, :], preferred_element_type=F32)
    y = y + jnp.dot(cv_ref[...], w_ref[HG_WIDTH:D_MODEL, :], preferred_element_type=F32)
    x_new = x_ref[...] + mod_ref[0, 2:3, :] * (_rms_rows(y) * nw_ref[...])
    o_ref[...] = x_new
    h_ref[...] = _prenorm(x_new, mod_ref, nw2_ref, 3)


def _outproj_call(of2, ob2, r2, cv2, x2, mod3, hgw, nw, nw2, w_bf, layer, tm, mod_row):
    m = x2.shape[0]
    return pl.pallas_call(
        _outproj_kernel,
        grid=(m // tm,),
        in_specs=[
            pl.BlockSpec((tm, HG_WIDTH), lambda i: (i, 0)),
            pl.BlockSpec((tm, HG_WIDTH), lambda i: (i, 0)),
            pl.BlockSpec((tm, HG_WIDTH), lambda i: (i, 1)),
            pl.BlockSpec((tm, CONV_CH), lambda i: (i, 0)),
            pl.BlockSpec((tm, D_MODEL), lambda i: (i, 0)),
            pl.BlockSpec((1, MOD_ROWS, D_MODEL), lambda i: (mod_row(i), 0, 0)),
            pl.BlockSpec((1, HEAD_DIM), lambda i: (0, 0)),
            pl.BlockSpec((1, D_MODEL), lambda i: (0, 0)),
            pl.BlockSpec((1, D_MODEL), lambda i: (0, 0)),
            pl.BlockSpec((None, D_MODEL, D_MODEL), lambda i: (layer, 0, 0), pipeline_mode=pl.Buffered(1)),
        ],
        out_specs=[
            pl.BlockSpec((tm, D_MODEL), lambda i: (i, 0)),
            pl.BlockSpec((tm, D_MODEL), lambda i: (i, 0)),
        ],
        out_shape=[
            jax.ShapeDtypeStruct((m, D_MODEL), F32),
            jax.ShapeDtypeStruct((m, D_MODEL), BF16),
        ],
        compiler_params=_cparams(("arbitrary",)),
        name="out_proj",
    )(of2, ob2, r2, cv2, x2, mod3, hgw, nw, nw2, w_bf)


ROW_BLOCK = 32
MXU_COLS = 256
MXU_ROWS = 256
FFN_CHUNK = 512
LANE_BLOCK = 256


def _gate_chunk(gp_ref, gm_ref, gn_ref, val_ref, cw_ref, cb_ref, act_out, first, last, mm_tile,
                mm_tiles, *, tm, grid_mode):
    tk = FFN_CHUNK
    rows = ROW_BLOCK
    lanes = LANE_BLOCK
    n_blocks = tm // rows
    total = n_blocks * (tk // lanes)
    row_id = lax.broadcasted_iota(jnp.int32, (rows, lanes), 0)
    edge = GRID_W if grid_mode else tm
    zero = jnp.zeros((), BF16)

    def gate_rows(r0, ls):
        if r0 < 0:
            return jnp.where(first, zero, gp_ref[GRID_W + r0:GRID_W + r0 + rows, ls])
        if r0 >= tm:
            return jnp.where(last, zero, gn_ref[r0 - tm:r0 - tm + rows, ls])
        return gm_ref[r0:r0 + rows, ls]

    done = 0
    for l0 in range(0, tk, lanes):
        ls = slice(l0, l0 + lanes)
        taps = (0.5 * cw_ref[:, :, ls]).astype(BF16)
        taps = [[jnp.broadcast_to(taps[dh, dw:dw + 1], (rows, lanes)) for dw in range(3)] for dh in range(3)]
        bias = 0.5 * cb_ref[:, ls]

        def columns(r0):
            ce = gate_rows(r0, ls)
            if not grid_mode:
                return [(taps[1][dw] * ce).astype(F32) for dw in range(3)]
            up = gate_rows(r0 - GRID_W, ls)
            dn = gate_rows(r0 + GRID_W, ls)
            return [(taps[0][dw] * up + taps[1][dw] * ce + taps[2][dw] * dn).astype(F32) for dw in range(3)]

        cur = columns(0)
        prev_left_row = None
        for blk in range(n_blocks):
            r0 = blk * rows
            for n in range(done * mm_tiles // total, (done + 1) * mm_tiles // total):
                mm_tile(n)
            done += 1
            nxt = columns(r0 + rows) if blk + 1 < n_blocks else None
            left, mid, right = cur
            from_prev = pltpu.roll(left, 1, axis=0)
            if r0 % edge == 0:
                from_prev = jnp.where(row_id == 0, 0.0, from_prev)
            else:
                from_prev = jnp.where(row_id == 0, prev_left_row, from_prev)
            from_next = pltpu.roll(right, rows - 1, axis=0)
            if (r0 + rows) % edge == 0:
                from_next = jnp.where(row_id == rows - 1, 0.0, from_next)
            else:
                from_next = jnp.where(row_id == rows - 1, nxt[2][0:1, :], from_next)
            half = mid + from_prev + from_next + bias
            gelu = half * (1.0 + lax.erf(half * np.float32(np.sqrt(2.0))))
            act_out[r0:r0 + rows, ls] = gelu.astype(BF16) * val_ref[r0:r0 + rows, ls]
            prev_left_row = left[rows - 1:rows, :]
            cur = nxt


def _ffn_down_kernel(gp_ref, gm_ref, gn_ref, val_ref, cw_ref, cb_ref, w_ref, x_ref, mod_ref, nw_ref, *rest,
                     tm, nk, tiles_per_seq, grid_mode, next_norm):
    if next_norm:
        modn_ref, nwn_ref, o_ref, h_ref, acc_ref, act0_ref, act1_ref = rest
    else:
        o_ref, acc_ref, act0_ref, act1_ref = rest
    s = pl.program_id(0)
    i = jnp.minimum(s, pl.num_programs(0) - 2) // nk
    km = jnp.maximum(s - 1, 0) % nk
    first = (i % tiles_per_seq) == 0
    last = (i % tiles_per_seq) == tiles_per_seq - 1

    @pl.when(s == 0)
    def _():
        act1_ref[...] = jnp.zeros_like(act1_ref)

    @pl.when(km == 0)
    def _():
        acc_ref[...] = jnp.zeros_like(acc_ref)

    def step(act_in, act_out):
        row_parts = tm // MXU_ROWS

        def mm_tile(n):
            rows = slice((n % row_parts) * MXU_ROWS, (n % row_parts + 1) * MXU_ROWS)
            cols = slice((n // row_parts) * MXU_COLS, (n // row_parts + 1) * MXU_COLS)
            acc_ref[rows, cols] += jnp.dot(act_in[rows, :], w_ref[:, cols], preferred_element_type=F32)

        _gate_chunk(gp_ref, gm_ref, gn_ref, val_ref, cw_ref, cb_ref, act_out, first, last, mm_tile,
                    row_parts * D_MODEL // MXU_COLS, tm=tm, grid_mode=grid_mode)

    @pl.when(s % 2 == 0)
    def _():
        step(act1_ref, act0_ref)

    @pl.when(s % 2 == 1)
    def _():
        step(act0_ref, act1_ref)

    @pl.when((km == nk - 1) & (s > 0))
    def _():
        x_new = x_ref[...] + mod_ref[0, 5:6, :] * (_rms_rows(acc_ref[...]) * nw_ref[...])
        o_ref[...] = x_new
        if next_norm:
            h_ref[...] = _prenorm(x_new, modn_ref, nwn_ref, 0)


def _ffn_down_call(gv2, cw, cb, w_bf, layer, x2, mod3, nw, tm, seq_len, mod_row, grid_mode, next_norm):
    m = x2.shape[0]
    tk = FFN_CHUNK
    nk = D_FF // tk
    hb = tm // GRID_W
    n_hblk = m // GRID_W
    tiles_per_seq = seq_len // tm
    n_steps = (m // tm) * nk + 1
    bi = lambda s: jnp.minimum(s, n_steps - 2) // nk
    bk = lambda s: jnp.minimum(s, n_steps - 2) % nk
    mi = lambda s: jnp.maximum(s - 1, 0) // nk
    mk = lambda s: jnp.maximum(s - 1, 0) % nk
    row_spec = pl.BlockSpec((tm, D_MODEL), lambda s: (mi(s), 0))
    mod_spec = pl.BlockSpec((1, MOD_ROWS, D_MODEL), lambda s: (mod_row(mi(s)), 0, 0))
    vec_spec = pl.BlockSpec((1, D_MODEL), lambda s: (0, 0))
    extra_in, extra_specs, out_specs = (), [], row_spec
    out_shape = jax.ShapeDtypeStruct((m, D_MODEL), F32)
    if next_norm:
        extra_in, extra_specs = tuple(next_norm), [mod_spec, vec_spec]
        out_specs = [row_spec, row_spec]
        out_shape = [out_shape, jax.ShapeDtypeStruct((m, D_MODEL), BF16)]
    return pl.pallas_call(
        functools.partial(_ffn_down_kernel, tm=tm, nk=nk, tiles_per_seq=tiles_per_seq, grid_mode=grid_mode,
                          next_norm=bool(next_norm)),
        grid=(n_steps,),
        in_specs=[
            pl.BlockSpec((GRID_W, tk), lambda s: (jnp.maximum(bi(s) * hb - 1, 0), bk(s))),
            pl.BlockSpec((tm, tk), lambda s: (bi(s), bk(s))),
            pl.BlockSpec((GRID_W, tk), lambda s: (jnp.minimum((bi(s) + 1) * hb, n_hblk - 1), bk(s))),
            pl.BlockSpec((tm, tk), lambda s: (bi(s), nk + bk(s))),
            pl.BlockSpec((3, 3, tk), lambda s: (0, 0, bk(s))),
            pl.BlockSpec((1, tk), lambda s: (0, bk(s))),
            pl.BlockSpec((None, tk, D_MODEL), lambda s: (layer, mk(s), 0)),
            row_spec,
            mod_spec,
            vec_spec,
        ] + extra_specs,
        out_specs=out_specs,
        out_shape=out_shape,
        scratch_shapes=[
            pltpu.VMEM((tm, D_MODEL), F32),
            pltpu.VMEM((tm, tk), BF16),
            pltpu.VMEM((tm, tk), BF16),
        ],
        compiler_params=_cparams(("arbitrary",)),
        name="ffn_down",
    )(gv2, gv2, gv2, gv2, cw, cb.reshape(1, -1), w_bf, x2, mod3, nw, *extra_in)


class _Tiles(NamedTuple):
    proj: int
    proj_f32: int
    norm: int
    out: int
    down: int
    scan: int
    cols: int


def _stream(x3, h2, mod, l, params, s0, consts, tiles, row_of, grid_mode, full, emit_next):
    bsz, length, _ = x3.shape
    m = bsz * length
    x2 = x3.reshape(m, D_MODEL)
    mod3 = mod[l]
    nw = params["norm_w"][l]
    tn = tiles.cols
    w_in = params["w_in"]
    if h2 is None:
        h2 = _prenorm_call(x2, mod3, nw[0:1], tiles.norm, row_of(tiles.norm), 0, "mix_prenorm")
    f2 = _proj_call(h2, w_in, l, HG_WIDTH, 2 * HG_WIDTH, tiles.proj_f32, tn, False, F32, "in_proj_f")
    if full:
        q2 = _proj_call(h2, w_in, l, 0, HG_WIDTH, tiles.proj, tn, True, BF16, "in_proj_q")
        r2 = _proj_call(h2, w_in, l, 3 * HG_WIDTH, IN_COLS - 3 * HG_WIDTH, tiles.proj, tn, False, BF16,
                        "in_proj_vgab")
    else:
        r2 = _proj_call(h2, w_in, l, 3 * HG_WIDTH, HG_WIDTH, tiles.proj, tn, False, BF16, "in_proj_v")
        q2 = r2
    of3, ob3, sfin = _scan_call(q2.reshape(bsz, length, -1), f2.reshape(bsz, length, -1),
                                r2.reshape(bsz, length, -1), params["lb_logits"], s0, consts, l,
                                tiles.scan)
    if not full:
        return None, None, sfin
    cv3 = _cconv_call(r2.reshape(bsz, length, -1), params["conv_w"][l], params["conv_b"][l],
                      params["conv_ln_w"][l], params["conv_ln_b"][l])
    x2, h2 = _outproj_call(of3.reshape(m, -1), ob3.reshape(m, -1), r2, cv3.reshape(m, -1), x2, mod3,
                           params["hg_norm_w"][l].reshape(1, -1), nw[1:2], nw[2:3], params["w_out"], l,
                           tiles.out, row_of(tiles.out))
    gv2 = _proj_call(h2, params["ffn_up"], l, 0, 2 * D_FF, tiles.proj, tn, False, BF16, "ffn_up")
    next_norm = (mod[l + 1], params["norm_w"][l + 1][0:1]) if emit_next else None
    out = _ffn_down_call(gv2, params["ffn_conv_w"][l], params["ffn_conv_b"][l], params["ffn_down"], l,
                         x2, mod3, nw[3:4], tiles.down, length, row_of(tiles.down), grid_mode, next_norm)
    x2, h_next = out if emit_next else (out, None)
    return x2.reshape(bsz, length, D_MODEL), h_next, sfin


def kernel(x, c, ctx, c_ctx, w_mod, b_mod, norm_w, w_in, lb_logits, hg_norm_w, conv_w, conv_b, conv_ln_w,
           conv_ln_b, w_out, ffn_up, ffn_conv_w, ffn_conv_b, ffn_down):
    bsz, seq, _ = x.shape
    ctx_len = ctx.shape[1]
    params = dict(norm_w=norm_w, w_in=w_in, lb_logits=lb_logits, hg_norm_w=hg_norm_w,
                  conv_w=conv_w, conv_b=conv_b, conv_ln_w=conv_ln_w, conv_ln_b=conv_ln_b,
                  w_out=w_out.astype(BF16), ffn_up=ffn_up, ffn_conv_w=ffn_conv_w,
                  ffn_conv_b=ffn_conv_b, ffn_down=ffn_down.astype(BF16))
    consts = jnp.asarray(_scan_constants(), jnp.int32)

    cond = jnp.concatenate([c, c_ctx[None, :], jnp.zeros((MOD_ROWS - bsz - 1, D_MODEL), c.dtype)], axis=0)
    mod = _mod_call(_silu(cond).astype(BF16), w_mod, b_mod)
    mod = mod[:, :bsz + 1].reshape(DEPTH, bsz + 1, N_MOD, D_MODEL)
    mod = jnp.pad(mod, ((0, 0), (0, 0), (0, MOD_ROWS - N_MOD), (0, 0)))

    lat_row = lambda tm: (lambda i: i // (seq // tm))
    ctx_row = lambda tm: (lambda i: bsz)
    zero_state = jnp.zeros((bsz, HEADS, 2, HEAD_DIM, HEAD_DIM), F32)
    ctx_rows = bsz * ctx_len
    ctx_tiles = _Tiles(proj=ctx_rows, proj_f32=ctx_rows, norm=ctx_len, out=ctx_len, down=ctx_len,
                       scan=ctx_len, cols=1024)
    lat_tiles = _Tiles(proj=2048, proj_f32=1024, norm=512, out=512, down=512, scan=2048, cols=1024)
    xc, hc, h = ctx, None, None
    for l in range(DEPTH):
        last = l == DEPTH - 1
        xc, hc, s_ctx = _stream(xc, hc, mod, l, params, zero_state, consts, ctx_tiles, ctx_row,
                                grid_mode=False, full=not last, emit_next=not last)
        x, h, _ = _stream(x, h, mod, l, params, s_ctx, consts, lat_tiles, lat_row,
                          grid_mode=True, full=True, emit_next=not last)
    return x
```

```python
import functools
from typing import NamedTuple

import numpy as np
import jax
import jax.numpy as jnp
from jax import lax
from jax.experimental import pallas as pl
from jax.experimental.pallas import tpu as pltpu

F32 = jnp.float32
BF16 = jnp.bfloat16

D_MODEL = 2048
DEPTH = 2
GRID_W = 64
HEADS = 8
HEAD_DIM = 128
HG_WIDTH = HEADS * HEAD_DIM
CONV_CH = D_MODEL - HG_WIDTH
CONV_WIDTH = 31
CONV_PAD = 16
D_FF = 5632
N_MOD = 6
MOD_ROWS = 8
EPS = 1e-6
LN_EPS = 1e-5
LOG2_E = float(np.log2(np.e))
IN_COLS = 3 * HG_WIDTH + 2 * HG_WIDTH + 2 * CONV_CH

CHUNK = 64
N_LEVELS = 6

VMEM_LIMIT = 56 * 1024 * 1024


def _cparams(sem):
    return pltpu.CompilerParams(dimension_semantics=sem, vmem_limit_bytes=VMEM_LIMIT)


def _silu(x):
    return x * jax.nn.sigmoid(x)


def _rms_rows(x):
    return x * lax.rsqrt(jnp.mean(x * x, axis=-1, keepdims=True) + EPS)


def _mod_kernel(s_ref, w_ref, b_ref, o_ref):
    w = w_ref[0].astype(BF16)
    o_ref[0] = jnp.dot(s_ref[...], w, preferred_element_type=F32) + b_ref[0]


def _mod_call(s_rows, w_mod, b_mod):
    tn = 1024
    n = w_mod.shape[-1]
    return pl.pallas_call(
        _mod_kernel,
        grid=(DEPTH, n // tn),
        in_specs=[
            pl.BlockSpec((MOD_ROWS, D_MODEL), lambda l, j: (0, 0)),
            pl.BlockSpec((1, D_MODEL, tn), lambda l, j: (l, 0, j)),
            pl.BlockSpec((1, 1, tn), lambda l, j: (l, 0, j)),
        ],
        out_specs=pl.BlockSpec((1, MOD_ROWS, tn), lambda l, j: (l, 0, j)),
        out_shape=jax.ShapeDtypeStruct((DEPTH, MOD_ROWS, n), F32),
        compiler_params=_cparams(("arbitrary", "arbitrary")),
        name="mod_matmul",
    )(s_rows, w_mod, b_mod.reshape(DEPTH, 1, n))


def _prenorm(x, mod_ref, nw_ref, shift_row):
    y = _rms_rows(x) * nw_ref[...]
    h = y * (1.0 + mod_ref[0, shift_row + 1:shift_row + 2, :]) + mod_ref[0, shift_row:shift_row + 1, :]
    return h.astype(BF16)


def _prenorm_kernel(x_ref, mod_ref, nw_ref, h_ref, *, shift_row):
    h_ref[...] = _prenorm(x_ref[...], mod_ref, nw_ref, shift_row)


def _prenorm_call(x2, mod3, nw, tm, mod_row, shift_row, name):
    m = x2.shape[0]
    return pl.pallas_call(
        functools.partial(_prenorm_kernel, shift_row=shift_row),
        grid=(m // tm,),
        in_specs=[
            pl.BlockSpec((tm, D_MODEL), lambda i: (i, 0)),
            pl.BlockSpec((1, MOD_ROWS, D_MODEL), lambda i: (mod_row(i), 0, 0)),
            pl.BlockSpec((1, D_MODEL), lambda i: (0, 0)),
        ],
        out_specs=pl.BlockSpec((tm, D_MODEL), lambda i: (i, 0)),
        out_shape=jax.ShapeDtypeStruct((m, D_MODEL), BF16),
        compiler_params=_cparams(("arbitrary",)),
        name=name,
    )(x2, mod3, nw)


def _proj_kernel(h_ref, w_ref, o_ref, wb_ref, *, act):
    @pl.when(pl.program_id(1) == 0)
    def _():
        wb_ref[...] = w_ref[...].astype(BF16)

    acc = jnp.dot(h_ref[...], wb_ref[...], preferred_element_type=F32)
    if act:
        acc = _silu(acc)
    o_ref[...] = acc.astype(o_ref.dtype)


def _proj_call(h2, w, layer, col0, n_cols, tm, tn, act, out_dtype, name):
    m = h2.shape[0]
    c0 = col0 // tn
    return pl.pallas_call(
        functools.partial(_proj_kernel, act=act),
        grid=(n_cols // tn, m // tm),
        in_specs=[
            pl.BlockSpec((tm, D_MODEL), lambda j, i: (i, 0)),
            pl.BlockSpec((None, D_MODEL, tn), lambda j, i: (layer, 0, c0 + j)),
        ],
        out_specs=pl.BlockSpec((tm, tn), lambda j, i: (i, j)),
        out_shape=jax.ShapeDtypeStruct((m, n_cols), out_dtype),
        scratch_shapes=[pltpu.VMEM((D_MODEL, tn), BF16)],
        compiler_params=_cparams(("arbitrary", "arbitrary")),
        name=name,
    )(h2, w)


def _scan_constants():
    c = CHUNK
    t = np.arange(c)[:, None]
    r = np.arange(c)[None, :]
    level = np.full((2, c, c), -1, np.int32)
    for lvl in range(N_LEVELS):
        half = c >> (lvl + 1)
        same = (t // (2 * half)) == (r // (2 * half))
        t_late = (t % (2 * half)) >= half
        r_late = (r % (2 * half)) >= half
        level[0][same & t_late & ~r_late] = lvl
        level[1][same & ~t_late & r_late] = lvl
    level[:, np.arange(c), np.arange(c)] = N_LEVELS
    return level


_NT = (((1,), (1,)), ((), ()))
_TN = (((0,), (0,)), ((), ()))


def _neg_abs(x):
    bits = lax.bitcast_convert_type(x, jnp.int32) | jnp.int32(-2 ** 31)
    return lax.bitcast_convert_type(bits, F32)


def _scan_kernel(lbl_ref, level_ref, qf_ref, qb_ref, vf_ref, vb_ref, ff_ref, fb_ref,
                 s0_ref, of_ref, ob_ref, sfin_ref, st_ref, *, layer, n_chunks):
    c = CHUNK
    step = pl.program_id(2)

    @pl.when(step == 0)
    def _():
        st_ref[...] = s0_ref[0, 0]

    rows = [lbl_ref[:, j, :] for j in range(DEPTH)]
    mx = functools.reduce(jnp.maximum, rows)
    es = [jnp.exp(rw - mx) for rw in rows]
    tot = functools.reduce(lambda a, b: a + b, es)
    lb = jnp.zeros_like(mx)
    for j in range(1, layer + 1):
        lb = lb + es[j] / tot

    level = level_ref[...]
    sub = lax.broadcasted_iota(jnp.int32, (c // 8, 8, HEAD_DIM), 1)

    def ref_rows(b, half, d):
        pick = half - 1 if d == 0 else half
        if half >= 8:
            parts = []
            for p0 in range(0, c, 2 * half):
                parts.append(jnp.broadcast_to(b[p0 + pick:p0 + pick + 1, :], (2 * half, HEAD_DIM)))
            return parts[0] if len(parts) == 1 else jnp.concatenate(parts, axis=0)
        b3 = b.reshape(c // 8, 8, HEAD_DIM)
        out = None
        for g0 in range(0, 8, 2 * half):
            cand = jnp.broadcast_to(b3[:, g0 + pick:g0 + pick + 1, :], b3.shape)
            out = cand if out is None else jnp.where(sub >= g0, cand, out)
        return out.reshape(c, HEAD_DIM)

    refs = ((qf_ref, vf_ref, ff_ref, of_ref), (qb_ref, vb_ref, fb_ref, ob_ref))

    def gates(d, row0):
        raw = refs[d][2][0, row0:row0 + c, :]
        lbd = lb[d:d + 1, :]
        f = lbd + (1.0 - lbd) * jax.nn.sigmoid(raw)
        x = (jnp.log(f) * LOG2_E).reshape(c // 8, 8, HEAD_DIM)
        for s in (1, 2, 4):
            if d == 0:
                x = x + jnp.where(sub >= s, pltpu.roll(x, s, axis=1), 0.0)
            else:
                x = x + jnp.where(sub < 8 - s, pltpu.roll(x, 8 - s, axis=1), 0.0)
        groups = [x[g] for g in range(c // 8)]
        edge = 7 if d == 0 else 0
        totals = [grp[edge:edge + 1, :] for grp in groups]
        order = range(1, c // 8) if d == 0 else range(c // 8 - 2, -1, -1)
        carry = None
        for g in order:
            before = totals[g - 1] if d == 0 else totals[g + 1]
            carry = before if carry is None else carry + before
            groups[g] = groups[g] + carry
        return f, jnp.concatenate(groups, axis=0)

    none = jnp.full((c, c), -1, jnp.int32)
    level2 = jnp.concatenate([jnp.concatenate([level[0], none], axis=1),
                              jnp.concatenate([none, level[1]], axis=1)], axis=0)
    pad = jnp.zeros((c, HEAD_DIM), BF16)

    def diag2(fwd, bwd, fill):
        return jnp.concatenate([jnp.concatenate([fwd, fill], axis=1),
                                jnp.concatenate([fill, bwd], axis=1)], axis=0)

    def intra(rows, gated):
        q_bf, k_bf, b, own, pair = [], [], [], [], []
        for d in range(2):
            f, bd = gated[d]
            qd = refs[d][0][0, rows[d]:rows[d] + c, :]
            q = qd.astype(F32)
            key = 1.0 - f
            own.append(jnp.sum(q * key, axis=-1, keepdims=True))
            pair_key = pltpu.roll(key, 1 if d == 0 else c - 1, axis=0)
            pair.append(jnp.sum(q * f * pair_key, axis=-1, keepdims=True))
            q_bf.append(qd)
            k_bf.append(key.astype(BF16))
            b.append(bd)
        rows_of = lambda fwd, bwd: jnp.concatenate([jnp.broadcast_to(fwd, (c, 2 * c)),
                                                    jnp.broadcast_to(bwd, (c, 2 * c))], axis=0)
        att = jnp.where(level2 == N_LEVELS, rows_of(own[0], own[1]), 0.0)
        att = jnp.where(level2 == N_LEVELS - 1, rows_of(pair[0], pair[1]), att)
        for lvl in range(N_LEVELS - 1):
            half = c >> (lvl + 1)
            el = [jnp.exp2(_neg_abs(b[d] - ref_rows(b[d], half, d))).astype(BF16) for d in range(2)]
            lhs = diag2(q_bf[0] * el[0], q_bf[1] * el[1], pad)
            rhs = diag2(k_bf[0] * el[0], k_bf[1] * el[1], pad)
            att = jnp.where(level2 == lvl, lax.dot_general(lhs, rhs, _NT, preferred_element_type=F32), att)
        att = att.astype(BF16)
        out = []
        for d in range(2):
            last = c - 1 if d == 0 else 0
            blast = b[d][last:last + 1, :]
            q_in = q_bf[d] * jnp.exp2(b[d]).astype(BF16)
            k_out = k_bf[d] * jnp.exp2(blast - b[d]).astype(BF16)
            decay = jnp.transpose(jnp.broadcast_to(jnp.exp2(blast), (8, HEAD_DIM)))[:, 0:1]
            out.append((jnp.concatenate([q_in, att[d * c:(d + 1) * c]], axis=1), k_out, decay))
        return out

    def readout(d, rows, q_att, k_out, decay, st):
        v = [refs[e][1][0, rows[e]:rows[e] + c, :] for e in range(2)]
        rhs = jnp.concatenate([st.astype(BF16), v[0], v[1]], axis=0)
        refs[d][3][0, rows[d]:rows[d] + c, :] = jnp.dot(q_att, rhs, preferred_element_type=F32)
        return st * decay + lax.dot_general(k_out, v[d], _TN, preferred_element_type=F32)

    work = [(ci * c, (n_chunks - 1 - ci) * c) for ci in range(n_chunks)]
    st = [st_ref[0], st_ref[1]]
    stage1 = {}
    stage2 = {}
    for n in range(len(work) + 2):
        if n >= 2:
            done = stage2.pop(n - 2)
            for d in range(2):
                st[d] = readout(d, work[n - 2], *done[d], st[d])
        if 1 <= n <= len(work):
            stage2[n - 1] = intra(work[n - 1], stage1.pop(n - 1))
        if n < len(work):
            stage1[n] = [gates(d, work[n][d]) for d in range(2)]
    st_ref[0] = st[0]
    st_ref[1] = st[1]

    @pl.when(step == pl.num_programs(2) - 1)
    def _():
        sfin_ref[0, 0] = st_ref[...]


def _scan_call(q3, f3, r3, lb_logits, s0, level, layer, tl):
    bsz, length, _ = q3.shape
    nl = length // tl
    fwd = lambda b, h, i: (b, i, h)
    bwd = lambda b, h, i: (b, nl - 1 - i, h)
    bwd_f = lambda b, h, i: (b, nl - 1 - i, HEADS + h)
    blk = (1, tl, HEAD_DIM)
    return pl.pallas_call(
        functools.partial(_scan_kernel, layer=layer, n_chunks=tl // CHUNK),
        grid=(bsz, HEADS, nl),
        in_specs=[
            pl.BlockSpec((2, DEPTH, HEAD_DIM), lambda b, h, i: (0, 0, h)),
            pl.BlockSpec(level.shape, lambda b, h, i: (0, 0, 0)),
            pl.BlockSpec(blk, fwd),
            pl.BlockSpec(blk, bwd),
            pl.BlockSpec(blk, fwd),
            pl.BlockSpec(blk, bwd),
            pl.BlockSpec(blk, fwd),
            pl.BlockSpec(blk, bwd_f),
            pl.BlockSpec((1, 1, 2, HEAD_DIM, HEAD_DIM), lambda b, h, i: (b, h, 0, 0, 0)),
        ],
        out_specs=[
            pl.BlockSpec(blk, fwd),
            pl.BlockSpec(blk, bwd),
            pl.BlockSpec((1, 1, 2, HEAD_DIM, HEAD_DIM), lambda b, h, i: (b, h, 0, 0, 0)),
        ],
        out_shape=[
            jax.ShapeDtypeStruct((bsz, length, HG_WIDTH), F32),
            jax.ShapeDtypeStruct((bsz, length, HG_WIDTH), F32),
            jax.ShapeDtypeStruct((bsz, HEADS, 2, HEAD_DIM, HEAD_DIM), F32),
        ],
        scratch_shapes=[pltpu.VMEM((2, HEAD_DIM, HEAD_DIM), F32)],
        compiler_params=_cparams(("arbitrary", "arbitrary", "arbitrary")),
        name="hgrn2_scan",
    )(lb_logits, level, q3, q3, r3, r3, f3, f3, s0)


def _cconv_kernel(a_ref, b_ref, cw_ref, cb_ref, lw_ref, lb_ref, o_ref, u_ref, *, length, rows):
    zeros = jnp.zeros((CONV_PAD, HEAD_DIM), F32)
    u_ref[0:CONV_PAD, :] = zeros
    u_ref[CONV_PAD + length:2 * CONV_PAD + length, :] = zeros
    n_tiles = length // rows

    def fill(ti, carry):
        r0 = pl.multiple_of(ti * rows, rows)
        a = a_ref[0, pl.ds(r0, rows), :].astype(F32)
        b = b_ref[0, pl.ds(r0, rows), :].astype(F32)
        u_ref[pl.ds(CONV_PAD + r0, rows), :] = a * jax.nn.sigmoid(b)
        return carry

    lax.fori_loop(0, n_tiles, fill, 0)

    half = CONV_WIDTH // 2

    def conv(ti, carry):
        r0 = pl.multiple_of(ti * rows, rows)
        lead = CONV_PAD - half
        span = -(-(lead + CONV_WIDTH - 1) // 8) * 8 - 8
        acc = jnp.zeros((rows, HEAD_DIM), F32)
        for phase in range(8):
            window = u_ref[pl.ds(r0 + phase, rows + span), :]
            for off in range(phase, lead + CONV_WIDTH, 8):
                if off >= lead:
                    j = off - lead
                    acc = acc + cw_ref[j:j + 1, :] * window[off - phase:off - phase + rows]
        acc = acc + cb_ref[...]
        mu = jnp.mean(acc, axis=-1, keepdims=True)
        dlt = acc - mu
        var = jnp.mean(dlt * dlt, axis=-1, keepdims=True)
        y = dlt * lax.rsqrt(var + LN_EPS) * lw_ref[...] + lb_ref[...]
        o_ref[0, pl.ds(r0, rows), :] = _silu(y).astype(BF16)
        return carry

    lax.fori_loop(0, n_tiles, conv, 0, unroll=4)


def _cconv_call(r3, cw, cb, lw, lb):
    bsz, length, _ = r3.shape
    groups = CONV_CH // HEAD_DIM
    a_col = 2 * HG_WIDTH // HEAD_DIM
    b_col = a_col + groups
    vec = pl.BlockSpec((1, HEAD_DIM), lambda b, g: (0, g))
    return pl.pallas_call(
        functools.partial(_cconv_kernel, length=length, rows=128),
        grid=(bsz, groups),
        in_specs=[
            pl.BlockSpec((1, length, HEAD_DIM), lambda b, g: (b, 0, a_col + g)),
            pl.BlockSpec((1, length, HEAD_DIM), lambda b, g: (b, 0, b_col + g)),
            pl.BlockSpec((CONV_WIDTH, HEAD_DIM), lambda b, g: (0, g)),
            vec, vec, vec,
        ],
        out_specs=pl.BlockSpec((1, length, HEAD_DIM), lambda b, g: (b, 0, g)),
        out_shape=jax.ShapeDtypeStruct((bsz, length, CONV_CH), BF16),
        scratch_shapes=[pltpu.VMEM((length + 2 * CONV_PAD, HEAD_DIM), F32)],
        compiler_params=_cparams(("arbitrary", "arbitrary")),
        name="conformer_conv",
    )(r3, r3, cw, cb.reshape(1, -1), lw.reshape(1, -1), lb.reshape(1, -1))


def _outproj_kernel(of_ref, ob_ref, g_ref, cv_ref, x_ref, mod_ref, hgw_ref, nw_ref, nw2_ref, w_ref, o_ref,
                    h_ref):
    o = of_ref[...] + ob_ref[...]
    heads = []
    for h in range(HEADS):
        oh = o[:, h * HEAD_DIM:(h + 1) * HEAD_DIM]
        heads.append(_rms_rows(oh) * hgw_ref[...])
    r = jnp.concatenate(heads, axis=-1) * _silu(g_ref[...].astype(F32))
    y = jnp.dot(r.astype(BF16), w_ref[0:HG_WIDTH, :], preferred_element_type=F32)
    y = y + jnp.dot(cv_ref[...], w_ref[HG_WIDTH:D_MODEL, :], preferred_element_type=F32)
    x_new = x_ref[...] + mod_ref[0, 2:3, :] * (_rms_rows(y) * nw_ref[...])
    o_ref[...] = x_new
    h_ref[...] = _prenorm(x_new, mod_ref, nw2_ref, 3)


def _outproj_call(of2, ob2, r2, cv2, x2, mod3, hgw, nw, nw2, w_bf, layer, tm, mod_row):
    m = x2.shape[0]
    return pl.pallas_call(
        _outproj_kernel,
        grid=(m // tm,),
        in_specs=[
            pl.BlockSpec((tm, HG_WIDTH), lambda i: (i, 0)),
            pl.BlockSpec((tm, HG_WIDTH), lambda i: (i, 0)),
            pl.BlockSpec((tm, HG_WIDTH), lambda i: (i, 1)),
            pl.BlockSpec((tm, CONV_CH), lambda i: (i, 0)),
            pl.BlockSpec((tm, D_MODEL), lambda i: (i, 0)),
            pl.BlockSpec((1, MOD_ROWS, D_MODEL), lambda i: (mod_row(i), 0, 0)),
            pl.BlockSpec((1, HEAD_DIM), lambda i: (0, 0)),
            pl.BlockSpec((1, D_MODEL), lambda i: (0, 0)),
            pl.BlockSpec((1, D_MODEL), lambda i: (0, 0)),
            pl.BlockSpec((None, D_MODEL, D_MODEL), lambda i: (layer, 0, 0), pipeline_mode=pl.Buffered(1)),
        ],
        out_specs=[
            pl.BlockSpec((tm, D_MODEL), lambda i: (i, 0)),
            pl.BlockSpec((tm, D_MODEL), lambda i: (i, 0)),
        ],
        out_shape=[
            jax.ShapeDtypeStruct((m, D_MODEL), F32),
            jax.ShapeDtypeStruct((m, D_MODEL), BF16),
        ],
        compiler_params=_cparams(("arbitrary",)),
        name="out_proj",
    )(of2, ob2, r2, cv2, x2, mod3, hgw, nw, nw2, w_bf)


ROW_BLOCK = 32
MXU_COLS = 256
MXU_ROWS = 256
FFN_CHUNK = 512
LANE_BLOCK = 256


def _gate_chunk(gp_ref, gm_ref, gn_ref, val_ref, cw_ref, cb_ref, act_out, first, last, mm_tile,
                mm_tiles, *, tm, grid_mode):
    tk = FFN_CHUNK
    rows = ROW_BLOCK
    lanes = LANE_BLOCK
    n_blocks = tm // rows
    total = n_blocks * (tk // lanes)
    row_id = lax.broadcasted_iota(jnp.int32, (rows, lanes), 0)
    edge = GRID_W if grid_mode else tm
    zero = jnp.zeros((), BF16)

    def gate_rows(r0, ls):
        if r0 < 0:
            return jnp.where(first, zero, gp_ref[GRID_W + r0:GRID_W + r0 + rows, ls])
        if r0 >= tm:
            return jnp.where(last, zero, gn_ref[r0 - tm:r0 - tm + rows, ls])
        return gm_ref[r0:r0 + rows, ls]

    done = 0
    for l0 in range(0, tk, lanes):
        ls = slice(l0, l0 + lanes)
        taps = (0.5 * cw_ref[:, :, ls]).astype(BF16)
        taps = [[jnp.broadcast_to(taps[dh, dw:dw + 1], (rows, lanes)) for dw in range(3)] for dh in range(3)]
        bias = 0.5 * cb_ref[:, ls]

        def columns(r0):
            ce = gate_rows(r0, ls)
            if not grid_mode:
                return [(taps[1][dw] * ce).astype(F32) for dw in range(3)]
            up = gate_rows(r0 - GRID_W, ls)
            dn = gate_rows(r0 + GRID_W, ls)
            return [(taps[0][dw] * up + taps[1][dw] * ce + taps[2][dw] * dn).astype(F32) for dw in range(3)]

        cur = columns(0)
        prev_left_row = None
        for blk in range(n_blocks):
            r0 = blk * rows
            for n in range(done * mm_tiles // total, (done + 1) * mm_tiles // total):
                mm_tile(n)
            done += 1
            nxt = columns(r0 + rows) if blk + 1 < n_blocks else None
            left, mid, right = cur
            from_prev = pltpu.roll(left, 1, axis=0)
            if r0 % edge == 0:
                from_prev = jnp.where(row_id == 0, 0.0, from_prev)
            else:
                from_prev = jnp.where(row_id == 0, prev_left_row, from_prev)
            from_next = pltpu.roll(right, rows - 1, axis=0)
            if (r0 + rows) % edge == 0:
                from_next = jnp.where(row_id == rows - 1, 0.0, from_next)
            else:
                from_next = jnp.where(row_id == rows - 1, nxt[2][0:1, :], from_next)
            half = mid + from_prev + from_next + bias
            gelu = half * (1.0 + lax.erf(half * np.float32(np.sqrt(2.0))))
            act_out[r0:r0 + rows, ls] = gelu.astype(BF16) * val_ref[r0:r0 + rows, ls]
            prev_left_row = left[rows - 1:rows, :]
            cur = nxt


def _ffn_down_kernel(gp_ref, gm_ref, gn_ref, val_ref, cw_ref, cb_ref, w_ref, x_ref, mod_ref, nw_ref, *rest,
                     tm, nk, tiles_per_seq, grid_mode, next_norm):
    if next_norm:
        modn_ref, nwn_ref, o_ref, h_ref, acc_ref, act0_ref, act1_ref = rest
    else:
        o_ref, acc_ref, act0_ref, act1_ref = rest
    s = pl.program_id(0)
    i = jnp.minimum(s, pl.num_programs(0) - 2) // nk
    km = jnp.maximum(s - 1, 0) % nk
    first = (i % tiles_per_seq) == 0
    last = (i % tiles_per_seq) == tiles_per_seq - 1

    @pl.when(s == 0)
    def _():
        act1_ref[...] = jnp.zeros_like(act1_ref)

    @pl.when(km == 0)
    def _():
        acc_ref[...] = jnp.zeros_like(acc_ref)

    def step(act_in, act_out):
        row_parts = tm // MXU_ROWS

        def mm_tile(n):
            rows = slice((n % row_parts) * MXU_ROWS, (n % row_parts + 1) * MXU_ROWS)
            cols = slice((n // row_parts) * MXU_COLS, (n // row_parts + 1) * MXU_COLS)
            acc_ref[rows, cols] += jnp.dot(act_in[rows, :], w_ref[:, cols], preferred_element_type=F32)

        _gate_chunk(gp_ref, gm_ref, gn_ref, val_ref, cw_ref, cb_ref, act_out, first, last, mm_tile,
                    row_parts * D_MODEL // MXU_COLS, tm=tm, grid_mode=grid_mode)

    @pl.when(s % 2 == 0)
    def _():
        step(act1_ref, act0_ref)

    @pl.when(s % 2 == 1)
    def _():
        step(act0_ref, act1_ref)

    @pl.when((km == nk - 1) & (s > 0))
    def _():
        x_new = x_ref[...] + mod_ref[0, 5:6, :] * (_rms_rows(acc_ref[...]) * nw_ref[...])
        o_ref[...] = x_new
        if next_norm:
            h_ref[...] = _prenorm(x_new, modn_ref, nwn_ref, 0)


def _ffn_down_call(gv2, cw, cb, w_bf, layer, x2, mod3, nw, tm, seq_len, mod_row, grid_mode, next_norm):
    m = x2.shape[0]
    tk = FFN_CHUNK
    nk = D_FF // tk
    hb = tm // GRID_W
    n_hblk = m // GRID_W
    tiles_per_seq = seq_len // tm
    n_steps = (m // tm) * nk + 1
    bi = lambda s: jnp.minimum(s, n_steps - 2) // nk
    bk = lambda s: jnp.minimum(s, n_steps - 2) % nk
    mi = lambda s: jnp.maximum(s - 1, 0) // nk
    mk = lambda s: jnp.maximum(s - 1, 0) % nk
    row_spec = pl.BlockSpec((tm, D_MODEL), lambda s: (mi(s), 0))
    mod_spec = pl.BlockSpec((1, MOD_ROWS, D_MODEL), lambda s: (mod_row(mi(s)), 0, 0))
    vec_spec = pl.BlockSpec((1, D_MODEL), lambda s: (0, 0))
    extra_in, extra_specs, out_specs = (), [], row_spec
    out_shape = jax.ShapeDtypeStruct((m, D_MODEL), F32)
    if next_norm:
        extra_in, extra_specs = tuple(next_norm), [mod_spec, vec_spec]
        out_specs = [row_spec, row_spec]
        out_shape = [out_shape, jax.ShapeDtypeStruct((m, D_MODEL), BF16)]
    return pl.pallas_call(
        functools.partial(_ffn_down_kernel, tm=tm, nk=nk, tiles_per_seq=tiles_per_seq, grid_mode=grid_mode,
                          next_norm=bool(next_norm)),
        grid=(n_steps,),
        in_specs=[
            pl.BlockSpec((GRID_W, tk), lambda s: (jnp.maximum(bi(s) * hb - 1, 0), bk(s))),
            pl.BlockSpec((tm, tk), lambda s: (bi(s), bk(s))),
            pl.BlockSpec((GRID_W, tk), lambda s: (jnp.minimum((bi(s) + 1) * hb, n_hblk - 1), bk(s))),
            pl.BlockSpec((tm, tk), lambda s: (bi(s), nk + bk(s))),
            pl.BlockSpec((3, 3, tk), lambda s: (0, 0, bk(s))),
            pl.BlockSpec((1, tk), lambda s: (0, bk(s))),
            pl.BlockSpec((None, tk, D_MODEL), lambda s: (layer, mk(s), 0)),
            row_spec,
            mod_spec,
            vec_spec,
        ] + extra_specs,
        out_specs=out_specs,
        out_shape=out_shape,
        scratch_shapes=[
            pltpu.VMEM((tm, D_MODEL), F32),
            pltpu.VMEM((tm, tk), BF16),
            pltpu.VMEM((tm, tk), BF16),
        ],
        compiler_params=_cparams(("arbitrary",)),
        name="ffn_down",
    )(gv2, gv2, gv2, gv2, cw, cb.reshape(1, -1), w_bf, x2, mod3, nw, *extra_in)


class _Tiles(NamedTuple):
    proj: int
    proj_f32: int
    norm: int
    out: int
    down: int
    scan: int
    cols: int


def _stream(x3, h2, mod, l, params, s0, consts, tiles, row_of, grid_mode, full, emit_next):
    bsz, length, _ = x3.shape
    m = bsz * length
    x2 = x3.reshape(m, D_MODEL)
    mod3 = mod[l]
    nw = params["norm_w"][l]
    tn = tiles.cols
    w_in = params["w_in"]
    if h2 is None:
        h2 = _prenorm_call(x2, mod3, nw[0:1], tiles.norm, row_of(tiles.norm), 0, "mix_prenorm")
    f2 = _proj_call(h2, w_in, l, HG_WIDTH, 2 * HG_WIDTH, tiles.proj_f32, tn, False, F32, "in_proj_f")
    if full:
        q2 = _proj_call(h2, w_in, l, 0, HG_WIDTH, tiles.proj, tn, True, BF16, "in_proj_q")
        r2 = _proj_call(h2, w_in, l, 3 * HG_WIDTH, IN_COLS - 3 * HG_WIDTH, tiles.proj, tn, False, BF16,
                        "in_proj_vgab")
    else:
        r2 = _proj_call(h2, w_in, l, 3 * HG_WIDTH, HG_WIDTH, tiles.proj, tn, False, BF16, "in_proj_v")
        q2 = r2
    of3, ob3, sfin = _scan_call(q2.reshape(bsz, length, -1), f2.reshape(bsz, length, -1),
                                r2.reshape(bsz, length, -1), params["lb_logits"], s0, consts, l,
                                tiles.scan)
    if not full:
        return None, None, sfin
    cv3 = _cconv_call(r2.reshape(bsz, length, -1), params["conv_w"][l], params["conv_b"][l],
                      params["conv_ln_w"][l], params["conv_ln_b"][l])
    x2, h2 = _outproj_call(of3.reshape(m, -1), ob3.reshape(m, -1), r2, cv3.reshape(m, -1), x2, mod3,
                           params["hg_norm_w"][l].reshape(1, -1), nw[1:2], nw[2:3], params["w_out"], l,
                           tiles.out, row_of(tiles.out))
    gv2 = _proj_call(h2, params["ffn_up"], l, 0, 2 * D_FF, tiles.proj, tn, False, BF16, "ffn_up")
    next_norm = (mod[l + 1], params["norm_w"][l + 1][0:1]) if emit_next else None
    out = _ffn_down_call(gv2, params["ffn_conv_w"][l], params["ffn_conv_b"][l], params["ffn_down"], l,
                         x2, mod3, nw[3:4], tiles.down, length, row_of(tiles.down), grid_mode, next_norm)
    x2, h_next = out if emit_next else (out, None)
    return x2.reshape(bsz, length, D_MODEL), h_next, sfin


def kernel(x, c, ctx, c_ctx, w_mod, b_mod, norm_w, w_in, lb_logits, hg_norm_w, conv_w, conv_b, conv_ln_w,
           conv_ln_b, w_out, ffn_up, ffn_conv_w, ffn_conv_b, ffn_down):
    bsz, seq, _ = x.shape
    ctx_len = ctx.shape[1]
    params = dict(norm_w=norm_w, w_in=w_in, lb_logits=lb_logits, hg_norm_w=hg_norm_w,
                  conv_w=conv_w, conv_b=conv_b, conv_ln_w=conv_ln_w, conv_ln_b=conv_ln_b,
                  w_out=w_out.astype(BF16), ffn_up=ffn_up, ffn_conv_w=ffn_conv_w,
                  ffn_conv_b=ffn_conv_b, ffn_down=ffn_down.astype(BF16))
    consts = jnp.asarray(_scan_constants(), jnp.int32)

    cond = jnp.concatenate([c, c_ctx[None, :], jnp.zeros((MOD_ROWS - bsz - 1, D_MODEL), c.dtype)], axis=0)
    mod = _mod_call(_silu(cond).astype(BF16), w_mod, b_mod)
    mod = mod[:, :bsz + 1].reshape(DEPTH, bsz + 1, N_MOD, D_MODEL)
    mod = jnp.pad(mod, ((0, 0), (0, 0), (0, MOD_ROWS - N_MOD), (0, 0)))

    lat_row = lambda tm: (lambda i: i // (seq // tm))
    ctx_row = lambda tm: (lambda i: bsz)
    zero_state = jnp.zeros((bsz, HEADS, 2, HEAD_DIM, HEAD_DIM), F32)
    ctx_rows = bsz * ctx_len
    ctx_tiles = _Tiles(proj=ctx_rows, proj_f32=ctx_rows, norm=ctx_len, out=ctx_len, down=ctx_len,
                       scan=ctx_len, cols=1024)
    lat_tiles = _Tiles(proj=2048, proj_f32=1024, norm=512, out=512, down=512, scan=2048, cols=1024)
    xc, hc, h = ctx, None, None
    for l in range(DEPTH):
        last = l == DEPTH - 1
        xc, hc, s_ctx = _stream(xc, hc, mod, l, params, zero_state, consts, ctx_tiles, ctx_row,
                                grid_mode=False, full=not last, emit_next=not last)
        x, h, _ = _stream(x, h, mod, l, params, s_ctx, consts, lat_tiles, lat_row,
                          grid_mode=True, full=True, emit_next=not last)
    return x
```

```python
import functools
from typing import NamedTuple

import numpy as np
import jax
import jax.numpy as jnp
from jax import lax
from jax.experimental import pallas as pl
from jax.experimental.pallas import tpu as pltpu

F32 = jnp.float32
BF16 = jnp.bfloat16

D_MODEL = 2048
DEPTH = 2
GRID_W = 64
HEADS = 8
HEAD_DIM = 128
HG_WIDTH = HEADS * HEAD_DIM
CONV_CH = D_MODEL - HG_WIDTH
CONV_WIDTH = 31
CONV_PAD = 16
D_FF = 5632
N_MOD = 6
MOD_ROWS = 8
EPS = 1e-6
LN_EPS = 1e-5
LOG2_E = float(np.log2(np.e))
IN_COLS = 3 * HG_WIDTH + 2 * HG_WIDTH + 2 * CONV_CH

CHUNK = 64
N_LEVELS = 6

VMEM_LIMIT = 56 * 1024 * 1024


def _cparams(sem):
    return pltpu.CompilerParams(dimension_semantics=sem, vmem_limit_bytes=VMEM_LIMIT)


def _silu(x):
    return x * jax.nn.sigmoid(x)


def _rms_rows(x):
    return x * lax.rsqrt(jnp.mean(x * x, axis=-1, keepdims=True) + EPS)


def _mod_kernel(s_ref, w_ref, b_ref, o_ref):
    w = w_ref[0].astype(BF16)
    o_ref[0] = jnp.dot(s_ref[...], w, preferred_element_type=F32) + b_ref[0]


def _mod_call(s_rows, w_mod, b_mod):
    tn = 1024
    n = w_mod.shape[-1]
    return pl.pallas_call(
        _mod_kernel,
        grid=(DEPTH, n // tn),
        in_specs=[
            pl.BlockSpec((MOD_ROWS, D_MODEL), lambda l, j: (0, 0)),
            pl.BlockSpec((1, D_MODEL, tn), lambda l, j: (l, 0, j)),
            pl.BlockSpec((1, 1, tn), lambda l, j: (l, 0, j)),
        ],
        out_specs=pl.BlockSpec((1, MOD_ROWS, tn), lambda l, j: (l, 0, j)),
        out_shape=jax.ShapeDtypeStruct((DEPTH, MOD_ROWS, n), F32),
        compiler_params=_cparams(("arbitrary", "arbitrary")),
        name="mod_matmul",
    )(s_rows, w_mod, b_mod.reshape(DEPTH, 1, n))


def _prenorm(x, mod_ref, nw_ref, shift_row):
    y = _rms_rows(x) * nw_ref[...]
    h = y * (1.0 + mod_ref[0, shift_row + 1:shift_row + 2, :]) + mod_ref[0, shift_row:shift_row + 1, :]
    return h.astype(BF16)


def _prenorm_kernel(x_ref, mod_ref, nw_ref, h_ref, *, shift_row):
    h_ref[...] = _prenorm(x_ref[...], mod_ref, nw_ref, shift_row)


def _prenorm_call(x2, mod3, nw, tm, mod_row, shift_row, name):
    m = x2.shape[0]
    return pl.pallas_call(
        functools.partial(_prenorm_kernel, shift_row=shift_row),
        grid=(m // tm,),
        in_specs=[
            pl.BlockSpec((tm, D_MODEL), lambda i: (i, 0)),
            pl.BlockSpec((1, MOD_ROWS, D_MODEL), lambda i: (mod_row(i), 0, 0)),
            pl.BlockSpec((1, D_MODEL), lambda i: (0, 0)),
        ],
        out_specs=pl.BlockSpec((tm, D_MODEL), lambda i: (i, 0)),
        out_shape=jax.ShapeDtypeStruct((m, D_MODEL), BF16),
        compiler_params=_cparams(("arbitrary",)),
        name=name,
    )(x2, mod3, nw)


def _proj_kernel(h_ref, w_ref, o_ref, wb_ref, *, act):
    @pl.when(pl.program_id(1) == 0)
    def _():
        wb_ref[...] = w_ref[...].astype(BF16)

    acc = jnp.dot(h_ref[...], wb_ref[...], preferred_element_type=F32)
    if act:
        acc = _silu(acc)
    o_ref[...] = acc.astype(o_ref.dtype)


def _proj_call(h2, w, layer, col0, n_cols, tm, tn, act, out_dtype, name):
    m = h2.shape[0]
    c0 = col0 // tn
    return pl.pallas_call(
        functools.partial(_proj_kernel, act=act),
        grid=(n_cols // tn, m // tm),
        in_specs=[
            pl.BlockSpec((tm, D_MODEL), lambda j, i: (i, 0)),
            pl.BlockSpec((None, D_MODEL, tn), lambda j, i: (layer, 0, c0 + j)),
        ],
        out_specs=pl.BlockSpec((tm, tn), lambda j, i: (i, j)),
        out_shape=jax.ShapeDtypeStruct((m, n_cols), out_dtype),
        scratch_shapes=[pltpu.VMEM((D_MODEL, tn), BF16)],
        compiler_params=_cparams(("arbitrary", "arbitrary")),
        name=name,
    )(h2, w)


def _scan_constants():
    c = CHUNK
    t = np.arange(c)[:, None]
    r = np.arange(c)[None, :]
    level = np.full((2, c, c), -1, np.int32)
    for lvl in range(N_LEVELS):
        half = c >> (lvl + 1)
        same = (t // (2 * half)) == (r // (2 * half))
        t_late = (t % (2 * half)) >= half
        r_late = (r % (2 * half)) >= half
        level[0][same & t_late & ~r_late] = lvl
        level[1][same & ~t_late & r_late] = lvl
    level[:, np.arange(c), np.arange(c)] = N_LEVELS
    return level


_NT = (((1,), (1,)), ((), ()))
_TN = (((0,), (0,)), ((), ()))


def _neg_abs(x):
    bits = lax.bitcast_convert_type(x, jnp.int32) | jnp.int32(-2 ** 31)
    return lax.bitcast_convert_type(bits, F32)


def _scan_kernel(lbl_ref, level_ref, qf_ref, qb_ref, vf_ref, vb_ref, ff_ref, fb_ref,
                 s0_ref, of_ref, ob_ref, sfin_ref, st_ref, *, layer, n_chunks):
    c = CHUNK
    step = pl.program_id(2)

    @pl.when(step == 0)
    def _():
        st_ref[...] = s0_ref[0, 0]

    rows = [lbl_ref[:, j, :] for j in range(DEPTH)]
    mx = functools.reduce(jnp.maximum, rows)
    es = [jnp.exp(rw - mx) for rw in rows]
    tot = functools.reduce(lambda a, b: a + b, es)
    lb = jnp.zeros_like(mx)
    for j in range(1, layer + 1):
        lb = lb + es[j] / tot

    level = level_ref[...]
    sub = lax.broadcasted_iota(jnp.int32, (c // 8, 8, HEAD_DIM), 1)

    def ref_rows(b, half, d):
        pick = half - 1 if d == 0 else half
        if half >= 8:
            parts = []
            for p0 in range(0, c, 2 * half):
                parts.append(jnp.broadcast_to(b[p0 + pick:p0 + pick + 1, :], (2 * half, HEAD_DIM)))
            return parts[0] if len(parts) == 1 else jnp.concatenate(parts, axis=0)
        b3 = b.reshape(c // 8, 8, HEAD_DIM)
        out = None
        for g0 in range(0, 8, 2 * half):
            cand = jnp.broadcast_to(b3[:, g0 + pick:g0 + pick + 1, :], b3.shape)
            out = cand if out is None else jnp.where(sub >= g0, cand, out)
        return out.reshape(c, HEAD_DIM)

    refs = ((qf_ref, vf_ref, ff_ref, of_ref), (qb_ref, vb_ref, fb_ref, ob_ref))

    def gates(d, row0):
        raw = refs[d][2][0, row0:row0 + c, :]
        lbd = lb[d:d + 1, :]
        f = lbd + (1.0 - lbd) * jax.nn.sigmoid(raw)
        x = (jnp.log(f) * LOG2_E).reshape(c // 8, 8, HEAD_DIM)
        for s in (1, 2, 4):
            if d == 0:
                x = x + jnp.where(sub >= s, pltpu.roll(x, s, axis=1), 0.0)
            else:
                x = x + jnp.where(sub < 8 - s, pltpu.roll(x, 8 - s, axis=1), 0.0)
        groups = [x[g] for g in range(c // 8)]
        edge = 7 if d == 0 else 0
        totals = [grp[edge:edge + 1, :] for grp in groups]
        order = range(1, c // 8) if d == 0 else range(c // 8 - 2, -1, -1)
        carry = None
        for g in order:
            before = totals[g - 1] if d == 0 else totals[g + 1]
            carry = before if carry is None else carry + before
            groups[g] = groups[g] + carry
        return f, jnp.concatenate(groups, axis=0)

    none = jnp.full((c, c), -1, jnp.int32)
    level2 = jnp.concatenate([jnp.concatenate([level[0], none], axis=1),
                              jnp.concatenate([none, level[1]], axis=1)], axis=0)
    pad = jnp.zeros((c, HEAD_DIM), BF16)

    def diag2(fwd, bwd, fill):
        return jnp.concatenate([jnp.concatenate([fwd, fill], axis=1),
                                jnp.concatenate([fill, bwd], axis=1)], axis=0)

    def intra(rows, gated):
        q_bf, k_bf, b, own, pair = [], [], [], [], []
        for d in range(2):
            f, bd = gated[d]
            qd = refs[d][0][0, rows[d]:rows[d] + c, :]
            q = qd.astype(F32)
            key = 1.0 - f
            own.append(jnp.sum(q * key, axis=-1, keepdims=True))
            pair_key = pltpu.roll(key, 1 if d == 0 else c - 1, axis=0)
            pair.append(jnp.sum(q * f * pair_key, axis=-1, keepdims=True))
            q_bf.append(qd)
            k_bf.append(key.astype(BF16))
            b.append(bd)
        rows_of = lambda fwd, bwd: jnp.concatenate([jnp.broadcast_to(fwd, (c, 2 * c)),
                                                    jnp.broadcast_to(bwd, (c, 2 * c))], axis=0)
        att = jnp.where(level2 == N_LEVELS, rows_of(own[0], own[1]), 0.0)
        att = jnp.where(level2 == N_LEVELS - 1, rows_of(pair[0], pair[1]), att)
        for lvl in range(N_LEVELS - 1):
            half = c >> (lvl + 1)
            el = [jnp.exp2(_neg_abs(b[d] - ref_rows(b[d], half, d))).astype(BF16) for d in range(2)]
            lhs = diag2(q_bf[0] * el[0], q_bf[1] * el[1], pad)
            rhs = diag2(k_bf[0] * el[0], k_bf[1] * el[1], pad)
            att = jnp.where(level2 == lvl, lax.dot_general(lhs, rhs, _NT, preferred_element_type=F32), att)
        q_in, k_out, decay = [], [], []
        for d in range(2):
            last = c - 1 if d == 0 else 0
            blast = b[d][last:last + 1, :]
            q_in.append(q_bf[d] * jnp.exp2(b[d]).astype(BF16))
            k_out.append(k_bf[d] * jnp.exp2(blast - b[d]).astype(BF16))
            decay.append(jnp.transpose(jnp.broadcast_to(jnp.exp2(blast), (8, HEAD_DIM)))[:, 0:1])
        q_att = jnp.concatenate([diag2(q_in[0], q_in[1], pad), att.astype(BF16)], axis=1)
        return q_att, diag2(k_out[0], k_out[1], pad), jnp.concatenate(decay, axis=0)

    def readout(rows, q_att, k_out, decay, st):
        v = jnp.concatenate([refs[d][1][0, rows[d]:rows[d] + c, :] for d in range(2)], axis=0)
        o = jnp.dot(q_att, jnp.concatenate([st.astype(BF16), v], axis=0), preferred_element_type=F32)
        for d in range(2):
            refs[d][3][0, rows[d]:rows[d] + c, :] = o[d * c:(d + 1) * c]
        return st * decay + lax.dot_general(k_out, v, _TN, preferred_element_type=F32)

    work = [(ci * c, (n_chunks - 1 - ci) * c) for ci in range(n_chunks)]
    st = st_ref[...].reshape(2 * HEAD_DIM, HEAD_DIM)
    stage1 = {}
    stage2 = {}
    for n in range(len(work) + 2):
        if n >= 2:
            st = readout(work[n - 2], *stage2.pop(n - 2), st)
        if 1 <= n <= len(work):
            stage2[n - 1] = intra(work[n - 1], stage1.pop(n - 1))
        if n < len(work):
            stage1[n] = [gates(d, work[n][d]) for d in range(2)]
    st_ref[...] = st.reshape(2, HEAD_DIM, HEAD_DIM)

    @pl.when(step == pl.num_programs(2) - 1)
    def _():
        sfin_ref[0, 0] = st_ref[...]


def _scan_call(q3, f3, r3, lb_logits, s0, level, layer, tl):
    bsz, length, _ = q3.shape
    nl = length // tl
    fwd = lambda b, h, i: (b, i, h)
    bwd = lambda b, h, i: (b, nl - 1 - i, h)
    bwd_f = lambda b, h, i: (b, nl - 1 - i, HEADS + h)
    blk = (1, tl, HEAD_DIM)
    return pl.pallas_call(
        functools.partial(_scan_kernel, layer=layer, n_chunks=tl // CHUNK),
        grid=(bsz, HEADS, nl),
        in_specs=[
            pl.BlockSpec((2, DEPTH, HEAD_DIM), lambda b, h, i: (0, 0, h)),
            pl.BlockSpec(level.shape, lambda b, h, i: (0, 0, 0)),
            pl.BlockSpec(blk, fwd),
            pl.BlockSpec(blk, bwd),
            pl.BlockSpec(blk, fwd),
            pl.BlockSpec(blk, bwd),
            pl.BlockSpec(blk, fwd),
            pl.BlockSpec(blk, bwd_f),
            pl.BlockSpec((1, 1, 2, HEAD_DIM, HEAD_DIM), lambda b, h, i: (b, h, 0, 0, 0)),
        ],
        out_specs=[
            pl.BlockSpec(blk, fwd),
            pl.BlockSpec(blk, bwd),
            pl.BlockSpec((1, 1, 2, HEAD_DIM, HEAD_DIM), lambda b, h, i: (b, h, 0, 0, 0)),
        ],
        out_shape=[
            jax.ShapeDtypeStruct((bsz, length, HG_WIDTH), F32),
            jax.ShapeDtypeStruct((bsz, length, HG_WIDTH), F32),
            jax.ShapeDtypeStruct((bsz, HEADS, 2, HEAD_DIM, HEAD_DIM), F32),
        ],
        scratch_shapes=[pltpu.VMEM((2, HEAD_DIM, HEAD_DIM), F32)],
        compiler_params=_cparams(("arbitrary", "arbitrary", "arbitrary")),
        name="hgrn2_scan",
    )(lb_logits, level, q3, q3, r3, r3, f3, f3, s0)


def _cconv_kernel(a_ref, b_ref, cw_ref, cb_ref, lw_ref, lb_ref, o_ref, u_ref, *, length, rows):
    zeros = jnp.zeros((CONV_PAD, HEAD_DIM), F32)
    u_ref[0:CONV_PAD, :] = zeros
    u_ref[CONV_PAD + length:2 * CONV_PAD + length, :] = zeros
    n_tiles = length // rows

    def fill(ti, carry):
        r0 = pl.multiple_of(ti * rows, rows)
        a = a_ref[0, pl.ds(r0, rows), :].astype(F32)
        b = b_ref[0, pl.ds(r0, rows), :].astype(F32)
        u_ref[pl.ds(CONV_PAD + r0, rows), :] = a * jax.nn.sigmoid(b)
        return carry

    lax.fori_loop(0, n_tiles, fill, 0)

    half = CONV_WIDTH // 2

    def conv(ti, carry):
        r0 = pl.multiple_of(ti * rows, rows)
        lead = CONV_PAD - half
        span = -(-(lead + CONV_WIDTH - 1) // 8) * 8 - 8
        acc = jnp.zeros((rows, HEAD_DIM), F32)
        for phase in range(8):
            window = u_ref[pl.ds(r0 + phase, rows + span), :]
            for off in range(phase, lead + CONV_WIDTH, 8):
                if off >= lead:
                    j = off - lead
                    acc = acc + cw_ref[j:j + 1, :] * window[off - phase:off - phase + rows]
        acc = acc + cb_ref[...]
        mu = jnp.mean(acc, axis=-1, keepdims=True)
        dlt = acc - mu
        var = jnp.mean(dlt * dlt, axis=-1, keepdims=True)
        y = dlt * lax.rsqrt(var + LN_EPS) * lw_ref[...] + lb_ref[...]
        o_ref[0, pl.ds(r0, rows), :] = _silu(y).astype(BF16)
        return carry

    lax.fori_loop(0, n_tiles, conv, 0, unroll=4)


def _cconv_call(r3, cw, cb, lw, lb):
    bsz, length, _ = r3.shape
    groups = CONV_CH // HEAD_DIM
    a_col = 2 * HG_WIDTH // HEAD_DIM
    b_col = a_col + groups
    vec = pl.BlockSpec((1, HEAD_DIM), lambda b, g: (0, g))
    return pl.pallas_call(
        functools.partial(_cconv_kernel, length=length, rows=128),
        grid=(bsz, groups),
        in_specs=[
            pl.BlockSpec((1, length, HEAD_DIM), lambda b, g: (b, 0, a_col + g)),
            pl.BlockSpec((1, length, HEAD_DIM), lambda b, g: (b, 0, b_col + g)),
            pl.BlockSpec((CONV_WIDTH, HEAD_DIM), lambda b, g: (0, g)),
            vec, vec, vec,
        ],
        out_specs=pl.BlockSpec((1, length, HEAD_DIM), lambda b, g: (b, 0, g)),
        out_shape=jax.ShapeDtypeStruct((bsz, length, CONV_CH), BF16),
        scratch_shapes=[pltpu.VMEM((length + 2 * CONV_PAD, HEAD_DIM), F32)],
        compiler_params=_cparams(("arbitrary", "arbitrary")),
        name="conformer_conv",
    )(r3, r3, cw, cb.reshape(1, -1), lw.reshape(1, -1), lb.reshape(1, -1))


def _outproj_kernel(of_ref, ob_ref, g_ref, cv_ref, x_ref, mod_ref, hgw_ref, nw_ref, nw2_ref, w_ref, o_ref,
                    h_ref):
    o = of_ref[...] + ob_ref[...]
    heads = []
    for h in range(HEADS):
        oh = o[:, h * HEAD_DIM:(h + 1) * HEAD_DIM]
        heads.append(_rms_rows(oh) * hgw_ref[...])
    r = jnp.concatenate(heads, axis=-1) * _silu(g_ref[...].astype(F32))
    y = jnp.dot(r.astype(BF16), w_ref[0:HG_WIDTH, :], preferred_element_type=F32)
    y = y + jnp.dot(cv_ref[...], w_ref[HG_WIDTH:D_MODEL, :], preferred_element_type=F32)
    x_new = x_ref[...] + mod_ref[0, 2:3, :] * (_rms_rows(y) * nw_ref[...])
    o_ref[...] = x_new
    h_ref[...] = _prenorm(x_new, mod_ref, nw2_ref, 3)


def _outproj_call(of2, ob2, r2, cv2, x2, mod3, hgw, nw, nw2, w_bf, layer, tm, mod_row):
    m = x2.shape[0]
    return pl.pallas_call(
        _outproj_kernel,
        grid=(m // tm,),
        in_specs=[
            pl.BlockSpec((tm, HG_WIDTH), lambda i: (i, 0)),
            pl.BlockSpec((tm, HG_WIDTH), lambda i: (i, 0)),
            pl.BlockSpec((tm, HG_WIDTH), lambda i: (i, 1)),
            pl.BlockSpec((tm, CONV_CH), lambda i: (i, 0)),
            pl.BlockSpec((tm, D_MODEL), lambda i: (i, 0)),
            pl.BlockSpec((1, MOD_ROWS, D_MODEL), lambda i: (mod_row(i), 0, 0)),
            pl.BlockSpec((1, HEAD_DIM), lambda i: (0, 0)),
            pl.BlockSpec((1, D_MODEL), lambda i: (0, 0)),
            pl.BlockSpec((1, D_MODEL), lambda i: (0, 0)),
            pl.BlockSpec((None, D_MODEL, D_MODEL), lambda i: (layer, 0, 0), pipeline_mode=pl.Buffered(1)),
        ],
        out_specs=[
            pl.BlockSpec((tm, D_MODEL), lambda i: (i, 0)),
            pl.BlockSpec((tm, D_MODEL), lambda i: (i, 0)),
        ],
        out_shape=[
            jax.ShapeDtypeStruct((m, D_MODEL), F32),
            jax.ShapeDtypeStruct((m, D_MODEL), BF16),
        ],
        compiler_params=_cparams(("arbitrary",)),
        name="out_proj",
    )(of2, ob2, r2, cv2, x2, mod3, hgw, nw, nw2, w_bf)


ROW_BLOCK = 32
MXU_COLS = 256
MXU_ROWS = 256
FFN_CHUNK = 512
LANE_BLOCK = 256


def _gate_chunk(gp_ref, gm_ref, gn_ref, val_ref, cw_ref, cb_ref, act_out, first, last, mm_tile,
                mm_tiles, *, tm, grid_mode):
    tk = FFN_CHUNK
    rows = ROW_BLOCK
    lanes = LANE_BLOCK
    n_blocks = tm // rows
    total = n_blocks * (tk // lanes)
    row_id = lax.broadcasted_iota(jnp.int32, (rows, lanes), 0)
    edge = GRID_W if grid_mode else tm
    zero = jnp.zeros((), BF16)

    def gate_rows(r0, ls):
        if r0 < 0:
            return jnp.where(first, zero, gp_ref[GRID_W + r0:GRID_W + r0 + rows, ls])
        if r0 >= tm:
            return jnp.where(last, zero, gn_ref[r0 - tm:r0 - tm + rows, ls])
        return gm_ref[r0:r0 + rows, ls]

    done = 0
    for l0 in range(0, tk, lanes):
        ls = slice(l0, l0 + lanes)
        taps = (0.5 * cw_ref[:, :, ls]).astype(BF16)
        taps = [[jnp.broadcast_to(taps[dh, dw:dw + 1], (rows, lanes)) for dw in range(3)] for dh in range(3)]
        bias = 0.5 * cb_ref[:, ls]

        def columns(r0):
            ce = gate_rows(r0, ls)
            if not grid_mode:
                return [(taps[1][dw] * ce).astype(F32) for dw in range(3)]
            up = gate_rows(r0 - GRID_W, ls)
            dn = gate_rows(r0 + GRID_W, ls)
            return [(taps[0][dw] * up + taps[1][dw] * ce + taps[2][dw] * dn).astype(F32) for dw in range(3)]

        cur = columns(0)
        prev_left_row = None
        for blk in range(n_blocks):
            r0 = blk * rows
            for n in range(done * mm_tiles // total, (done + 1) * mm_tiles // total):
                mm_tile(n)
            done += 1
            nxt = columns(r0 + rows) if blk + 1 < n_blocks else None
            left, mid, right = cur
            from_prev = pltpu.roll(left, 1, axis=0)
            if r0 % edge == 0:
                from_prev = jnp.where(row_id == 0, 0.0, from_prev)
            else:
                from_prev = jnp.where(row_id == 0, prev_left_row, from_prev)
            from_next = pltpu.roll(right, rows - 1, axis=0)
            if (r0 + rows) % edge == 0:
                from_next = jnp.where(row_id == rows - 1, 0.0, from_next)
            else:
                from_next = jnp.where(row_id == rows - 1, nxt[2][0:1, :], from_next)
            half = mid + from_prev + from_next + bias
            gelu = half * (1.0 + lax.erf(half * np.float32(np.sqrt(2.0))))
            act_out[r0:r0 + rows, ls] = gelu.astype(BF16) * val_ref[r0:r0 + rows, ls]
            prev_left_row = left[rows - 1:rows, :]
            cur = nxt


def _ffn_down_kernel(gp_ref, gm_ref, gn_ref, val_ref, cw_ref, cb_ref, w_ref, x_ref, mod_ref, nw_ref, *rest,
                     tm, nk, tiles_per_seq, grid_mode, next_norm):
    if next_norm:
        modn_ref, nwn_ref, o_ref, h_ref, acc_ref, act0_ref, act1_ref = rest
    else:
        o_ref, acc_ref, act0_ref, act1_ref = rest
    s = pl.program_id(0)
    i = jnp.minimum(s, pl.num_programs(0) - 2) // nk
    km = jnp.maximum(s - 1, 0) % nk
    first = (i % tiles_per_seq) == 0
    last = (i % tiles_per_seq) == tiles_per_seq - 1

    @pl.when(s == 0)
    def _():
        act1_ref[...] = jnp.zeros_like(act1_ref)

    @pl.when(km == 0)
    def _():
        acc_ref[...] = jnp.zeros_like(acc_ref)

    def step(act_in, act_out):
        row_parts = tm // MXU_ROWS

        def mm_tile(n):
            rows = slice((n % row_parts) * MXU_ROWS, (n % row_parts + 1) * MXU_ROWS)
            cols = slice((n // row_parts) * MXU_COLS, (n // row_parts + 1) * MXU_COLS)
            acc_ref[rows, cols] += jnp.dot(act_in[rows, :], w_ref[:, cols], preferred_element_type=F32)

        _gate_chunk(gp_ref, gm_ref, gn_ref, val_ref, cw_ref, cb_ref, act_out, first, last, mm_tile,
                    row_parts * D_MODEL // MXU_COLS, tm=tm, grid_mode=grid_mode)

    @pl.when(s % 2 == 0)
    def _():
        step(act1_ref, act0_ref)

    @pl.when(s % 2 == 1)
    def _():
        step(act0_ref, act1_ref)

    @pl.when((km == nk - 1) & (s > 0))
    def _():
        x_new = x_ref[...] + mod_ref[0, 5:6, :] * (_rms_rows(acc_ref[...]) * nw_ref[...])
        o_ref[...] = x_new
        if next_norm:
            h_ref[...] = _prenorm(x_new, modn_ref, nwn_ref, 0)


def _ffn_down_call(gv2, cw, cb, w_bf, layer, x2, mod3, nw, tm, seq_len, mod_row, grid_mode, next_norm):
    m = x2.shape[0]
    tk = FFN_CHUNK
    nk = D_FF // tk
    hb = tm // GRID_W
    n_hblk = m // GRID_W
    tiles_per_seq = seq_len // tm
    n_steps = (m // tm) * nk + 1
    bi = lambda s: jnp.minimum(s, n_steps - 2) // nk
    bk = lambda s: jnp.minimum(s, n_steps - 2) % nk
    mi = lambda s: jnp.maximum(s - 1, 0) // nk
    mk = lambda s: jnp.maximum(s - 1, 0) % nk
    row_spec = pl.BlockSpec((tm, D_MODEL), lambda s: (mi(s), 0))
    mod_spec = pl.BlockSpec((1, MOD_ROWS, D_MODEL), lambda s: (mod_row(mi(s)), 0, 0))
    vec_spec = pl.BlockSpec((1, D_MODEL), lambda s: (0, 0))
    extra_in, extra_specs, out_specs = (), [], row_spec
    out_shape = jax.ShapeDtypeStruct((m, D_MODEL), F32)
    if next_norm:
        extra_in, extra_specs = tuple(next_norm), [mod_spec, vec_spec]
        out_specs = [row_spec, row_spec]
        out_shape = [out_shape, jax.ShapeDtypeStruct((m, D_MODEL), BF16)]
    return pl.pallas_call(
        functools.partial(_ffn_down_kernel, tm=tm, nk=nk, tiles_per_seq=tiles_per_seq, grid_mode=grid_mode,
                          next_norm=bool(next_norm)),
        grid=(n_steps,),
        in_specs=[
            pl.BlockSpec((GRID_W, tk), lambda s: (jnp.maximum(bi(s) * hb - 1, 0), bk(s))),
            pl.BlockSpec((tm, tk), lambda s: (bi(s), bk(s))),
            pl.BlockSpec((GRID_W, tk), lambda s: (jnp.minimum((bi(s) + 1) * hb, n_hblk - 1), bk(s))),
            pl.BlockSpec((tm, tk), lambda s: (bi(s), nk + bk(s))),
            pl.BlockSpec((3, 3, tk), lambda s: (0, 0, bk(s))),
            pl.BlockSpec((1, tk), lambda s: (0, bk(s))),
            pl.BlockSpec((None, tk, D_MODEL), lambda s: (layer, mk(s), 0)),
            row_spec,
            mod_spec,
            vec_spec,
        ] + extra_specs,
        out_specs=out_specs,
        out_shape=out_shape,
        scratch_shapes=[
            pltpu.VMEM((tm, D_MODEL), F32),
            pltpu.VMEM((tm, tk), BF16),
            pltpu.VMEM((tm, tk), BF16),
        ],
        compiler_params=_cparams(("arbitrary",)),
        name="ffn_down",
    )(gv2, gv2, gv2, gv2, cw, cb.reshape(1, -1), w_bf, x2, mod3, nw, *extra_in)


class _Tiles(NamedTuple):
    proj: int
    proj_f32: int
    norm: int
    out: int
    down: int
    scan: int
    cols: int


def _stream(x3, h2, mod, l, params, s0, consts, tiles, row_of, grid_mode, full, emit_next):
    bsz, length, _ = x3.shape
    m = bsz * length
    x2 = x3.reshape(m, D_MODEL)
    mod3 = mod[l]
    nw = params["norm_w"][l]
    tn = tiles.cols
    w_in = params["w_in"]
    if h2 is None:
        h2 = _prenorm_call(x2, mod3, nw[0:1], tiles.norm, row_of(tiles.norm), 0, "mix_prenorm")
    f2 = _proj_call(h2, w_in, l, HG_WIDTH, 2 * HG_WIDTH, tiles.proj_f32, tn, False, F32, "in_proj_f")
    if full:
        q2 = _proj_call(h2, w_in, l, 0, HG_WIDTH, tiles.proj, tn, True, BF16, "in_proj_q")
        r2 = _proj_call(h2, w_in, l, 3 * HG_WIDTH, IN_COLS - 3 * HG_WIDTH, tiles.proj, tn, False, BF16,
                        "in_proj_vgab")
    else:
        r2 = _proj_call(h2, w_in, l, 3 * HG_WIDTH, HG_WIDTH, tiles.proj, tn, False, BF16, "in_proj_v")
        q2 = r2
    of3, ob3, sfin = _scan_call(q2.reshape(bsz, length, -1), f2.reshape(bsz, length, -1),
                                r2.reshape(bsz, length, -1), params["lb_logits"], s0, consts, l,
                                tiles.scan)
    if not full:
        return None, None, sfin
    cv3 = _cconv_call(r2.reshape(bsz, length, -1), params["conv_w"][l], params["conv_b"][l],
                      params["conv_ln_w"][l], params["conv_ln_b"][l])
    x2, h2 = _outproj_call(of3.reshape(m, -1), ob3.reshape(m, -1), r2, cv3.reshape(m, -1), x2, mod3,
                           params["hg_norm_w"][l].reshape(1, -1), nw[1:2], nw[2:3], params["w_out"], l,
                           tiles.out, row_of(tiles.out))
    gv2 = _proj_call(h2, params["ffn_up"], l, 0, 2 * D_FF, tiles.proj, tn, False, BF16, "ffn_up")
    next_norm = (mod[l + 1], params["norm_w"][l + 1][0:1]) if emit_next else None
    out = _ffn_down_call(gv2, params["ffn_conv_w"][l], params["ffn_conv_b"][l], params["ffn_down"], l,
                         x2, mod3, nw[3:4], tiles.down, length, row_of(tiles.down), grid_mode, next_norm)
    x2, h_next = out if emit_next else (out, None)
    return x2.reshape(bsz, length, D_MODEL), h_next, sfin


def kernel(x, c, ctx, c_ctx, w_mod, b_mod, norm_w, w_in, lb_logits, hg_norm_w, conv_w, conv_b, conv_ln_w,
           conv_ln_b, w_out, ffn_up, ffn_conv_w, ffn_conv_b, ffn_down):
    bsz, seq, _ = x.shape
    ctx_len = ctx.shape[1]
    params = dict(norm_w=norm_w, w_in=w_in, lb_logits=lb_logits, hg_norm_w=hg_norm_w,
                  conv_w=conv_w, conv_b=conv_b, conv_ln_w=conv_ln_w, conv_ln_b=conv_ln_b,
                  w_out=w_out.astype(BF16), ffn_up=ffn_up, ffn_conv_w=ffn_conv_w,
                  ffn_conv_b=ffn_conv_b, ffn_down=ffn_down.astype(BF16))
    consts = jnp.asarray(_scan_constants(), jnp.int32)

    cond = jnp.concatenate([c, c_ctx[None, :], jnp.zeros((MOD_ROWS - bsz - 1, D_MODEL), c.dtype)], axis=0)
    mod = _mod_call(_silu(cond).astype(BF16), w_mod, b_mod)
    mod = mod[:, :bsz + 1].reshape(DEPTH, bsz + 1, N_MOD, D_MODEL)
    mod = jnp.pad(mod, ((0, 0), (0, 0), (0, MOD_ROWS - N_MOD), (0, 0)))

    lat_row = lambda tm: (lambda i: i // (seq // tm))
    ctx_row = lambda tm: (lambda i: bsz)
    zero_state = jnp.zeros((bsz, HEADS, 2, HEAD_DIM, HEAD_DIM), F32)
    ctx_rows = bsz * ctx_len
    ctx_tiles = _Tiles(proj=ctx_rows, proj_f32=ctx_rows, norm=ctx_len, out=ctx_len, down=ctx_len,
                       scan=ctx_len, cols=1024)
    lat_tiles = _Tiles(proj=2048, proj_f32=1024, norm=512, out=512, down=512, scan=2048, cols=1024)
    xc, hc, h = ctx, None, None
    for l in range(DEPTH):
        last = l == DEPTH - 1
        xc, hc, s_ctx = _stream(xc, hc, mod, l, params, zero_state, consts, ctx_tiles, ctx_row,
                                grid_mode=False, full=not last, emit_next=not last)
        x, h, _ = _stream(x, h, mod, l, params, s_ctx, consts, lat_tiles, lat_row,
                          grid_mode=True, full=True, emit_next=not last)
    return x
```

```python
import functools
from typing import NamedTuple

import numpy as np
import jax
import jax.numpy as jnp
from jax import lax
from jax.experimental import pallas as pl
from jax.experimental.pallas import tpu as pltpu

F32 = jnp.float32
BF16 = jnp.bfloat16

D_MODEL = 2048
DEPTH = 2
GRID_W = 64
HEADS = 8
HEAD_DIM = 128
HG_WIDTH = HEADS * HEAD_DIM
CONV_CH = D_MODEL - HG_WIDTH
CONV_WIDTH = 31
CONV_PAD = 16
CONV_ROWS = 128
D_FF = 5632
N_MOD = 6
MOD_ROWS = 8
EPS = 1e-6
LN_EPS = 1e-5
LOG2_E = float(np.log2(np.e))
IN_COLS = 3 * HG_WIDTH + 2 * HG_WIDTH + 2 * CONV_CH

CHUNK = 64
N_LEVELS = 6

V7X_VMEM_BYTES = 64 * 1024 * 1024
VMEM_LIMIT = V7X_VMEM_BYTES * 7 // 8


def _cparams(sem):
    return pltpu.CompilerParams(dimension_semantics=sem, vmem_limit_bytes=VMEM_LIMIT)


def _silu(x):
    return x * jax.nn.sigmoid(x)


def _rms_rows(x):
    return x * lax.rsqrt(jnp.mean(x * x, axis=-1, keepdims=True) + EPS)


def _mod_kernel(s_ref, w_ref, b_ref, o_ref):
    w = w_ref[0].astype(BF16)
    o_ref[0] = jnp.dot(s_ref[...], w, preferred_element_type=F32) + b_ref[0]


MOD_COLS = 1024


def _mod_call(s_rows, w_mod, b_mod):
    tn = MOD_COLS
    n = w_mod.shape[-1]
    return pl.pallas_call(
        _mod_kernel,
        grid=(DEPTH, n // tn),
        in_specs=[
            pl.BlockSpec((MOD_ROWS, D_MODEL), lambda l, j: (0, 0)),
            pl.BlockSpec((1, D_MODEL, tn), lambda l, j: (l, 0, j)),
            pl.BlockSpec((1, 1, tn), lambda l, j: (l, 0, j)),
        ],
        out_specs=pl.BlockSpec((1, MOD_ROWS, tn), lambda l, j: (l, 0, j)),
        out_shape=jax.ShapeDtypeStruct((DEPTH, MOD_ROWS, n), F32),
        compiler_params=_cparams(("arbitrary", "arbitrary")),
        name="mod_matmul",
    )(s_rows, w_mod, b_mod.reshape(DEPTH, 1, n))


def _prenorm(x, mod_ref, nw_ref, shift_row):
    y = _rms_rows(x) * nw_ref[...]
    h = y * (1.0 + mod_ref[0, shift_row + 1:shift_row + 2, :]) + mod_ref[0, shift_row:shift_row + 1, :]
    return h.astype(BF16)


def _prenorm_kernel(x_ref, mod_ref, nw_ref, h_ref, *, shift_row):
    h_ref[...] = _prenorm(x_ref[...], mod_ref, nw_ref, shift_row)


def _prenorm_call(x2, mod3, nw, tm, mod_row, shift_row, name):
    m = x2.shape[0]
    return pl.pallas_call(
        functools.partial(_prenorm_kernel, shift_row=shift_row),
        grid=(m // tm,),
        in_specs=[
            pl.BlockSpec((tm, D_MODEL), lambda i: (i, 0)),
            pl.BlockSpec((1, MOD_ROWS, D_MODEL), lambda i: (mod_row(i), 0, 0)),
            pl.BlockSpec((1, D_MODEL), lambda i: (0, 0)),
        ],
        out_specs=pl.BlockSpec((tm, D_MODEL), lambda i: (i, 0)),
        out_shape=jax.ShapeDtypeStruct((m, D_MODEL), BF16),
        compiler_params=_cparams(("arbitrary",)),
        name=name,
    )(x2, mod3, nw)


def _proj_kernel(h_ref, w_ref, o_ref, wb_ref, *, act):
    @pl.when(pl.program_id(1) == 0)
    def _():
        wb_ref[...] = w_ref[...].astype(BF16)

    acc = jnp.dot(h_ref[...], wb_ref[...], preferred_element_type=F32)
    if act:
        acc = _silu(acc)
    o_ref[...] = acc.astype(o_ref.dtype)


def _proj_call(h2, w, layer, col0, n_cols, tm, tn, act, out_dtype, name):
    m = h2.shape[0]
    c0 = col0 // tn
    return pl.pallas_call(
        functools.partial(_proj_kernel, act=act),
        grid=(n_cols // tn, m // tm),
        in_specs=[
            pl.BlockSpec((tm, D_MODEL), lambda j, i: (i, 0)),
            pl.BlockSpec((None, D_MODEL, tn), lambda j, i: (layer, 0, c0 + j)),
        ],
        out_specs=pl.BlockSpec((tm, tn), lambda j, i: (i, j)),
        out_shape=jax.ShapeDtypeStruct((m, n_cols), out_dtype),
        scratch_shapes=[pltpu.VMEM((D_MODEL, tn), BF16)],
        compiler_params=_cparams(("arbitrary", "arbitrary")),
        name=name,
    )(h2, w)


def _scan_constants():
    c = CHUNK
    t = np.arange(c)[:, None]
    r = np.arange(c)[None, :]
    level = np.full((2, c, c), -1, np.int32)
    for lvl in range(N_LEVELS):
        half = c >> (lvl + 1)
        same = (t // (2 * half)) == (r // (2 * half))
        t_late = (t % (2 * half)) >= half
        r_late = (r % (2 * half)) >= half
        level[0][same & t_late & ~r_late] = lvl
        level[1][same & ~t_late & r_late] = lvl
    level[:, np.arange(c), np.arange(c)] = N_LEVELS
    return level


_NT = (((1,), (1,)), ((), ()))
_TN = (((0,), (0,)), ((), ()))


def _neg_abs(x):
    bits = lax.bitcast_convert_type(x, jnp.int32) | jnp.int32(-2 ** 31)
    return lax.bitcast_convert_type(bits, F32)


def _scan_kernel(lbl_ref, level_ref, qf_ref, qb_ref, vf_ref, vb_ref, ff_ref, fb_ref,
                 s0_ref, of_ref, ob_ref, sfin_ref, st_ref, *, layer, n_chunks):
    c = CHUNK
    step = pl.program_id(2)

    @pl.when(step == 0)
    def _():
        st_ref[...] = s0_ref[0, 0]

    rows = [lbl_ref[:, j, :] for j in range(DEPTH)]
    mx = functools.reduce(jnp.maximum, rows)
    es = [jnp.exp(rw - mx) for rw in rows]
    tot = functools.reduce(lambda a, b: a + b, es)
    lb = jnp.zeros_like(mx)
    for j in range(1, layer + 1):
        lb = lb + es[j] / tot

    level = level_ref[...]
    sub = lax.broadcasted_iota(jnp.int32, (c // 8, 8, HEAD_DIM), 1)

    def ref_rows(b, half, d):
        pick = half - 1 if d == 0 else half
        if half >= 8:
            parts = []
            for p0 in range(0, c, 2 * half):
                parts.append(jnp.broadcast_to(b[p0 + pick:p0 + pick + 1, :], (2 * half, HEAD_DIM)))
            return parts[0] if len(parts) == 1 else jnp.concatenate(parts, axis=0)
        b3 = b.reshape(c // 8, 8, HEAD_DIM)
        out = None
        for g0 in range(0, 8, 2 * half):
            cand = jnp.broadcast_to(b3[:, g0 + pick:g0 + pick + 1, :], b3.shape)
            out = cand if out is None else jnp.where(sub >= g0, cand, out)
        return out.reshape(c, HEAD_DIM)

    refs = ((qf_ref, vf_ref, ff_ref, of_ref), (qb_ref, vb_ref, fb_ref, ob_ref))

    def gates(d, row0):
        raw = refs[d][2][0, row0:row0 + c, :]
        lbd = lb[d:d + 1, :]
        f = lbd + (1.0 - lbd) * jax.nn.sigmoid(raw)
        x = (jnp.log(f) * LOG2_E).reshape(c // 8, 8, HEAD_DIM)
        for s in (1, 2, 4):
            if d == 0:
                x = x + jnp.where(sub >= s, pltpu.roll(x, s, axis=1), 0.0)
            else:
                x = x + jnp.where(sub < 8 - s, pltpu.roll(x, 8 - s, axis=1), 0.0)
        groups = [x[g] for g in range(c // 8)]
        edge = 7 if d == 0 else 0
        totals = [grp[edge:edge + 1, :] for grp in groups]
        order = range(1, c // 8) if d == 0 else range(c // 8 - 2, -1, -1)
        carry = None
        for g in order:
            before = totals[g - 1] if d == 0 else totals[g + 1]
            carry = before if carry is None else carry + before
            groups[g] = groups[g] + carry
        return f, jnp.concatenate(groups, axis=0)

    none = jnp.full((c, c), -1, jnp.int32)
    level2 = jnp.concatenate([jnp.concatenate([level[0], none], axis=1),
                              jnp.concatenate([none, level[1]], axis=1)], axis=0)
    pad = jnp.zeros((c, HEAD_DIM), BF16)

    def diag2(fwd, bwd, fill):
        return jnp.concatenate([jnp.concatenate([fwd, fill], axis=1),
                                jnp.concatenate([fill, bwd], axis=1)], axis=0)

    def intra(rows, gated):
        q_bf, k_bf, b, own, pair = [], [], [], [], []
        for d in range(2):
            f, bd = gated[d]
            qd = refs[d][0][0, rows[d]:rows[d] + c, :]
            q = qd.astype(F32)
            key = 1.0 - f
            own.append(jnp.sum(q * key, axis=-1, keepdims=True))
            pair_key = pltpu.roll(key, 1 if d == 0 else c - 1, axis=0)
            pair.append(jnp.sum(q * f * pair_key, axis=-1, keepdims=True))
            q_bf.append(qd)
            k_bf.append(key.astype(BF16))
            b.append(bd)
        rows_of = lambda fwd, bwd: jnp.concatenate([jnp.broadcast_to(fwd, (c, 2 * c)),
                                                    jnp.broadcast_to(bwd, (c, 2 * c))], axis=0)
        att = jnp.where(level2 == N_LEVELS, rows_of(own[0], own[1]), 0.0)
        att = jnp.where(level2 == N_LEVELS - 1, rows_of(pair[0], pair[1]), att)
        for lvl in range(N_LEVELS - 1):
            half = c >> (lvl + 1)
            el = [jnp.exp2(_neg_abs(b[d] - ref_rows(b[d], half, d))).astype(BF16) for d in range(2)]
            lhs = diag2(q_bf[0] * el[0], q_bf[1] * el[1], pad)
            rhs = diag2(k_bf[0] * el[0], k_bf[1] * el[1], pad)
            att = jnp.where(level2 == lvl, lax.dot_general(lhs, rhs, _NT, preferred_element_type=F32), att)
        q_in, k_out, decay = [], [], []
        for d in range(2):
            last = c - 1 if d == 0 else 0
            blast = b[d][last:last + 1, :]
            q_in.append(q_bf[d] * jnp.exp2(b[d]).astype(BF16))
            k_out.append(k_bf[d] * jnp.exp2(blast - b[d]).astype(BF16))
            decay.append(jnp.transpose(jnp.broadcast_to(jnp.exp2(blast), (8, HEAD_DIM)))[:, 0:1])
        q_att = jnp.concatenate([diag2(q_in[0], q_in[1], pad), att.astype(BF16)], axis=1)
        return q_att, diag2(k_out[0], k_out[1], pad), jnp.concatenate(decay, axis=0)

    def readout(rows, q_att, k_out, decay, st):
        v = jnp.concatenate([refs[d][1][0, rows[d]:rows[d] + c, :] for d in range(2)], axis=0)
        o = jnp.dot(q_att, jnp.concatenate([st.astype(BF16), v], axis=0), preferred_element_type=F32)
        for d in range(2):
            refs[d][3][0, rows[d]:rows[d] + c, :] = o[d * c:(d + 1) * c]
        return st * decay + lax.dot_general(k_out, v, _TN, preferred_element_type=F32)

    work = [(ci * c, (n_chunks - 1 - ci) * c) for ci in range(n_chunks)]
    st = st_ref[...].reshape(2 * HEAD_DIM, HEAD_DIM)
    stage1 = {}
    stage2 = {}
    for n in range(len(work) + 2):
        if n >= 2:
            st = readout(work[n - 2], *stage2.pop(n - 2), st)
        if 1 <= n <= len(work):
            stage2[n - 1] = intra(work[n - 1], stage1.pop(n - 1))
        if n < len(work):
            stage1[n] = [gates(d, work[n][d]) for d in range(2)]
    st_ref[...] = st.reshape(2, HEAD_DIM, HEAD_DIM)

    @pl.when(step == pl.num_programs(2) - 1)
    def _():
        sfin_ref[0, 0] = st_ref[...]


def _scan_call(q3, f3, r3, lb_logits, s0, level, layer, tl):
    bsz, length, _ = q3.shape
    nl = length // tl
    fwd = lambda b, h, i: (b, i, h)
    bwd = lambda b, h, i: (b, nl - 1 - i, h)
    bwd_f = lambda b, h, i: (b, nl - 1 - i, HEADS + h)
    blk = (1, tl, HEAD_DIM)
    return pl.pallas_call(
        functools.partial(_scan_kernel, layer=layer, n_chunks=tl // CHUNK),
        grid=(bsz, HEADS, nl),
        in_specs=[
            pl.BlockSpec((2, DEPTH, HEAD_DIM), lambda b, h, i: (0, 0, h)),
            pl.BlockSpec(level.shape, lambda b, h, i: (0, 0, 0)),
            pl.BlockSpec(blk, fwd),
            pl.BlockSpec(blk, bwd),
            pl.BlockSpec(blk, fwd),
            pl.BlockSpec(blk, bwd),
            pl.BlockSpec(blk, fwd),
            pl.BlockSpec(blk, bwd_f),
            pl.BlockSpec((1, 1, 2, HEAD_DIM, HEAD_DIM), lambda b, h, i: (b, h, 0, 0, 0)),
        ],
        out_specs=[
            pl.BlockSpec(blk, fwd),
            pl.BlockSpec(blk, bwd),
            pl.BlockSpec((1, 1, 2, HEAD_DIM, HEAD_DIM), lambda b, h, i: (b, h, 0, 0, 0)),
        ],
        out_shape=[
            jax.ShapeDtypeStruct((bsz, length, HG_WIDTH), F32),
            jax.ShapeDtypeStruct((bsz, length, HG_WIDTH), F32),
            jax.ShapeDtypeStruct((bsz, HEADS, 2, HEAD_DIM, HEAD_DIM), F32),
        ],
        scratch_shapes=[pltpu.VMEM((2, HEAD_DIM, HEAD_DIM), F32)],
        compiler_params=_cparams(("arbitrary", "arbitrary", "arbitrary")),
        name="hgrn2_scan",
    )(lb_logits, level, q3, q3, r3, r3, f3, f3, s0)


def _cconv_kernel(a_ref, b_ref, cw_ref, cb_ref, lw_ref, lb_ref, o_ref, u_ref, *, length, rows):
    zeros = jnp.zeros((CONV_PAD, HEAD_DIM), F32)
    u_ref[0:CONV_PAD, :] = zeros
    u_ref[CONV_PAD + length:2 * CONV_PAD + length, :] = zeros
    n_tiles = length // rows

    def fill(ti, carry):
        r0 = pl.multiple_of(ti * rows, rows)
        a = a_ref[0, pl.ds(r0, rows), :].astype(F32)
        b = b_ref[0, pl.ds(r0, rows), :].astype(F32)
        u_ref[pl.ds(CONV_PAD + r0, rows), :] = a * jax.nn.sigmoid(b)
        return carry

    lax.fori_loop(0, n_tiles, fill, 0)

    half = CONV_WIDTH // 2

    def conv(ti, carry):
        r0 = pl.multiple_of(ti * rows, rows)
        lead = CONV_PAD - half
        span = -(-(lead + CONV_WIDTH - 1) // 8) * 8 - 8
        acc = jnp.zeros((rows, HEAD_DIM), F32)
        for phase in range(8):
            window = u_ref[pl.ds(r0 + phase, rows + span), :]
            for off in range(phase, lead + CONV_WIDTH, 8):
                if off >= lead:
                    j = off - lead
                    acc = acc + cw_ref[j:j + 1, :] * window[off - phase:off - phase + rows]
        acc = acc + cb_ref[...]
        mu = jnp.mean(acc, axis=-1, keepdims=True)
        dlt = acc - mu
        var = jnp.mean(dlt * dlt, axis=-1, keepdims=True)
        y = dlt * lax.rsqrt(var + LN_EPS) * lw_ref[...] + lb_ref[...]
        o_ref[0, pl.ds(r0, rows), :] = _silu(y).astype(BF16)
        return carry

    lax.fori_loop(0, n_tiles, conv, 0, unroll=4)


def _cconv_call(r3, cw, cb, lw, lb):
    bsz, length, _ = r3.shape
    groups = CONV_CH // HEAD_DIM
    a_col = 2 * HG_WIDTH // HEAD_DIM
    b_col = a_col + groups
    vec = pl.BlockSpec((1, HEAD_DIM), lambda b, g: (0, g))
    return pl.pallas_call(
        functools.partial(_cconv_kernel, length=length, rows=CONV_ROWS),
        grid=(bsz, groups),
        in_specs=[
            pl.BlockSpec((1, length, HEAD_DIM), lambda b, g: (b, 0, a_col + g)),
            pl.BlockSpec((1, length, HEAD_DIM), lambda b, g: (b, 0, b_col + g)),
            pl.BlockSpec((CONV_WIDTH, HEAD_DIM), lambda b, g: (0, g)),
            vec, vec, vec,
        ],
        out_specs=pl.BlockSpec((1, length, HEAD_DIM), lambda b, g: (b, 0, g)),
        out_shape=jax.ShapeDtypeStruct((bsz, length, CONV_CH), BF16),
        scratch_shapes=[pltpu.VMEM((length + 2 * CONV_PAD, HEAD_DIM), F32)],
        compiler_params=_cparams(("arbitrary", "arbitrary")),
        name="conformer_conv",
    )(r3, r3, cw, cb.reshape(1, -1), lw.reshape(1, -1), lb.reshape(1, -1))


def _outproj_kernel(of_ref, ob_ref, g_ref, cv_ref, x_ref, mod_ref, hgw_ref, nw_ref, nw2_ref, w_ref, o_ref,
                    h_ref):
    o = of_ref[...] + ob_ref[...]
    heads = []
    for h in range(HEADS):
        oh = o[:, h * HEAD_DIM:(h + 1) * HEAD_DIM]
        heads.append(_rms_rows(oh) * hgw_ref[...])
    r = jnp.concatenate(heads, axis=-1) * _silu(g_ref[...].astype(F32))
    y = jnp.dot(r.astype(BF16), w_ref[0:HG_WIDTH, :], preferred_element_type=F32)
    y = y + jnp.dot(cv_ref[...], w_ref[HG_WIDTH:D_MODEL, :], preferred_element_type=F32)
    x_new = x_ref[...] + mod_ref[0, 2:3, :] * (_rms_rows(y) * nw_ref[...])
    o_ref[...] = x_new
    h_ref[...] = _prenorm(x_new, mod_ref, nw2_ref, 3)


def _outproj_call(of2, ob2, r2, cv2, x2, mod3, hgw, nw, nw2, w_bf, layer, tm, mod_row):
    m = x2.shape[0]
    return pl.pallas_call(
        _outproj_kernel,
        grid=(m // tm,),
        in_specs=[
            pl.BlockSpec((tm, HG_WIDTH), lambda i: (i, 0)),
            pl.BlockSpec((tm, HG_WIDTH), lambda i: (i, 0)),
            pl.BlockSpec((tm, HG_WIDTH), lambda i: (i, 1)),
            pl.BlockSpec((tm, CONV_CH), lambda i: (i, 0)),
            pl.BlockSpec((tm, D_MODEL), lambda i: (i, 0)),
            pl.BlockSpec((1, MOD_ROWS, D_MODEL), lambda i: (mod_row(i), 0, 0)),
            pl.BlockSpec((1, HEAD_DIM), lambda i: (0, 0)),
            pl.BlockSpec((1, D_MODEL), lambda i: (0, 0)),
            pl.BlockSpec((1, D_MODEL), lambda i: (0, 0)),
            pl.BlockSpec((None, D_MODEL, D_MODEL), lambda i: (layer, 0, 0), pipeline_mode=pl.Buffered(1)),
        ],
        out_specs=[
            pl.BlockSpec((tm, D_MODEL), lambda i: (i, 0)),
            pl.BlockSpec((tm, D_MODEL), lambda i: (i, 0)),
        ],
        out_shape=[
            jax.ShapeDtypeStruct((m, D_MODEL), F32),
            jax.ShapeDtypeStruct((m, D_MODEL), BF16),
        ],
        compiler_params=_cparams(("arbitrary",)),
        name="out_proj",
    )(of2, ob2, r2, cv2, x2, mod3, hgw, nw, nw2, w_bf)


ROW_BLOCK = 32
MXU_COLS = 256
MXU_ROWS = 256
FFN_CHUNK = 512
LANE_BLOCK = 256


def _gate_chunk(gp_ref, gm_ref, gn_ref, val_ref, cw_ref, cb_ref, act_out, first, last, mm_tile,
                mm_tiles, *, tm, grid_mode):
    tk = FFN_CHUNK
    rows = ROW_BLOCK
    lanes = LANE_BLOCK
    n_blocks = tm // rows
    total = n_blocks * (tk // lanes)
    row_id = lax.broadcasted_iota(jnp.int32, (rows, lanes), 0)
    edge = GRID_W if grid_mode else tm
    zero = jnp.zeros((), BF16)

    def gate_rows(r0, ls):
        if r0 < 0:
            return jnp.where(first, zero, gp_ref[GRID_W + r0:GRID_W + r0 + rows, ls])
        if r0 >= tm:
            return jnp.where(last, zero, gn_ref[r0 - tm:r0 - tm + rows, ls])
        return gm_ref[r0:r0 + rows, ls]

    done = 0
    for l0 in range(0, tk, lanes):
        ls = slice(l0, l0 + lanes)
        taps = (0.5 * cw_ref[:, :, ls]).astype(BF16)
        taps = [[jnp.broadcast_to(taps[dh, dw:dw + 1], (rows, lanes)) for dw in range(3)] for dh in range(3)]
        bias = 0.5 * cb_ref[:, ls]

        def columns(r0):
            ce = gate_rows(r0, ls)
            if not grid_mode:
                return [(taps[1][dw] * ce).astype(F32) for dw in range(3)]
            up = gate_rows(r0 - GRID_W, ls)
            dn = gate_rows(r0 + GRID_W, ls)
            return [(taps[0][dw] * up + taps[1][dw] * ce + taps[2][dw] * dn).astype(F32) for dw in range(3)]

        cur = columns(0)
        prev_left_row = None
        for blk in range(n_blocks):
            r0 = blk * rows
            for n in range(done * mm_tiles // total, (done + 1) * mm_tiles // total):
                mm_tile(n)
            done += 1
            nxt = columns(r0 + rows) if blk + 1 < n_blocks else None
            left, mid, right = cur
            from_prev = pltpu.roll(left, 1, axis=0)
            if r0 % edge == 0:
                from_prev = jnp.where(row_id == 0, 0.0, from_prev)
            else:
                from_prev = jnp.where(row_id == 0, prev_left_row, from_prev)
            from_next = pltpu.roll(right, rows - 1, axis=0)
            if (r0 + rows) % edge == 0:
                from_next = jnp.where(row_id == rows - 1, 0.0, from_next)
            else:
                from_next = jnp.where(row_id == rows - 1, nxt[2][0:1, :], from_next)
            half = mid + from_prev + from_next + bias
            gelu = half * (1.0 + lax.erf(half * np.float32(np.sqrt(2.0))))
            act_out[r0:r0 + rows, ls] = gelu.astype(BF16) * val_ref[r0:r0 + rows, ls]
            prev_left_row = left[rows - 1:rows, :]
            cur = nxt


def _ffn_down_kernel(gp_ref, gm_ref, gn_ref, val_ref, cw_ref, cb_ref, w_ref, x_ref, mod_ref, nw_ref, *rest,
                     tm, nk, tiles_per_seq, grid_mode, next_norm):
    if next_norm:
        modn_ref, nwn_ref, o_ref, h_ref, acc_ref, act0_ref, act1_ref = rest
    else:
        o_ref, acc_ref, act0_ref, act1_ref = rest
    s = pl.program_id(0)
    i = jnp.minimum(s, pl.num_programs(0) - 2) // nk
    km = jnp.maximum(s - 1, 0) % nk
    first = (i % tiles_per_seq) == 0
    last = (i % tiles_per_seq) == tiles_per_seq - 1

    @pl.when(s == 0)
    def _():
        act1_ref[...] = jnp.zeros_like(act1_ref)

    @pl.when(km == 0)
    def _():
        acc_ref[...] = jnp.zeros_like(acc_ref)

    def step(act_in, act_out):
        row_parts = tm // MXU_ROWS

        def mm_tile(n):
            rows = slice((n % row_parts) * MXU_ROWS, (n % row_parts + 1) * MXU_ROWS)
            cols = slice((n // row_parts) * MXU_COLS, (n // row_parts + 1) * MXU_COLS)
            acc_ref[rows, cols] += jnp.dot(act_in[rows, :], w_ref[:, cols], preferred_element_type=F32)

        _gate_chunk(gp_ref, gm_ref, gn_ref, val_ref, cw_ref, cb_ref, act_out, first, last, mm_tile,
                    row_parts * D_MODEL // MXU_COLS, tm=tm, grid_mode=grid_mode)

    @pl.when(s % 2 == 0)
    def _():
        step(act1_ref, act0_ref)

    @pl.when(s % 2 == 1)
    def _():
        step(act0_ref, act1_ref)

    @pl.when((km == nk - 1) & (s > 0))
    def _():
        x_new = x_ref[...] + mod_ref[0, 5:6, :] * (_rms_rows(acc_ref[...]) * nw_ref[...])
        o_ref[...] = x_new
        if next_norm:
            h_ref[...] = _prenorm(x_new, modn_ref, nwn_ref, 0)


def _ffn_down_call(gv2, cw, cb, w_bf, layer, x2, mod3, nw, tm, seq_len, mod_row, grid_mode, next_norm):
    m = x2.shape[0]
    tk = FFN_CHUNK
    nk = D_FF // tk
    hb = tm // GRID_W
    n_hblk = m // GRID_W
    tiles_per_seq = seq_len // tm
    n_steps = (m // tm) * nk + 1
    bi = lambda s: jnp.minimum(s, n_steps - 2) // nk
    bk = lambda s: jnp.minimum(s, n_steps - 2) % nk
    mi = lambda s: jnp.maximum(s - 1, 0) // nk
    mk = lambda s: jnp.maximum(s - 1, 0) % nk
    row_spec = pl.BlockSpec((tm, D_MODEL), lambda s: (mi(s), 0))
    mod_spec = pl.BlockSpec((1, MOD_ROWS, D_MODEL), lambda s: (mod_row(mi(s)), 0, 0))
    vec_spec = pl.BlockSpec((1, D_MODEL), lambda s: (0, 0))
    extra_in, extra_specs, out_specs = (), [], row_spec
    out_shape = jax.ShapeDtypeStruct((m, D_MODEL), F32)
    if next_norm:
        extra_in, extra_specs = tuple(next_norm), [mod_spec, vec_spec]
        out_specs = [row_spec, row_spec]
        out_shape = [out_shape, jax.ShapeDtypeStruct((m, D_MODEL), BF16)]
    return pl.pallas_call(
        functools.partial(_ffn_down_kernel, tm=tm, nk=nk, tiles_per_seq=tiles_per_seq, grid_mode=grid_mode,
                          next_norm=bool(next_norm)),
        grid=(n_steps,),
        in_specs=[
            pl.BlockSpec((GRID_W, tk), lambda s: (jnp.maximum(bi(s) * hb - 1, 0), bk(s))),
            pl.BlockSpec((tm, tk), lambda s: (bi(s), bk(s))),
            pl.BlockSpec((GRID_W, tk), lambda s: (jnp.minimum((bi(s) + 1) * hb, n_hblk - 1), bk(s))),
            pl.BlockSpec((tm, tk), lambda s: (bi(s), nk + bk(s))),
            pl.BlockSpec((3, 3, tk), lambda s: (0, 0, bk(s))),
            pl.BlockSpec((1, tk), lambda s: (0, bk(s))),
            pl.BlockSpec((None, tk, D_MODEL), lambda s: (layer, mk(s), 0)),
            row_spec,
            mod_spec,
            vec_spec,
        ] + extra_specs,
        out_specs=out_specs,
        out_shape=out_shape,
        scratch_shapes=[
            pltpu.VMEM((tm, D_MODEL), F32),
            pltpu.VMEM((tm, tk), BF16),
            pltpu.VMEM((tm, tk), BF16),
        ],
        compiler_params=_cparams(("arbitrary",)),
        name="ffn_down",
    )(gv2, gv2, gv2, gv2, cw, cb.reshape(1, -1), w_bf, x2, mod3, nw, *extra_in)


class _Tiles(NamedTuple):
    proj: int
    proj_f32: int
    norm: int
    out: int
    down: int
    scan: int
    cols: int


def _stream(x3, h2, mod, l, params, s0, consts, tiles, row_of, grid_mode, full, emit_next):
    bsz, length, _ = x3.shape
    m = bsz * length
    x2 = x3.reshape(m, D_MODEL)
    mod3 = mod[l]
    nw = params["norm_w"][l]
    tn = tiles.cols
    w_in = params["w_in"]
    if h2 is None:
        h2 = _prenorm_call(x2, mod3, nw[0:1], tiles.norm, row_of(tiles.norm), 0, "mix_prenorm")
    f2 = _proj_call(h2, w_in, l, HG_WIDTH, 2 * HG_WIDTH, tiles.proj_f32, tn, False, F32, "in_proj_f")
    if full:
        q2 = _proj_call(h2, w_in, l, 0, HG_WIDTH, tiles.proj, tn, True, BF16, "in_proj_q")
        r2 = _proj_call(h2, w_in, l, 3 * HG_WIDTH, IN_COLS - 3 * HG_WIDTH, tiles.proj, tn, False, BF16,
                        "in_proj_vgab")
    else:
        r2 = _proj_call(h2, w_in, l, 3 * HG_WIDTH, HG_WIDTH, tiles.proj, tn, False, BF16, "in_proj_v")
        q2 = r2
    of3, ob3, sfin = _scan_call(q2.reshape(bsz, length, -1), f2.reshape(bsz, length, -1),
                                r2.reshape(bsz, length, -1), params["lb_logits"], s0, consts, l,
                                tiles.scan)
    if not full:
        return None, None, sfin
    cv3 = _cconv_call(r2.reshape(bsz, length, -1), params["conv_w"][l], params["conv_b"][l],
                      params["conv_ln_w"][l], params["conv_ln_b"][l])
    x2, h2 = _outproj_call(of3.reshape(m, -1), ob3.reshape(m, -1), r2, cv3.reshape(m, -1), x2, mod3,
                           params["hg_norm_w"][l].reshape(1, -1), nw[1:2], nw[2:3], params["w_out"], l,
                           tiles.out, row_of(tiles.out))
    gv2 = _proj_call(h2, params["ffn_up"], l, 0, 2 * D_FF, tiles.proj, tn, False, BF16, "ffn_up")
    next_norm = (mod[l + 1], params["norm_w"][l + 1][0:1]) if emit_next else None
    out = _ffn_down_call(gv2, params["ffn_conv_w"][l], params["ffn_conv_b"][l], params["ffn_down"], l,
                         x2, mod3, nw[3:4], tiles.down, length, row_of(tiles.down), grid_mode, next_norm)
    x2, h_next = out if emit_next else (out, None)
    return x2.reshape(bsz, length, D_MODEL), h_next, sfin


def kernel(x, c, ctx, c_ctx, w_mod, b_mod, norm_w, w_in, lb_logits, hg_norm_w, conv_w, conv_b, conv_ln_w,
           conv_ln_b, w_out, ffn_up, ffn_conv_w, ffn_conv_b, ffn_down):
    bsz, seq, _ = x.shape
    ctx_len = ctx.shape[1]
    params = dict(norm_w=norm_w, w_in=w_in, lb_logits=lb_logits, hg_norm_w=hg_norm_w,
                  conv_w=conv_w, conv_b=conv_b, conv_ln_w=conv_ln_w, conv_ln_b=conv_ln_b,
                  w_out=w_out.astype(BF16), ffn_up=ffn_up, ffn_conv_w=ffn_conv_w,
                  ffn_conv_b=ffn_conv_b, ffn_down=ffn_down.astype(BF16))
    consts = jnp.asarray(_scan_constants(), jnp.int32)

    cond = jnp.concatenate([c, c_ctx[None, :], jnp.zeros((MOD_ROWS - bsz - 1, D_MODEL), c.dtype)], axis=0)
    mod = _mod_call(_silu(cond).astype(BF16), w_mod, b_mod)
    mod = mod[:, :bsz + 1].reshape(DEPTH, bsz + 1, N_MOD, D_MODEL)
    mod = jnp.pad(mod, ((0, 0), (0, 0), (0, MOD_ROWS - N_MOD), (0, 0)))

    lat_row = lambda tm: (lambda i: i // (seq // tm))
    ctx_row = lambda tm: (lambda i: bsz)
    zero_state = jnp.zeros((bsz, HEADS, 2, HEAD_DIM, HEAD_DIM), F32)
    ctx_rows = bsz * ctx_len
    ctx_tiles = _Tiles(proj=ctx_rows, proj_f32=ctx_rows, norm=ctx_len, out=ctx_len, down=ctx_len,
                       scan=ctx_len, cols=1024)
    lat_tiles = _Tiles(proj=2048, proj_f32=1024, norm=512, out=512, down=512, scan=2048, cols=1024)
    xc, hc, h = ctx, None, None
    for l in range(DEPTH):
        last = l == DEPTH - 1
        xc, hc, s_ctx = _stream(xc, hc, mod, l, params, zero_state, consts, ctx_tiles, ctx_row,
                                grid_mode=False, full=not last, emit_next=not last)
        x, h, _ = _stream(x, h, mod, l, params, s_ctx, consts, lat_tiles, lat_row,
                          grid_mode=True, full=True, emit_next=not last)
    return x
```

```python
import functools
from typing import NamedTuple

import numpy as np
import jax
import jax.numpy as jnp
from jax import lax
from jax.experimental import pallas as pl
from jax.experimental.pallas import tpu as pltpu

F32 = jnp.float32
BF16 = jnp.bfloat16

D_MODEL = 2048
DEPTH = 2
GRID_W = 64
HEADS = 8
HEAD_DIM = 128
HG_WIDTH = HEADS * HEAD_DIM
CONV_CH = D_MODEL - HG_WIDTH
CONV_WIDTH = 31
CONV_PAD = 16
CONV_ROWS = 128
D_FF = 5632
N_MOD = 6
MOD_ROWS = 8
EPS = 1e-6
LN_EPS = 1e-5
LOG2_E = float(np.log2(np.e))
IN_COLS = 3 * HG_WIDTH + 2 * HG_WIDTH + 2 * CONV_CH

CHUNK = 64
N_LEVELS = 6

V7X_VMEM_BYTES = 64 * 1024 * 1024
VMEM_LIMIT = V7X_VMEM_BYTES * 7 // 8


def _cparams(sem):
    return pltpu.CompilerParams(dimension_semantics=sem, vmem_limit_bytes=VMEM_LIMIT)


def _silu(x):
    return x * jax.nn.sigmoid(x)


def _rms_rows(x):
    return x * lax.rsqrt(jnp.mean(x * x, axis=-1, keepdims=True) + EPS)


def _mod_kernel(s_ref, w_ref, b_ref, o_ref):
    w = w_ref[0].astype(BF16)
    o_ref[0] = jnp.dot(s_ref[...], w, preferred_element_type=F32) + b_ref[0]


MOD_COLS = 1024


def _mod_call(s_rows, w_mod, b_mod):
    tn = MOD_COLS
    n = w_mod.shape[-1]
    return pl.pallas_call(
        _mod_kernel,
        grid=(DEPTH, n // tn),
        in_specs=[
            pl.BlockSpec((MOD_ROWS, D_MODEL), lambda l, j: (0, 0)),
            pl.BlockSpec((1, D_MODEL, tn), lambda l, j: (l, 0, j)),
            pl.BlockSpec((1, 1, tn), lambda l, j: (l, 0, j)),
        ],
        out_specs=pl.BlockSpec((1, MOD_ROWS, tn), lambda l, j: (l, 0, j)),
        out_shape=jax.ShapeDtypeStruct((DEPTH, MOD_ROWS, n), F32),
        compiler_params=_cparams(("arbitrary", "arbitrary")),
        name="mod_matmul",
    )(s_rows, w_mod, b_mod.reshape(DEPTH, 1, n))


def _prenorm(x, mod_ref, nw_ref, shift_row):
    y = _rms_rows(x) * nw_ref[...]
    h = y * (1.0 + mod_ref[0, shift_row + 1:shift_row + 2, :]) + mod_ref[0, shift_row:shift_row + 1, :]
    return h.astype(BF16)


def _prenorm_kernel(x_ref, mod_ref, nw_ref, h_ref, *, shift_row):
    h_ref[...] = _prenorm(x_ref[...], mod_ref, nw_ref, shift_row)


def _prenorm_call(x2, mod3, nw, tm, mod_row, shift_row, name):
    m = x2.shape[0]
    return pl.pallas_call(
        functools.partial(_prenorm_kernel, shift_row=shift_row),
        grid=(m // tm,),
        in_specs=[
            pl.BlockSpec((tm, D_MODEL), lambda i: (i, 0)),
            pl.BlockSpec((1, MOD_ROWS, D_MODEL), lambda i: (mod_row(i), 0, 0)),
            pl.BlockSpec((1, D_MODEL), lambda i: (0, 0)),
        ],
        out_specs=pl.BlockSpec((tm, D_MODEL), lambda i: (i, 0)),
        out_shape=jax.ShapeDtypeStruct((m, D_MODEL), BF16),
        compiler_params=_cparams(("arbitrary",)),
        name=name,
    )(x2, mod3, nw)


def _proj_kernel(h_ref, w_ref, o_ref, wb_ref, *, act):
    @pl.when(pl.program_id(1) == 0)
    def _():
        wb_ref[...] = w_ref[...].astype(BF16)

    acc = jnp.dot(h_ref[...], wb_ref[...], preferred_element_type=F32)
    if act:
        acc = _silu(acc)
    o_ref[...] = acc.astype(o_ref.dtype)


def _proj_call(h2, w, layer, col0, n_cols, tm, tn, act, out_dtype, name):
    m = h2.shape[0]
    c0 = col0 // tn
    return pl.pallas_call(
        functools.partial(_proj_kernel, act=act),
        grid=(n_cols // tn, m // tm),
        in_specs=[
            pl.BlockSpec((tm, D_MODEL), lambda j, i: (i, 0)),
            pl.BlockSpec((None, D_MODEL, tn), lambda j, i: (layer, 0, c0 + j)),
        ],
        out_specs=pl.BlockSpec((tm, tn), lambda j, i: (i, j)),
        out_shape=jax.ShapeDtypeStruct((m, n_cols), out_dtype),
        scratch_shapes=[pltpu.VMEM((D_MODEL, tn), BF16)],
        compiler_params=_cparams(("arbitrary", "arbitrary")),
        name=name,
    )(h2, w)


def _scan_constants():
    c = CHUNK
    t = np.arange(c)[:, None]
    r = np.arange(c)[None, :]
    level = np.full((2, c, c), -1, np.int32)
    for lvl in range(N_LEVELS):
        half = c >> (lvl + 1)
        same = (t // (2 * half)) == (r // (2 * half))
        t_late = (t % (2 * half)) >= half
        r_late = (r % (2 * half)) >= half
        level[0][same & t_late & ~r_late] = lvl
        level[1][same & ~t_late & r_late] = lvl
    level[:, np.arange(c), np.arange(c)] = N_LEVELS
    return level


_NT = (((1,), (1,)), ((), ()))
_TN = (((0,), (0,)), ((), ()))


def _neg_abs(x):
    bits = lax.bitcast_convert_type(x, jnp.int32) | jnp.int32(-2 ** 31)
    return lax.bitcast_convert_type(bits, F32)


def _scan_kernel(lbl_ref, level_ref, qf_ref, qb_ref, vf_ref, vb_ref, ff_ref, fb_ref,
                 s0_ref, of_ref, ob_ref, sfin_ref, st_ref, *, layer, n_chunks):
    c = CHUNK
    step = pl.program_id(2)

    @pl.when(step == 0)
    def _():
        st_ref[...] = s0_ref[0, 0]

    rows = [lbl_ref[:, j, :] for j in range(DEPTH)]
    mx = functools.reduce(jnp.maximum, rows)
    es = [jnp.exp(rw - mx) for rw in rows]
    tot = functools.reduce(lambda a, b: a + b, es)
    lb = jnp.zeros_like(mx)
    for j in range(1, layer + 1):
        lb = lb + es[j] / tot

    level = level_ref[...]
    sub = lax.broadcasted_iota(jnp.int32, (c // 8, 8, HEAD_DIM), 1)

    def ref_rows(b, half, d):
        pick = half - 1 if d == 0 else half
        if half >= 8:
            parts = []
            for p0 in range(0, c, 2 * half):
                parts.append(jnp.broadcast_to(b[p0 + pick:p0 + pick + 1, :], (2 * half, HEAD_DIM)))
            return parts[0] if len(parts) == 1 else jnp.concatenate(parts, axis=0)
        b3 = b.reshape(c // 8, 8, HEAD_DIM)
        out = None
        for g0 in range(0, 8, 2 * half):
            cand = jnp.broadcast_to(b3[:, g0 + pick:g0 + pick + 1, :], b3.shape)
            out = cand if out is None else jnp.where(sub >= g0, cand, out)
        return out.reshape(c, HEAD_DIM)

    refs = ((qf_ref, vf_ref, ff_ref, of_ref), (qb_ref, vb_ref, fb_ref, ob_ref))

    def gates(d, row0):
        raw = refs[d][2][0, row0:row0 + c, :]
        lbd = lb[d:d + 1, :]
        f = lbd + (1.0 - lbd) * jax.nn.sigmoid(raw)
        x = (jnp.log(f) * LOG2_E).reshape(c // 8, 8, HEAD_DIM)
        for s in (1, 2, 4):
            if d == 0:
                x = x + jnp.where(sub >= s, pltpu.roll(x, s, axis=1), 0.0)
            else:
                x = x + jnp.where(sub < 8 - s, pltpu.roll(x, 8 - s, axis=1), 0.0)
        groups = [x[g] for g in range(c // 8)]
        edge = 7 if d == 0 else 0
        totals = [grp[edge:edge + 1, :] for grp in groups]
        order = range(1, c // 8) if d == 0 else range(c // 8 - 2, -1, -1)
        carry = None
        for g in order:
            before = totals[g - 1] if d == 0 else totals[g + 1]
            carry = before if carry is None else carry + before
            groups[g] = groups[g] + carry
        return f, jnp.concatenate(groups, axis=0)

    none = jnp.full((c, c), -1, jnp.int32)
    level2 = jnp.concatenate([jnp.concatenate([level[0], none], axis=1),
                              jnp.concatenate([none, level[1]], axis=1)], axis=0)
    pad = jnp.zeros((c, HEAD_DIM), BF16)

    def diag2(fwd, bwd, fill):
        return jnp.concatenate([jnp.concatenate([fwd, fill], axis=1),
                                jnp.concatenate([fill, bwd], axis=1)], axis=0)

    def intra(rows, gated):
        q_bf, k_bf, b, own, pair = [], [], [], [], []
        for d in range(2):
            f, bd = gated[d]
            qd = refs[d][0][0, rows[d]:rows[d] + c, :]
            q = qd.astype(F32)
            key = 1.0 - f
            own.append(jnp.sum(q * key, axis=-1, keepdims=True))
            pair_key = pltpu.roll(key, 1 if d == 0 else c - 1, axis=0)
            pair.append(jnp.sum(q * f * pair_key, axis=-1, keepdims=True))
            q_bf.append(qd)
            k_bf.append(key.astype(BF16))
            b.append(bd)
        rows_of = lambda fwd, bwd: jnp.concatenate([jnp.broadcast_to(fwd, (c, 2 * c)),
                                                    jnp.broadcast_to(bwd, (c, 2 * c))], axis=0)
        att = jnp.where(level2 == N_LEVELS, rows_of(own[0], own[1]), 0.0)
        att = jnp.where(level2 == N_LEVELS - 1, rows_of(pair[0], pair[1]), att)
        for lvl in range(N_LEVELS - 1):
            half = c >> (lvl + 1)
            el = [jnp.exp2(_neg_abs(b[d] - ref_rows(b[d], half, d))).astype(BF16) for d in range(2)]
            lhs = diag2(q_bf[0] * el[0], q_bf[1] * el[1], pad)
            rhs = diag2(k_bf[0] * el[0], k_bf[1] * el[1], pad)
            att = jnp.where(level2 == lvl, lax.dot_general(lhs, rhs, _NT, preferred_element_type=F32), att)
        q_in, k_out, decay = [], [], []
        for d in range(2):
            last = c - 1 if d == 0 else 0
            blast = b[d][last:last + 1, :]
            q_in.append(q_bf[d] * jnp.exp2(b[d]).astype(BF16))
            k_out.append(k_bf[d] * jnp.exp2(blast - b[d]).astype(BF16))
            decay.append(jnp.transpose(jnp.broadcast_to(jnp.exp2(blast), (8, HEAD_DIM)))[:, 0:1])
        q_att = jnp.concatenate([diag2(q_in[0], q_in[1], pad), att.astype(BF16)], axis=1)
        return q_att, diag2(k_out[0], k_out[1], pad), jnp.concatenate(decay, axis=0)

    def readout(rows, q_att, k_out, decay, st):
        v = jnp.concatenate([refs[d][1][0, rows[d]:rows[d] + c, :] for d in range(2)], axis=0)
        o = jnp.dot(q_att, jnp.concatenate([st.astype(BF16), v], axis=0), preferred_element_type=F32)
        for d in range(2):
            refs[d][3][0, rows[d]:rows[d] + c, :] = o[d * c:(d + 1) * c]
        return st * decay + lax.dot_general(k_out, v, _TN, preferred_element_type=F32)

    work = [(ci * c, (n_chunks - 1 - ci) * c) for ci in range(n_chunks)]
    st = st_ref[...].reshape(2 * HEAD_DIM, HEAD_DIM)
    stage1 = {}
    stage2 = {}
    for n in range(len(work) + 2):
        if n >= 2:
            st = readout(work[n - 2], *stage2.pop(n - 2), st)
        if 1 <= n <= len(work):
            stage2[n - 1] = intra(work[n - 1], stage1.pop(n - 1))
        if n < len(work):
            stage1[n] = [gates(d, work[n][d]) for d in range(2)]
    st_ref[...] = st.reshape(2, HEAD_DIM, HEAD_DIM)

    @pl.when(step == pl.num_programs(2) - 1)
    def _():
        sfin_ref[0, 0] = st_ref[...]


def _scan_call(q3, f3, r3, lb_logits, s0, level, layer, tl):
    bsz, length, _ = q3.shape
    nl = length // tl
    fwd = lambda b, h, i: (b, i, h)
    bwd = lambda b, h, i: (b, nl - 1 - i, h)
    bwd_f = lambda b, h, i: (b, nl - 1 - i, HEADS + h)
    blk = (1, tl, HEAD_DIM)
    return pl.pallas_call(
        functools.partial(_scan_kernel, layer=layer, n_chunks=tl // CHUNK),
        grid=(bsz, HEADS, nl),
        in_specs=[
            pl.BlockSpec((2, DEPTH, HEAD_DIM), lambda b, h, i: (0, 0, h)),
            pl.BlockSpec(level.shape, lambda b, h, i: (0, 0, 0)),
            pl.BlockSpec(blk, fwd),
            pl.BlockSpec(blk, bwd),
            pl.BlockSpec(blk, fwd),
            pl.BlockSpec(blk, bwd),
            pl.BlockSpec(blk, fwd),
            pl.BlockSpec(blk, bwd_f),
            pl.BlockSpec((1, 1, 2, HEAD_DIM, HEAD_DIM), lambda b, h, i: (b, h, 0, 0, 0)),
        ],
        out_specs=[
            pl.BlockSpec(blk, fwd),
            pl.BlockSpec(blk, bwd),
            pl.BlockSpec((1, 1, 2, HEAD_DIM, HEAD_DIM), lambda b, h, i: (b, h, 0, 0, 0)),
        ],
        out_shape=[
            jax.ShapeDtypeStruct((bsz, length, HG_WIDTH), F32),
            jax.ShapeDtypeStruct((bsz, length, HG_WIDTH), F32),
            jax.ShapeDtypeStruct((bsz, HEADS, 2, HEAD_DIM, HEAD_DIM), F32),
        ],
        scratch_shapes=[pltpu.VMEM((2, HEAD_DIM, HEAD_DIM), F32)],
        compiler_params=_cparams(("arbitrary", "arbitrary", "arbitrary")),
        name="hgrn2_scan",
    )(lb_logits, level, q3, q3, r3, r3, f3, f3, s0)


def _cconv_kernel(a_ref, b_ref, cw_ref, cb_ref, lw_ref, lb_ref, o_ref, u_ref, *, length, rows):
    zeros = jnp.zeros((CONV_PAD, HEAD_DIM), F32)
    u_ref[0:CONV_PAD, :] = zeros
    u_ref[CONV_PAD + length:2 * CONV_PAD + length, :] = zeros
    n_tiles = length // rows

    def fill(ti, carry):
        r0 = pl.multiple_of(ti * rows, rows)
        a = a_ref[0, pl.ds(r0, rows), :].astype(F32)
        b = b_ref[0, pl.ds(r0, rows), :].astype(F32)
        u_ref[pl.ds(CONV_PAD + r0, rows), :] = a * jax.nn.sigmoid(b)
        return carry

    lax.fori_loop(0, n_tiles, fill, 0)

    half = CONV_WIDTH // 2

    def conv(ti, carry):
        r0 = pl.multiple_of(ti * rows, rows)
        lead = CONV_PAD - half
        span = -(-(lead + CONV_WIDTH - 1) // 8) * 8 - 8
        acc = jnp.zeros((rows, HEAD_DIM), F32)
        for phase in range(8):
            window = u_ref[pl.ds(r0 + phase, rows + span), :]
            for off in range(phase, lead + CONV_WIDTH, 8):
                if off >= lead:
                    j = off - lead
                    acc = acc + cw_ref[j:j + 1, :] * window[off - phase:off - phase + rows]
        acc = acc + cb_ref[...]
        mu = jnp.mean(acc, axis=-1, keepdims=True)
        dlt = acc - mu
        var = jnp.mean(dlt * dlt, axis=-1, keepdims=True)
        y = dlt * lax.rsqrt(var + LN_EPS) * lw_ref[...] + lb_ref[...]
        o_ref[0, pl.ds(r0, rows), :] = _silu(y).astype(BF16)
        return carry

    lax.fori_loop(0, n_tiles, conv, 0, unroll=4)


def _cconv_call(r3, cw, cb, lw, lb):
    bsz, length, _ = r3.shape
    groups = CONV_CH // HEAD_DIM
    a_col = 2 * HG_WIDTH // HEAD_DIM
    b_col = a_col + groups
    vec = pl.BlockSpec((1, HEAD_DIM), lambda b, g: (0, g))
    return pl.pallas_call(
        functools.partial(_cconv_kernel, length=length, rows=CONV_ROWS),
        grid=(bsz, groups),
        in_specs=[
            pl.BlockSpec((1, length, HEAD_DIM), lambda b, g: (b, 0, a_col + g)),
            pl.BlockSpec((1, length, HEAD_DIM), lambda b, g: (b, 0, b_col + g)),
            pl.BlockSpec((CONV_WIDTH, HEAD_DIM), lambda b, g: (0, g)),
            vec, vec, vec,
        ],
        out_specs=pl.BlockSpec((1, length, HEAD_DIM), lambda b, g: (b, 0, g)),
        out_shape=jax.ShapeDtypeStruct((bsz, length, CONV_CH), BF16),
        scratch_shapes=[pltpu.VMEM((length + 2 * CONV_PAD, HEAD_DIM), F32)],
        compiler_params=_cparams(("arbitrary", "arbitrary")),
        name="conformer_conv",
    )(r3, r3, cw, cb.reshape(1, -1), lw.reshape(1, -1), lb.reshape(1, -1))


def _outproj_kernel(of_ref, ob_ref, g_ref, cv_ref, x_ref, mod_ref, hgw_ref, nw_ref, nw2_ref, w_ref, o_ref,
                    h_ref):
    o = of_ref[...] + ob_ref[...]
    heads = []
    for h in range(HEADS):
        oh = o[:, h * HEAD_DIM:(h + 1) * HEAD_DIM]
        heads.append(_rms_rows(oh) * hgw_ref[...])
    r = jnp.concatenate(heads, axis=-1) * _silu(g_ref[...].astype(F32))
    y = jnp.dot(r.astype(BF16), w_ref[0:HG_WIDTH, :], preferred_element_type=F32)
    y = y + jnp.dot(cv_ref[...], w_ref[HG_WIDTH:D_MODEL, :], preferred_element_type=F32)
    x_new = x_ref[...] + mod_ref[0, 2:3, :] * (_rms_rows(y) * nw_ref[...])
    o_ref[...] = x_new
    h_ref[...] = _prenorm(x_new, mod_ref, nw2_ref, 3)


def _outproj_call(of2, ob2, r2, cv2, x2, mod3, hgw, nw, nw2, w_bf, layer, tm, mod_row):
    m = x2.shape[0]
    return pl.pallas_call(
        _outproj_kernel,
        grid=(m // tm,),
        in_specs=[
            pl.BlockSpec((tm, HG_WIDTH), lambda i: (i, 0)),
            pl.BlockSpec((tm, HG_WIDTH), lambda i: (i, 0)),
            pl.BlockSpec((tm, HG_WIDTH), lambda i: (i, 1)),
            pl.BlockSpec((tm, CONV_CH), lambda i: (i, 0)),
            pl.BlockSpec((tm, D_MODEL), lambda i: (i, 0)),
            pl.BlockSpec((1, MOD_ROWS, D_MODEL), lambda i: (mod_row(i), 0, 0)),
            pl.BlockSpec((1, HEAD_DIM), lambda i: (0, 0)),
            pl.BlockSpec((1, D_MODEL), lambda i: (0, 0)),
            pl.BlockSpec((1, D_MODEL), lambda i: (0, 0)),
            pl.BlockSpec((None, D_MODEL, D_MODEL), lambda i: (layer, 0, 0), pipeline_mode=pl.Buffered(1)),
        ],
        out_specs=[
            pl.BlockSpec((tm, D_MODEL), lambda i: (i, 0)),
            pl.BlockSpec((tm, D_MODEL), lambda i: (i, 0)),
        ],
        out_shape=[
            jax.ShapeDtypeStruct((m, D_MODEL), F32),
            jax.ShapeDtypeStruct((m, D_MODEL), BF16),
        ],
        compiler_params=_cparams(("arbitrary",)),
        name="out_proj",
    )(of2, ob2, r2, cv2, x2, mod3, hgw, nw, nw2, w_bf)


ROW_BLOCK = 32
MXU_COLS = 256
MXU_ROWS = 256
FFN_CHUNK = 512
LANE_BLOCK = 256


def _gate_chunk(gp_ref, gm_ref, gn_ref, val_ref, cw_ref, cb_ref, act_out, first, last, mm_tile,
                mm_tiles, *, tm, grid_mode):
    tk = FFN_CHUNK
    rows = ROW_BLOCK
    lanes = LANE_BLOCK
    n_blocks = tm // rows
    total = n_blocks * (tk // lanes)
    row_id = lax.broadcasted_iota(jnp.int32, (rows, lanes), 0)
    edge = GRID_W if grid_mode else tm
    zero = jnp.zeros((), BF16)

    def gate_rows(r0, ls):
        if r0 < 0:
            return jnp.where(first, zero, gp_ref[GRID_W + r0:GRID_W + r0 + rows, ls])
        if r0 >= tm:
            return jnp.where(last, zero, gn_ref[r0 - tm:r0 - tm + rows, ls])
        return gm_ref[r0:r0 + rows, ls]

    done = 0
    for l0 in range(0, tk, lanes):
        ls = slice(l0, l0 + lanes)
        taps = (0.5 * cw_ref[:, :, ls]).astype(BF16)
        taps = [[jnp.broadcast_to(taps[dh, dw:dw + 1], (rows, lanes)) for dw in range(3)] for dh in range(3)]
        bias = 0.5 * cb_ref[:, ls]

        def columns(r0):
            ce = gate_rows(r0, ls)
            if not grid_mode:
                return [(taps[1][dw] * ce).astype(F32) for dw in range(3)]
            up = gate_rows(r0 - GRID_W, ls)
            dn = gate_rows(r0 + GRID_W, ls)
            return [(taps[0][dw] * up + taps[1][dw] * ce + taps[2][dw] * dn).astype(F32) for dw in range(3)]

        cur = columns(0)
        prev_left_row = None
        for blk in range(n_blocks):
            r0 = blk * rows
            for n in range(done * mm_tiles // total, (done + 1) * mm_tiles // total):
                mm_tile(n)
            done += 1
            nxt = columns(r0 + rows) if blk + 1 < n_blocks else None
            left, mid, right = cur
            from_prev = pltpu.roll(left, 1, axis=0)
            if r0 % edge == 0:
                from_prev = jnp.where(row_id == 0, 0.0, from_prev)
            else:
                from_prev = jnp.where(row_id == 0, prev_left_row, from_prev)
            from_next = pltpu.roll(right, rows - 1, axis=0)
            if (r0 + rows) % edge == 0:
                from_next = jnp.where(row_id == rows - 1, 0.0, from_next)
            else:
                from_next = jnp.where(row_id == rows - 1, nxt[2][0:1, :], from_next)
            half = mid + from_prev + from_next + bias
            gelu = half * (1.0 + lax.erf(half * np.float32(np.sqrt(2.0))))
            act_out[r0:r0 + rows, ls] = gelu.astype(BF16) * val_ref[r0:r0 + rows, ls]
            prev_left_row = left[rows - 1:rows, :]
            cur = nxt


def _ffn_down_kernel(gp_ref, gm_ref, gn_ref, val_ref, cw_ref, cb_ref, w_ref, x_ref, mod_ref, nw_ref, *rest,
                     tm, nk, tiles_per_seq, grid_mode, next_norm):
    if next_norm:
        modn_ref, nwn_ref, o_ref, h_ref, acc_ref, act0_ref, act1_ref = rest
    else:
        o_ref, acc_ref, act0_ref, act1_ref = rest
    s = pl.program_id(0)
    i = jnp.minimum(s, pl.num_programs(0) - 2) // nk
    km = jnp.maximum(s - 1, 0) % nk
    first = (i % tiles_per_seq) == 0
    last = (i % tiles_per_seq) == tiles_per_seq - 1

    @pl.when(s == 0)
    def _():
        act1_ref[...] = jnp.zeros_like(act1_ref)

    @pl.when(km == 0)
    def _():
        acc_ref[...] = jnp.zeros_like(acc_ref)

    def step(act_in, act_out):
        row_parts = tm // MXU_ROWS

        def mm_tile(n):
            rows = slice((n % row_parts) * MXU_ROWS, (n % row_parts + 1) * MXU_ROWS)
            cols = slice((n // row_parts) * MXU_COLS, (n // row_parts + 1) * MXU_COLS)
            acc_ref[rows, cols] += jnp.dot(act_in[rows, :], w_ref[:, cols], preferred_element_type=F32)

        _gate_chunk(gp_ref, gm_ref, gn_ref, val_ref, cw_ref, cb_ref, act_out, first, last, mm_tile,
                    row_parts * D_MODEL // MXU_COLS, tm=tm, grid_mode=grid_mode)

    @pl.when(s % 2 == 0)
    def _():
        step(act1_ref, act0_ref)

    @pl.when(s % 2 == 1)
    def _():
        step(act0_ref, act1_ref)

    @pl.when((km == nk - 1) & (s > 0))
    def _():
        x_new = x_ref[...] + mod_ref[0, 5:6, :] * (_rms_rows(acc_ref[...]) * nw_ref[...])
        o_ref[...] = x_new
        if next_norm:
            h_ref[...] = _prenorm(x_new, modn_ref, nwn_ref, 0)


def _ffn_down_call(gv2, cw, cb, w_bf, layer, x2, mod3, nw, tm, seq_len, mod_row, grid_mode, next_norm):
    m = x2.shape[0]
    tk = FFN_CHUNK
    nk = D_FF // tk
    hb = tm // GRID_W
    n_hblk = m // GRID_W
    tiles_per_seq = seq_len // tm
    n_steps = (m // tm) * nk + 1
    bi = lambda s: jnp.minimum(s, n_steps - 2) // nk
    bk = lambda s: jnp.minimum(s, n_steps - 2) % nk
    mi = lambda s: jnp.maximum(s - 1, 0) // nk
    mk = lambda s: jnp.maximum(s - 1, 0) % nk
    row_spec = pl.BlockSpec((tm, D_MODEL), lambda s: (mi(s), 0))
    mod_spec = pl.BlockSpec((1, MOD_ROWS, D_MODEL), lambda s: (mod_row(mi(s)), 0, 0))
    vec_spec = pl.BlockSpec((1, D_MODEL), lambda s: (0, 0))
    extra_in, extra_specs, out_specs = (), [], row_spec
    out_shape = jax.ShapeDtypeStruct((m, D_MODEL), F32)
    if next_norm:
        extra_in, extra_specs = tuple(next_norm), [mod_spec, vec_spec]
        out_specs = [row_spec, row_spec]
        out_shape = [out_shape, jax.ShapeDtypeStruct((m, D_MODEL), BF16)]
    return pl.pallas_call(
        functools.partial(_ffn_down_kernel, tm=tm, nk=nk, tiles_per_seq=tiles_per_seq, grid_mode=grid_mode,
                          next_norm=bool(next_norm)),
        grid=(n_steps,),
        in_specs=[
            pl.BlockSpec((GRID_W, tk), lambda s: (jnp.maximum(bi(s) * hb - 1, 0), bk(s))),
            pl.BlockSpec((tm, tk), lambda s: (bi(s), bk(s))),
            pl.BlockSpec((GRID_W, tk), lambda s: (jnp.minimum((bi(s) + 1) * hb, n_hblk - 1), bk(s))),
            pl.BlockSpec((tm, tk), lambda s: (bi(s), nk + bk(s))),
            pl.BlockSpec((3, 3, tk), lambda s: (0, 0, bk(s))),
            pl.BlockSpec((1, tk), lambda s: (0, bk(s))),
            pl.BlockSpec((None, tk, D_MODEL), lambda s: (layer, mk(s), 0)),
            row_spec,
            mod_spec,
            vec_spec,
        ] + extra_specs,
        out_specs=out_specs,
        out_shape=out_shape,
        scratch_shapes=[
            pltpu.VMEM((tm, D_MODEL), F32),
            pltpu.VMEM((tm, tk), BF16),
            pltpu.VMEM((tm, tk), BF16),
        ],
        compiler_params=_cparams(("arbitrary",)),
        name="ffn_down",
    )(gv2, gv2, gv2, gv2, cw, cb.reshape(1, -1), w_bf, x2, mod3, nw, *extra_in)


class _Tiles(NamedTuple):
    proj: int
    proj_f32: int
    norm: int
    out: int
    down: int
    scan: int
    cols: int


def _stream(x3, h2, mod, l, params, s0, consts, tiles, row_of, grid_mode, full, emit_next):
    bsz, length, _ = x3.shape
    m = bsz * length
    x2 = x3.reshape(m, D_MODEL)
    mod3 = mod[l]
    nw = params["norm_w"][l]
    tn = tiles.cols
    w_in = params["w_in"]
    if h2 is None:
        h2 = _prenorm_call(x2, mod3, nw[0:1], tiles.norm, row_of(tiles.norm), 0, "mix_prenorm")
    f2 = _proj_call(h2, w_in, l, HG_WIDTH, 2 * HG_WIDTH, tiles.proj_f32, tn, False, F32, "in_proj_f")
    if full:
        q2 = _proj_call(h2, w_in, l, 0, HG_WIDTH, tiles.proj, tn, True, BF16, "in_proj_q")
        r2 = _proj_call(h2, w_in, l, 3 * HG_WIDTH, IN_COLS - 3 * HG_WIDTH, tiles.proj, tn, False, BF16,
                        "in_proj_vgab")
    else:
        r2 = _proj_call(h2, w_in, l, 3 * HG_WIDTH, HG_WIDTH, tiles.proj, tn, False, BF16, "in_proj_v")
        q2 = r2
    of3, ob3, sfin = _scan_call(q2.reshape(bsz, length, -1), f2.reshape(bsz, length, -1),
                                r2.reshape(bsz, length, -1), params["lb_logits"], s0, consts, l,
                                tiles.scan)
    if not full:
        return None, None, sfin
    cv3 = _cconv_call(r2.reshape(bsz, length, -1), params["conv_w"][l], params["conv_b"][l],
                      params["conv_ln_w"][l], params["conv_ln_b"][l])
    x2, h2 = _outproj_call(of3.reshape(m, -1), ob3.reshape(m, -1), r2, cv3.reshape(m, -1), x2, mod3,
                           params["hg_norm_w"][l].reshape(1, -1), nw[1:2], nw[2:3], params["w_out"], l,
                           tiles.out, row_of(tiles.out))
    gv2 = _proj_call(h2, params["ffn_up"], l, 0, 2 * D_FF, tiles.proj, tn, False, BF16, "ffn_up")
    next_norm = (mod[l + 1], params["norm_w"][l + 1][0:1]) if emit_next else None
    out = _ffn_down_call(gv2, params["ffn_conv_w"][l], params["ffn_conv_b"][l], params["ffn_down"], l,
                         x2, mod3, nw[3:4], tiles.down, length, row_of(tiles.down), grid_mode, next_norm)
    x2, h_next = out if emit_next else (out, None)
    return x2.reshape(bsz, length, D_MODEL), h_next, sfin


def kernel(x, c, ctx, c_ctx, w_mod, b_mod, norm_w, w_in, lb_logits, hg_norm_w, conv_w, conv_b, conv_ln_w,
           conv_ln_b, w_out, ffn_up, ffn_conv_w, ffn_conv_b, ffn_down):
    bsz, seq, _ = x.shape
    ctx_len = ctx.shape[1]
    params = dict(norm_w=norm_w, w_in=w_in, lb_logits=lb_logits, hg_norm_w=hg_norm_w,
                  conv_w=conv_w, conv_b=conv_b, conv_ln_w=conv_ln_w, conv_ln_b=conv_ln_b,
                  w_out=w_out.astype(BF16), ffn_up=ffn_up, ffn_conv_w=ffn_conv_w,
                  ffn_conv_b=ffn_conv_b, ffn_down=ffn_down.astype(BF16))
    consts = jnp.asarray(_scan_constants(), jnp.int32)

    cond = jnp.concatenate([c, c_ctx[None, :], jnp.zeros((MOD_ROWS - bsz - 1, D_MODEL), c.dtype)], axis=0)
    mod = _mod_call(_silu(cond).astype(BF16), w_mod, b_mod)
    mod = mod[:, :bsz + 1].reshape(DEPTH, bsz + 1, N_MOD, D_MODEL)
    mod = jnp.pad(mod, ((0, 0), (0, 0), (0, MOD_ROWS - N_MOD), (0, 0)))

    lat_row = lambda tm: (lambda i: i // (seq // tm))
    ctx_row = lambda tm: (lambda i: bsz)
    zero_state = jnp.zeros((bsz, HEADS, 2, HEAD_DIM, HEAD_DIM), F32)
    ctx_rows = bsz * ctx_len
    ctx_tiles = _Tiles(proj=ctx_rows, proj_f32=ctx_rows, norm=ctx_len, out=ctx_len, down=ctx_len,
                       scan=ctx_len, cols=1024)
    lat_tiles = _Tiles(proj=2048, proj_f32=1024, norm=512, out=512, down=512, scan=4096, cols=1024)
    xc, hc, h = ctx, None, None
    for l in range(DEPTH):
        last = l == DEPTH - 1
        xc, hc, s_ctx = _stream(xc, hc, mod, l, params, zero_state, consts, ctx_tiles, ctx_row,
                                grid_mode=False, full=not last, emit_next=not last)
        x, h, _ = _stream(x, h, mod, l, params, s_ctx, consts, lat_tiles, lat_row,
                          grid_mode=True, full=True, emit_next=not last)
    return x
```

```python
import functools
from typing import NamedTuple

import numpy as np
import jax
import jax.numpy as jnp
from jax import lax
from jax.experimental import pallas as pl
from jax.experimental.pallas import tpu as pltpu

F32 = jnp.float32
BF16 = jnp.bfloat16

D_MODEL = 2048
DEPTH = 2
GRID_W = 64
HEADS = 8
HEAD_DIM = 128
HG_WIDTH = HEADS * HEAD_DIM
CONV_CH = D_MODEL - HG_WIDTH
CONV_WIDTH = 31
CONV_PAD = 16
CONV_ROWS = 128
D_FF = 5632
N_MOD = 6
MOD_ROWS = 8
EPS = 1e-6
LN_EPS = 1e-5
LOG2_E = float(np.log2(np.e))
IN_COLS = 3 * HG_WIDTH + 2 * HG_WIDTH + 2 * CONV_CH

CHUNK = 64
N_LEVELS = 6

V7X_VMEM_BYTES = 64 * 1024 * 1024
VMEM_LIMIT = V7X_VMEM_BYTES * 7 // 8


def _cparams(sem):
    return pltpu.CompilerParams(dimension_semantics=sem, vmem_limit_bytes=VMEM_LIMIT)


def _silu(x):
    return x * jax.nn.sigmoid(x)


def _rms_rows(x):
    return x * lax.rsqrt(jnp.mean(x * x, axis=-1, keepdims=True) + EPS)


def _mod_kernel(s_ref, w_ref, b_ref, o_ref):
    w = w_ref[0].astype(BF16)
    o_ref[0] = jnp.dot(s_ref[...], w, preferred_element_type=F32) + b_ref[0]


MOD_COLS = 1024


def _mod_call(s_rows, w_mod, b_mod):
    tn = MOD_COLS
    n = w_mod.shape[-1]
    return pl.pallas_call(
        _mod_kernel,
        grid=(DEPTH, n // tn),
        in_specs=[
            pl.BlockSpec((MOD_ROWS, D_MODEL), lambda l, j: (0, 0)),
            pl.BlockSpec((1, D_MODEL, tn), lambda l, j: (l, 0, j)),
            pl.BlockSpec((1, 1, tn), lambda l, j: (l, 0, j)),
        ],
        out_specs=pl.BlockSpec((1, MOD_ROWS, tn), lambda l, j: (l, 0, j)),
        out_shape=jax.ShapeDtypeStruct((DEPTH, MOD_ROWS, n), F32),
        compiler_params=_cparams(("arbitrary", "arbitrary")),
        name="mod_matmul",
    )(s_rows, w_mod, b_mod.reshape(DEPTH, 1, n))


def _prenorm(x, mod_ref, nw_ref, shift_row):
    y = _rms_rows(x) * nw_ref[...]
    h = y * (1.0 + mod_ref[0, shift_row + 1:shift_row + 2, :]) + mod_ref[0, shift_row:shift_row + 1, :]
    return h.astype(BF16)


def _prenorm_kernel(x_ref, mod_ref, nw_ref, h_ref, *, shift_row):
    h_ref[...] = _prenorm(x_ref[...], mod_ref, nw_ref, shift_row)


def _prenorm_call(x2, mod3, nw, tm, mod_row, shift_row, name):
    m = x2.shape[0]
    return pl.pallas_call(
        functools.partial(_prenorm_kernel, shift_row=shift_row),
        grid=(m // tm,),
        in_specs=[
            pl.BlockSpec((tm, D_MODEL), lambda i: (i, 0)),
            pl.BlockSpec((1, MOD_ROWS, D_MODEL), lambda i: (mod_row(i), 0, 0)),
            pl.BlockSpec((1, D_MODEL), lambda i: (0, 0)),
        ],
        out_specs=pl.BlockSpec((tm, D_MODEL), lambda i: (i, 0)),
        out_shape=jax.ShapeDtypeStruct((m, D_MODEL), BF16),
        compiler_params=_cparams(("arbitrary",)),
        name=name,
    )(x2, mod3, nw)


def _proj_kernel(h_ref, w_ref, o_ref, wb_ref, *, act):
    @pl.when(pl.program_id(1) == 0)
    def _():
        wb_ref[...] = w_ref[...].astype(BF16)

    acc = jnp.dot(h_ref[...], wb_ref[...], preferred_element_type=F32)
    if act:
        acc = _silu(acc)
    o_ref[...] = acc.astype(o_ref.dtype)


def _proj_call(h2, w, layer, col0, n_cols, tm, tn, act, out_dtype, name):
    m = h2.shape[0]
    c0 = col0 // tn
    return pl.pallas_call(
        functools.partial(_proj_kernel, act=act),
        grid=(n_cols // tn, m // tm),
        in_specs=[
            pl.BlockSpec((tm, D_MODEL), lambda j, i: (i, 0)),
            pl.BlockSpec((None, D_MODEL, tn), lambda j, i: (layer, 0, c0 + j)),
        ],
        out_specs=pl.BlockSpec((tm, tn), lambda j, i: (i, j)),
        out_shape=jax.ShapeDtypeStruct((m, n_cols), out_dtype),
        scratch_shapes=[pltpu.VMEM((D_MODEL, tn), BF16)],
        compiler_params=_cparams(("arbitrary", "arbitrary")),
        name=name,
    )(h2, w)


def _scan_constants():
    c = CHUNK
    t = np.arange(c)[:, None]
    r = np.arange(c)[None, :]
    level = np.full((2, c, c), -1, np.int32)
    for lvl in range(N_LEVELS):
        half = c >> (lvl + 1)
        same = (t // (2 * half)) == (r // (2 * half))
        t_late = (t % (2 * half)) >= half
        r_late = (r % (2 * half)) >= half
        level[0][same & t_late & ~r_late] = lvl
        level[1][same & ~t_late & r_late] = lvl
    level[:, np.arange(c), np.arange(c)] = N_LEVELS
    return level


_NT = (((1,), (1,)), ((), ()))
_TN = (((0,), (0,)), ((), ()))


def _neg_abs(x):
    bits = lax.bitcast_convert_type(x, jnp.int32) | jnp.int32(-2 ** 31)
    return lax.bitcast_convert_type(bits, F32)


def _scan_kernel(lbl_ref, level_ref, qf_ref, qb_ref, vf_ref, vb_ref, ff_ref, fb_ref,
                 s0_ref, of_ref, ob_ref, sfin_ref, st_ref, *, layer, n_chunks):
    c = CHUNK
    step = pl.program_id(2)

    @pl.when(step == 0)
    def _():
        st_ref[...] = s0_ref[0, 0]

    rows = [lbl_ref[:, j, :] for j in range(DEPTH)]
    mx = functools.reduce(jnp.maximum, rows)
    es = [jnp.exp(rw - mx) for rw in rows]
    tot = functools.reduce(lambda a, b: a + b, es)
    lb = jnp.zeros_like(mx)
    for j in range(1, layer + 1):
        lb = lb + es[j] / tot

    level = level_ref[...]
    sub = lax.broadcasted_iota(jnp.int32, (c // 8, 8, HEAD_DIM), 1)

    def ref_rows(b, half, d):
        pick = half - 1 if d == 0 else half
        if half >= 8:
            parts = []
            for p0 in range(0, c, 2 * half):
                parts.append(jnp.broadcast_to(b[p0 + pick:p0 + pick + 1, :], (2 * half, HEAD_DIM)))
            return parts[0] if len(parts) == 1 else jnp.concatenate(parts, axis=0)
        b3 = b.reshape(c // 8, 8, HEAD_DIM)
        out = None
        for g0 in range(0, 8, 2 * half):
            cand = jnp.broadcast_to(b3[:, g0 + pick:g0 + pick + 1, :], b3.shape)
            out = cand if out is None else jnp.where(sub >= g0, cand, out)
        return out.reshape(c, HEAD_DIM)

    refs = ((qf_ref, vf_ref, ff_ref, of_ref), (qb_ref, vb_ref, fb_ref, ob_ref))

    def gates(d, row0):
        raw = refs[d][2][0, row0:row0 + c, :]
        lbd = lb[d:d + 1, :]
        f = lbd + (1.0 - lbd) * jax.nn.sigmoid(raw)
        x = (jnp.log(f) * LOG2_E).reshape(c // 8, 8, HEAD_DIM)
        for s in (1, 2, 4):
            if d == 0:
                x = x + jnp.where(sub >= s, pltpu.roll(x, s, axis=1), 0.0)
            else:
                x = x + jnp.where(sub < 8 - s, pltpu.roll(x, 8 - s, axis=1), 0.0)
        groups = [x[g] for g in range(c // 8)]
        edge = 7 if d == 0 else 0
        totals = [grp[edge:edge + 1, :] for grp in groups]
        order = range(1, c // 8) if d == 0 else range(c // 8 - 2, -1, -1)
        carry = None
        for g in order:
            before = totals[g - 1] if d == 0 else totals[g + 1]
            carry = before if carry is None else carry + before
            groups[g] = groups[g] + carry
        return f, jnp.concatenate(groups, axis=0)

    none = jnp.full((c, c), -1, jnp.int32)
    level2 = jnp.concatenate([jnp.concatenate([level[0], none], axis=1),
                              jnp.concatenate([none, level[1]], axis=1)], axis=0)
    pad = jnp.zeros((c, HEAD_DIM), BF16)

    def diag2(fwd, bwd, fill):
        return jnp.concatenate([jnp.concatenate([fwd, fill], axis=1),
                                jnp.concatenate([fill, bwd], axis=1)], axis=0)

    def intra(rows, gated):
        q_bf, k_bf, b, own, pair = [], [], [], [], []
        for d in range(2):
            f, bd = gated[d]
            qd = refs[d][0][0, rows[d]:rows[d] + c, :]
            q = qd.astype(F32)
            key = 1.0 - f
            own.append(jnp.sum(q * key, axis=-1, keepdims=True))
            pair_key = pltpu.roll(key, 1 if d == 0 else c - 1, axis=0)
            pair.append(jnp.sum(q * f * pair_key, axis=-1, keepdims=True))
            q_bf.append(qd)
            k_bf.append(key.astype(BF16))
            b.append(bd)
        rows_of = lambda fwd, bwd: jnp.concatenate([jnp.broadcast_to(fwd, (c, 2 * c)),
                                                    jnp.broadcast_to(bwd, (c, 2 * c))], axis=0)
        att = jnp.where(level2 == N_LEVELS, rows_of(own[0], own[1]), 0.0)
        att = jnp.where(level2 == N_LEVELS - 1, rows_of(pair[0], pair[1]), att)
        for lvl in range(N_LEVELS - 1):
            half = c >> (lvl + 1)
            el = [jnp.exp2(_neg_abs(b[d] - ref_rows(b[d], half, d))).astype(BF16) for d in range(2)]
            lhs = diag2(q_bf[0] * el[0], q_bf[1] * el[1], pad)
            rhs = diag2(k_bf[0] * el[0], k_bf[1] * el[1], pad)
            att = jnp.where(level2 == lvl, lax.dot_general(lhs, rhs, _NT, preferred_element_type=F32), att)
        q_in, k_out, decay = [], [], []
        for d in range(2):
            last = c - 1 if d == 0 else 0
            blast = b[d][last:last + 1, :]
            q_in.append(q_bf[d] * jnp.exp2(b[d]).astype(BF16))
            k_out.append(k_bf[d] * jnp.exp2(blast - b[d]).astype(BF16))
            decay.append(jnp.transpose(jnp.broadcast_to(jnp.exp2(blast), (8, HEAD_DIM)))[:, 0:1])
        q_att = jnp.concatenate([diag2(q_in[0], q_in[1], pad), att.astype(BF16)], axis=1)
        return q_att, diag2(k_out[0], k_out[1], pad), jnp.concatenate(decay, axis=0)

    def readout(rows, q_att, k_out, decay, st):
        v = jnp.concatenate([refs[d][1][0, rows[d]:rows[d] + c, :] for d in range(2)], axis=0)
        o = jnp.dot(q_att, jnp.concatenate([st.astype(BF16), v], axis=0), preferred_element_type=F32)
        for d in range(2):
            refs[d][3][0, rows[d]:rows[d] + c, :] = o[d * c:(d + 1) * c]
        return st * decay + lax.dot_general(k_out, v, _TN, preferred_element_type=F32)

    work = [(ci * c, (n_chunks - 1 - ci) * c) for ci in range(n_chunks)]
    st = st_ref[...].reshape(2 * HEAD_DIM, HEAD_DIM)
    stage1 = {}
    stage2 = {}
    for n in range(len(work) + 2):
        if n >= 2:
            st = readout(work[n - 2], *stage2.pop(n - 2), st)
        if 1 <= n <= len(work):
            stage2[n - 1] = intra(work[n - 1], stage1.pop(n - 1))
        if n < len(work):
            stage1[n] = [gates(d, work[n][d]) for d in range(2)]
    st_ref[...] = st.reshape(2, HEAD_DIM, HEAD_DIM)

    @pl.when(step == pl.num_programs(2) - 1)
    def _():
        sfin_ref[0, 0] = st_ref[...]


def _scan_call(q3, f3, r3, lb_logits, s0, level, layer, tl):
    bsz, length, _ = q3.shape
    nl = length // tl
    fwd = lambda b, h, i: (b, i, h)
    bwd = lambda b, h, i: (b, nl - 1 - i, h)
    bwd_f = lambda b, h, i: (b, nl - 1 - i, HEADS + h)
    blk = (1, tl, HEAD_DIM)
    return pl.pallas_call(
        functools.partial(_scan_kernel, layer=layer, n_chunks=tl // CHUNK),
        grid=(bsz, HEADS, nl),
        in_specs=[
            pl.BlockSpec((2, DEPTH, HEAD_DIM), lambda b, h, i: (0, 0, h)),
            pl.BlockSpec(level.shape, lambda b, h, i: (0, 0, 0)),
            pl.BlockSpec(blk, fwd),
            pl.BlockSpec(blk, bwd),
            pl.BlockSpec(blk, fwd),
            pl.BlockSpec(blk, bwd),
            pl.BlockSpec(blk, fwd),
            pl.BlockSpec(blk, bwd_f),
            pl.BlockSpec((1, 1, 2, HEAD_DIM, HEAD_DIM), lambda b, h, i: (b, h, 0, 0, 0)),
        ],
        out_specs=[
            pl.BlockSpec(blk, fwd),
            pl.BlockSpec(blk, bwd),
            pl.BlockSpec((1, 1, 2, HEAD_DIM, HEAD_DIM), lambda b, h, i: (b, h, 0, 0, 0)),
        ],
        out_shape=[
            jax.ShapeDtypeStruct((bsz, length, HG_WIDTH), F32),
            jax.ShapeDtypeStruct((bsz, length, HG_WIDTH), F32),
            jax.ShapeDtypeStruct((bsz, HEADS, 2, HEAD_DIM, HEAD_DIM), F32),
        ],
        scratch_shapes=[pltpu.VMEM((2, HEAD_DIM, HEAD_DIM), F32)],
        compiler_params=_cparams(("arbitrary", "arbitrary", "arbitrary")),
        name="hgrn2_scan",
    )(lb_logits, level, q3, q3, r3, r3, f3, f3, s0)


def _cconv_kernel(a_ref, b_ref, cw_ref, cb_ref, lw_ref, lb_ref, o_ref, u_ref, *, length, rows):
    zeros = jnp.zeros((CONV_PAD, HEAD_DIM), F32)
    u_ref[0:CONV_PAD, :] = zeros
    u_ref[CONV_PAD + length:2 * CONV_PAD + length, :] = zeros
    n_tiles = length // rows

    def fill(ti, carry):
        r0 = pl.multiple_of(ti * rows, rows)
        a = a_ref[0, pl.ds(r0, rows), :].astype(F32)
        b = b_ref[0, pl.ds(r0, rows), :].astype(F32)
        u_ref[pl.ds(CONV_PAD + r0, rows), :] = a * jax.nn.sigmoid(b)
        return carry

    lax.fori_loop(0, n_tiles, fill, 0, unroll=4)

    half = CONV_WIDTH // 2

    def conv(ti, carry):
        r0 = pl.multiple_of(ti * rows, rows)
        lead = CONV_PAD - half
        span = -(-(lead + CONV_WIDTH - 1) // 8) * 8 - 8
        acc = jnp.zeros((rows, HEAD_DIM), F32)
        for phase in range(8):
            window = u_ref[pl.ds(r0 + phase, rows + span), :]
            for off in range(phase, lead + CONV_WIDTH, 8):
                if off >= lead:
                    j = off - lead
                    acc = acc + cw_ref[j:j + 1, :] * window[off - phase:off - phase + rows]
        acc = acc + cb_ref[...]
        mu = jnp.mean(acc, axis=-1, keepdims=True)
        dlt = acc - mu
        var = jnp.mean(dlt * dlt, axis=-1, keepdims=True)
        y = dlt * lax.rsqrt(var + LN_EPS) * lw_ref[...] + lb_ref[...]
        o_ref[0, pl.ds(r0, rows), :] = _silu(y).astype(BF16)
        return carry

    lax.fori_loop(0, n_tiles, conv, 0, unroll=4)


def _cconv_call(r3, cw, cb, lw, lb):
    bsz, length, _ = r3.shape
    groups = CONV_CH // HEAD_DIM
    a_col = 2 * HG_WIDTH // HEAD_DIM
    b_col = a_col + groups
    vec = pl.BlockSpec((1, HEAD_DIM), lambda b, g: (0, g))
    return pl.pallas_call(
        functools.partial(_cconv_kernel, length=length, rows=CONV_ROWS),
        grid=(bsz, groups),
        in_specs=[
            pl.BlockSpec((1, length, HEAD_DIM), lambda b, g: (b, 0, a_col + g)),
            pl.BlockSpec((1, length, HEAD_DIM), lambda b, g: (b, 0, b_col + g)),
            pl.BlockSpec((CONV_WIDTH, HEAD_DIM), lambda b, g: (0, g)),
            vec, vec, vec,
        ],
        out_specs=pl.BlockSpec((1, length, HEAD_DIM), lambda b, g: (b, 0, g)),
        out_shape=jax.ShapeDtypeStruct((bsz, length, CONV_CH), BF16),
        scratch_shapes=[pltpu.VMEM((length + 2 * CONV_PAD, HEAD_DIM), F32)],
        compiler_params=_cparams(("arbitrary", "arbitrary")),
        name="conformer_conv",
    )(r3, r3, cw, cb.reshape(1, -1), lw.reshape(1, -1), lb.reshape(1, -1))


def _outproj_kernel(of_ref, ob_ref, g_ref, cv_ref, x_ref, mod_ref, hgw_ref, nw_ref, nw2_ref, w_ref, o_ref,
                    h_ref):
    o = of_ref[...] + ob_ref[...]
    heads = []
    for h in range(HEADS):
        oh = o[:, h * HEAD_DIM:(h + 1) * HEAD_DIM]
        heads.append(_rms_rows(oh) * hgw_ref[...])
    r = jnp.concatenate(heads, axis=-1) * _silu(g_ref[...].astype(F32))
    y = jnp.dot(r.astype(BF16), w_ref[0:HG_WIDTH, :], preferred_element_type=F32)
    y = y + jnp.dot(cv_ref[...], w_ref[HG_WIDTH:D_MODEL, :], preferred_element_type=F32)
    x_new = x_ref[...] + mod_ref[0, 2:3, :] * (_rms_rows(y) * nw_ref[...])
    o_ref[...] = x_new
    h_ref[...] = _prenorm(x_new, mod_ref, nw2_ref, 3)


def _outproj_call(of2, ob2, r2, cv2, x2, mod3, hgw, nw, nw2, w_bf, layer, tm, mod_row):
    m = x2.shape[0]
    return pl.pallas_call(
        _outproj_kernel,
        grid=(m // tm,),
        in_specs=[
            pl.BlockSpec((tm, HG_WIDTH), lambda i: (i, 0)),
            pl.BlockSpec((tm, HG_WIDTH), lambda i: (i, 0)),
            pl.BlockSpec((tm, HG_WIDTH), lambda i: (i, 1)),
            pl.BlockSpec((tm, CONV_CH), lambda i: (i, 0)),
            pl.BlockSpec((tm, D_MODEL), lambda i: (i, 0)),
            pl.BlockSpec((1, MOD_ROWS, D_MODEL), lambda i: (mod_row(i), 0, 0)),
            pl.BlockSpec((1, HEAD_DIM), lambda i: (0, 0)),
            pl.BlockSpec((1, D_MODEL), lambda i: (0, 0)),
            pl.BlockSpec((1, D_MODEL), lambda i: (0, 0)),
            pl.BlockSpec((None, D_MODEL, D_MODEL), lambda i: (layer, 0, 0), pipeline_mode=pl.Buffered(1)),
        ],
        out_specs=[
            pl.BlockSpec((tm, D_MODEL), lambda i: (i, 0)),
            pl.BlockSpec((tm, D_MODEL), lambda i: (i, 0)),
        ],
        out_shape=[
            jax.ShapeDtypeStruct((m, D_MODEL), F32),
            jax.ShapeDtypeStruct((m, D_MODEL), BF16),
        ],
        compiler_params=_cparams(("arbitrary",)),
        name="out_proj",
    )(of2, ob2, r2, cv2, x2, mod3, hgw, nw, nw2, w_bf)


ROW_BLOCK = 32
MXU_COLS = 256
MXU_ROWS = 256
FFN_CHUNK = 512
LANE_BLOCK = 256


def _gate_chunk(gp_ref, gm_ref, gn_ref, val_ref, cw_ref, cb_ref, act_out, first, last, mm_tile,
                mm_tiles, *, tm, grid_mode):
    tk = FFN_CHUNK
    rows = ROW_BLOCK
    lanes = LANE_BLOCK
    n_blocks = tm // rows
    total = n_blocks * (tk // lanes)
    row_id = lax.broadcasted_iota(jnp.int32, (rows, lanes), 0)
    edge = GRID_W if grid_mode else tm
    zero = jnp.zeros((), BF16)

    def gate_rows(r0, ls):
        if r0 < 0:
            return jnp.where(first, zero, gp_ref[GRID_W + r0:GRID_W + r0 + rows, ls])
        if r0 >= tm:
            return jnp.where(last, zero, gn_ref[r0 - tm:r0 - tm + rows, ls])
        return gm_ref[r0:r0 + rows, ls]

    done = 0
    for l0 in range(0, tk, lanes):
        ls = slice(l0, l0 + lanes)
        taps = (0.5 * cw_ref[:, :, ls]).astype(BF16)
        taps = [[jnp.broadcast_to(taps[dh, dw:dw + 1], (rows, lanes)) for dw in range(3)] for dh in range(3)]
        bias = 0.5 * cb_ref[:, ls]

        def columns(r0):
            ce = gate_rows(r0, ls)
            if not grid_mode:
                return [(taps[1][dw] * ce).astype(F32) for dw in range(3)]
            up = gate_rows(r0 - GRID_W, ls)
            dn = gate_rows(r0 + GRID_W, ls)
            return [(taps[0][dw] * up + taps[1][dw] * ce + taps[2][dw] * dn).astype(F32) for dw in range(3)]

        cur = columns(0)
        prev_left_row = None
        for blk in range(n_blocks):
            r0 = blk * rows
            for n in range(done * mm_tiles // total, (done + 1) * mm_tiles // total):
                mm_tile(n)
            done += 1
            nxt = columns(r0 + rows) if blk + 1 < n_blocks else None
            left, mid, right = cur
            from_prev = pltpu.roll(left, 1, axis=0)
            if r0 % edge == 0:
                from_prev = jnp.where(row_id == 0, 0.0, from_prev)
            else:
                from_prev = jnp.where(row_id == 0, prev_left_row, from_prev)
            from_next = pltpu.roll(right, rows - 1, axis=0)
            if (r0 + rows) % edge == 0:
                from_next = jnp.where(row_id == rows - 1, 0.0, from_next)
            else:
                from_next = jnp.where(row_id == rows - 1, nxt[2][0:1, :], from_next)
            half = mid + from_prev + from_next + bias
            gelu = half * (1.0 + lax.erf(half * np.float32(np.sqrt(2.0))))
            act_out[r0:r0 + rows, ls] = gelu.astype(BF16) * val_ref[r0:r0 + rows, ls]
            prev_left_row = left[rows - 1:rows, :]
            cur = nxt


def _ffn_down_kernel(gp_ref, gm_ref, gn_ref, val_ref, cw_ref, cb_ref, w_ref, x_ref, mod_ref, nw_ref, *rest,
                     tm, nk, tiles_per_seq, grid_mode, next_norm):
    if next_norm:
        modn_ref, nwn_ref, o_ref, h_ref, acc_ref, act0_ref, act1_ref = rest
    else:
        o_ref, acc_ref, act0_ref, act1_ref = rest
    s = pl.program_id(0)
    i = jnp.minimum(s, pl.num_programs(0) - 2) // nk
    km = jnp.maximum(s - 1, 0) % nk
    first = (i % tiles_per_seq) == 0
    last = (i % tiles_per_seq) == tiles_per_seq - 1

    @pl.when(s == 0)
    def _():
        act1_ref[...] = jnp.zeros_like(act1_ref)

    @pl.when(km == 0)
    def _():
        acc_ref[...] = jnp.zeros_like(acc_ref)

    def step(act_in, act_out):
        row_parts = tm // MXU_ROWS

        def mm_tile(n):
            rows = slice((n % row_parts) * MXU_ROWS, (n % row_parts + 1) * MXU_ROWS)
            cols = slice((n // row_parts) * MXU_COLS, (n // row_parts + 1) * MXU_COLS)
            acc_ref[rows, cols] += jnp.dot(act_in[rows, :], w_ref[:, cols], preferred_element_type=F32)

        _gate_chunk(gp_ref, gm_ref, gn_ref, val_ref, cw_ref, cb_ref, act_out, first, last, mm_tile,
                    row_parts * D_MODEL // MXU_COLS, tm=tm, grid_mode=grid_mode)

    @pl.when(s % 2 == 0)
    def _():
        step(act1_ref, act0_ref)

    @pl.when(s % 2 == 1)
    def _():
        step(act0_ref, act1_ref)

    @pl.when((km == nk - 1) & (s > 0))
    def _():
        x_new = x_ref[...] + mod_ref[0, 5:6, :] * (_rms_rows(acc_ref[...]) * nw_ref[...])
        o_ref[...] = x_new
        if next_norm:
            h_ref[...] = _prenorm(x_new, modn_ref, nwn_ref, 0)


def _ffn_down_call(gv2, cw, cb, w_bf, layer, x2, mod3, nw, tm, seq_len, mod_row, grid_mode, next_norm):
    m = x2.shape[0]
    tk = FFN_CHUNK
    nk = D_FF // tk
    hb = tm // GRID_W
    n_hblk = m // GRID_W
    tiles_per_seq = seq_len // tm
    n_steps = (m // tm) * nk + 1
    bi = lambda s: jnp.minimum(s, n_steps - 2) // nk
    bk = lambda s: jnp.minimum(s, n_steps - 2) % nk
    mi = lambda s: jnp.maximum(s - 1, 0) // nk
    mk = lambda s: jnp.maximum(s - 1, 0) % nk
    row_spec = pl.BlockSpec((tm, D_MODEL), lambda s: (mi(s), 0))
    mod_spec = pl.BlockSpec((1, MOD_ROWS, D_MODEL), lambda s: (mod_row(mi(s)), 0, 0))
    vec_spec = pl.BlockSpec((1, D_MODEL), lambda s: (0, 0))
    extra_in, extra_specs, out_specs = (), [], row_spec
    out_shape = jax.ShapeDtypeStruct((m, D_MODEL), F32)
    if next_norm:
        extra_in, extra_specs = tuple(next_norm), [mod_spec, vec_spec]
        out_specs = [row_spec, row_spec]
        out_shape = [out_shape, jax.ShapeDtypeStruct((m, D_MODEL), BF16)]
    return pl.pallas_call(
        functools.partial(_ffn_down_kernel, tm=tm, nk=nk, tiles_per_seq=tiles_per_seq, grid_mode=grid_mode,
                          next_norm=bool(next_norm)),
        grid=(n_steps,),
        in_specs=[
            pl.BlockSpec((GRID_W, tk), lambda s: (jnp.maximum(bi(s) * hb - 1, 0), bk(s))),
            pl.BlockSpec((tm, tk), lambda s: (bi(s), bk(s))),
            pl.BlockSpec((GRID_W, tk), lambda s: (jnp.minimum((bi(s) + 1) * hb, n_hblk - 1), bk(s))),
            pl.BlockSpec((tm, tk), lambda s: (bi(s), nk + bk(s))),
            pl.BlockSpec((3, 3, tk), lambda s: (0, 0, bk(s))),
            pl.BlockSpec((1, tk), lambda s: (0, bk(s))),
            pl.BlockSpec((None, tk, D_MODEL), lambda s: (layer, mk(s), 0)),
            row_spec,
            mod_spec,
            vec_spec,
        ] + extra_specs,
        out_specs=out_specs,
        out_shape=out_shape,
        scratch_shapes=[
            pltpu.VMEM((tm, D_MODEL), F32),
            pltpu.VMEM((tm, tk), BF16),
            pltpu.VMEM((tm, tk), BF16),
        ],
        compiler_params=_cparams(("arbitrary",)),
        name="ffn_down",
    )(gv2, gv2, gv2, gv2, cw, cb.reshape(1, -1), w_bf, x2, mod3, nw, *extra_in)


class _Tiles(NamedTuple):
    proj: int
    proj_f32: int
    norm: int
    out: int
    down: int
    scan: int
    cols: int


def _stream(x3, h2, mod, l, params, s0, consts, tiles, row_of, grid_mode, full, emit_next):
    bsz, length, _ = x3.shape
    m = bsz * length
    x2 = x3.reshape(m, D_MODEL)
    mod3 = mod[l]
    nw = params["norm_w"][l]
    tn = tiles.cols
    w_in = params["w_in"]
    if h2 is None:
        h2 = _prenorm_call(x2, mod3, nw[0:1], tiles.norm, row_of(tiles.norm), 0, "mix_prenorm")
    f2 = _proj_call(h2, w_in, l, HG_WIDTH, 2 * HG_WIDTH, tiles.proj_f32, tn, False, F32, "in_proj_f")
    if full:
        q2 = _proj_call(h2, w_in, l, 0, HG_WIDTH, tiles.proj, tn, True, BF16, "in_proj_q")
        r2 = _proj_call(h2, w_in, l, 3 * HG_WIDTH, IN_COLS - 3 * HG_WIDTH, tiles.proj, tn, False, BF16,
                        "in_proj_vgab")
    else:
        r2 = _proj_call(h2, w_in, l, 3 * HG_WIDTH, HG_WIDTH, tiles.proj, tn, False, BF16, "in_proj_v")
        q2 = r2
    of3, ob3, sfin = _scan_call(q2.reshape(bsz, length, -1), f2.reshape(bsz, length, -1),
                                r2.reshape(bsz, length, -1), params["lb_logits"], s0, consts, l,
                                tiles.scan)
    if not full:
        return None, None, sfin
    cv3 = _cconv_call(r2.reshape(bsz, length, -1), params["conv_w"][l], params["conv_b"][l],
                      params["conv_ln_w"][l], params["conv_ln_b"][l])
    x2, h2 = _outproj_call(of3.reshape(m, -1), ob3.reshape(m, -1), r2, cv3.reshape(m, -1), x2, mod3,
                           params["hg_norm_w"][l].reshape(1, -1), nw[1:2], nw[2:3], params["w_out"], l,
                           tiles.out, row_of(tiles.out))
    gv2 = _proj_call(h2, params["ffn_up"], l, 0, 2 * D_FF, tiles.proj, tn, False, BF16, "ffn_up")
    next_norm = (mod[l + 1], params["norm_w"][l + 1][0:1]) if emit_next else None
    out = _ffn_down_call(gv2, params["ffn_conv_w"][l], params["ffn_conv_b"][l], params["ffn_down"], l,
                         x2, mod3, nw[3:4], tiles.down, length, row_of(tiles.down), grid_mode, next_norm)
    x2, h_next = out if emit_next else (out, None)
    return x2.reshape(bsz, length, D_MODEL), h_next, sfin


def kernel(x, c, ctx, c_ctx, w_mod, b_mod, norm_w, w_in, lb_logits, hg_norm_w, conv_w, conv_b, conv_ln_w,
           conv_ln_b, w_out, ffn_up, ffn_conv_w, ffn_conv_b, ffn_down):
    bsz, seq, _ = x.shape
    ctx_len = ctx.shape[1]
    params = dict(norm_w=norm_w, w_in=w_in, lb_logits=lb_logits, hg_norm_w=hg_norm_w,
                  conv_w=conv_w, conv_b=conv_b, conv_ln_w=conv_ln_w, conv_ln_b=conv_ln_b,
                  w_out=w_out.astype(BF16), ffn_up=ffn_up, ffn_conv_w=ffn_conv_w,
                  ffn_conv_b=ffn_conv_b, ffn_down=ffn_down.astype(BF16))
    consts = jnp.asarray(_scan_constants(), jnp.int32)

    cond = jnp.concatenate([c, c_ctx[None, :], jnp.zeros((MOD_ROWS - bsz - 1, D_MODEL), c.dtype)], axis=0)
    mod = _mod_call(_silu(cond).astype(BF16), w_mod, b_mod)
    mod = mod[:, :bsz + 1].reshape(DEPTH, bsz + 1, N_MOD, D_MODEL)
    mod = jnp.pad(mod, ((0, 0), (0, 0), (0, MOD_ROWS - N_MOD), (0, 0)))

    lat_row = lambda tm: (lambda i: i // (seq // tm))
    ctx_row = lambda tm: (lambda i: bsz)
    zero_state = jnp.zeros((bsz, HEADS, 2, HEAD_DIM, HEAD_DIM), F32)
    ctx_rows = bsz * ctx_len
    ctx_tiles = _Tiles(proj=ctx_rows, proj_f32=ctx_rows, norm=ctx_len, out=ctx_len, down=ctx_len,
                       scan=ctx_len, cols=1024)
    lat_tiles = _Tiles(proj=2048, proj_f32=1024, norm=512, out=512, down=512, scan=4096, cols=1024)
    xc, hc, h = ctx, None, None
    for l in range(DEPTH):
        last = l == DEPTH - 1
        xc, hc, s_ctx = _stream(xc, hc, mod, l, params, zero_state, consts, ctx_tiles, ctx_row,
                                grid_mode=False, full=not last, emit_next=not last)
        x, h, _ = _stream(x, h, mod, l, params, s_ctx, consts, lat_tiles, lat_row,
                          grid_mode=True, full=True, emit_next=not last)
    return x
```

```python
import functools
from typing import NamedTuple

import numpy as np
import jax
import jax.numpy as jnp
from jax import lax
from jax.experimental import pallas as pl
from jax.experimental.pallas import tpu as pltpu

F32 = jnp.float32
BF16 = jnp.bfloat16

D_MODEL = 2048
DEPTH = 2
GRID_W = 64
HEADS = 8
HEAD_DIM = 128
HG_WIDTH = HEADS * HEAD_DIM
CONV_CH = D_MODEL - HG_WIDTH
CONV_WIDTH = 31
CONV_PAD = 16
CONV_ROWS = 128
D_FF = 5632
N_MOD = 6
MOD_ROWS = 8
EPS = 1e-6
LN_EPS = 1e-5
LOG2_E = float(np.log2(np.e))
IN_COLS = 3 * HG_WIDTH + 2 * HG_WIDTH + 2 * CONV_CH

CHUNK = 64
N_LEVELS = 6

V7X_VMEM_BYTES = 64 * 1024 * 1024
VMEM_LIMIT = V7X_VMEM_BYTES * 7 // 8


def _cparams(sem):
    return pltpu.CompilerParams(dimension_semantics=sem, vmem_limit_bytes=VMEM_LIMIT)


def _silu(x):
    return x * jax.nn.sigmoid(x)


def _rms_rows(x):
    return x * lax.rsqrt(jnp.mean(x * x, axis=-1, keepdims=True) + EPS)


def _mod_kernel(s_ref, w_ref, b_ref, o_ref):
    w = w_ref[0].astype(BF16)
    o_ref[0] = jnp.dot(s_ref[...], w, preferred_element_type=F32) + b_ref[0]


MOD_COLS = 1024


def _mod_call(s_rows, w_mod, b_mod):
    tn = MOD_COLS
    n = w_mod.shape[-1]
    return pl.pallas_call(
        _mod_kernel,
        grid=(DEPTH, n // tn),
        in_specs=[
            pl.BlockSpec((MOD_ROWS, D_MODEL), lambda l, j: (0, 0)),
            pl.BlockSpec((1, D_MODEL, tn), lambda l, j: (l, 0, j)),
            pl.BlockSpec((1, 1, tn), lambda l, j: (l, 0, j)),
        ],
        out_specs=pl.BlockSpec((1, MOD_ROWS, tn), lambda l, j: (l, 0, j)),
        out_shape=jax.ShapeDtypeStruct((DEPTH, MOD_ROWS, n), F32),
        compiler_params=_cparams(("arbitrary", "arbitrary")),
        name="mod_matmul",
    )(s_rows, w_mod, b_mod.reshape(DEPTH, 1, n))


def _prenorm(x, mod_ref, nw_ref, shift_row):
    y = _rms_rows(x) * nw_ref[...]
    h = y * (1.0 + mod_ref[0, shift_row + 1:shift_row + 2, :]) + mod_ref[0, shift_row:shift_row + 1, :]
    return h.astype(BF16)


def _prenorm_kernel(x_ref, mod_ref, nw_ref, h_ref, *, shift_row):
    h_ref[...] = _prenorm(x_ref[...], mod_ref, nw_ref, shift_row)


def _prenorm_call(x2, mod3, nw, tm, mod_row, shift_row, name):
    m = x2.shape[0]
    return pl.pallas_call(
        functools.partial(_prenorm_kernel, shift_row=shift_row),
        grid=(m // tm,),
        in_specs=[
            pl.BlockSpec((tm, D_MODEL), lambda i: (i, 0)),
            pl.BlockSpec((1, MOD_ROWS, D_MODEL), lambda i: (mod_row(i), 0, 0)),
            pl.BlockSpec((1, D_MODEL), lambda i: (0, 0)),
        ],
        out_specs=pl.BlockSpec((tm, D_MODEL), lambda i: (i, 0)),
        out_shape=jax.ShapeDtypeStruct((m, D_MODEL), BF16),
        compiler_params=_cparams(("arbitrary",)),
        name=name,
    )(x2, mod3, nw)


def _proj_kernel(h_ref, w_ref, o_ref, wb_ref, *, act):
    @pl.when(pl.program_id(1) == 0)
    def _():
        wb_ref[...] = w_ref[...].astype(BF16)

    acc = jnp.dot(h_ref[...], wb_ref[...], preferred_element_type=F32)
    if act:
        acc = _silu(acc)
    o_ref[...] = acc.astype(o_ref.dtype)


def _proj_call(h2, w, layer, col0, n_cols, tm, tn, act, out_dtype, name):
    m = h2.shape[0]
    c0 = col0 // tn
    return pl.pallas_call(
        functools.partial(_proj_kernel, act=act),
        grid=(n_cols // tn, m // tm),
        in_specs=[
            pl.BlockSpec((tm, D_MODEL), lambda j, i: (i, 0)),
            pl.BlockSpec((None, D_MODEL, tn), lambda j, i: (layer, 0, c0 + j)),
        ],
        out_specs=pl.BlockSpec((tm, tn), lambda j, i: (i, j)),
        out_shape=jax.ShapeDtypeStruct((m, n_cols), out_dtype),
        scratch_shapes=[pltpu.VMEM((D_MODEL, tn), BF16)],
        compiler_params=_cparams(("arbitrary", "arbitrary")),
        name=name,
    )(h2, w)


def _scan_constants():
    c = CHUNK
    t = np.arange(c)[:, None]
    r = np.arange(c)[None, :]
    level = np.full((2, c, c), -1, np.int32)
    for lvl in range(N_LEVELS):
        half = c >> (lvl + 1)
        same = (t // (2 * half)) == (r // (2 * half))
        t_late = (t % (2 * half)) >= half
        r_late = (r % (2 * half)) >= half
        level[0][same & t_late & ~r_late] = lvl
        level[1][same & ~t_late & r_late] = lvl
    level[:, np.arange(c), np.arange(c)] = N_LEVELS
    return level


_NT = (((1,), (1,)), ((), ()))
_TN = (((0,), (0,)), ((), ()))


def _neg_abs(x):
    bits = lax.bitcast_convert_type(x, jnp.int32) | jnp.int32(-2 ** 31)
    return lax.bitcast_convert_type(bits, F32)


def _scan_kernel(lbl_ref, level_ref, qf_ref, qb_ref, vf_ref, vb_ref, ff_ref, fb_ref,
                 s0_ref, of_ref, ob_ref, sfin_ref, st_ref, *, layer, n_chunks):
    c = CHUNK
    step = pl.program_id(2)

    @pl.when(step == 0)
    def _():
        st_ref[...] = s0_ref[0, 0]

    rows = [lbl_ref[:, j, :] for j in range(DEPTH)]
    mx = functools.reduce(jnp.maximum, rows)
    es = [jnp.exp(rw - mx) for rw in rows]
    tot = functools.reduce(lambda a, b: a + b, es)
    lb = jnp.zeros_like(mx)
    for j in range(1, layer + 1):
        lb = lb + es[j] / tot

    level = level_ref[...]
    sub = lax.broadcasted_iota(jnp.int32, (c // 8, 8, HEAD_DIM), 1)

    def ref_rows(b, half, d):
        pick = half - 1 if d == 0 else half
        if half >= 8:
            parts = []
            for p0 in range(0, c, 2 * half):
                parts.append(jnp.broadcast_to(b[p0 + pick:p0 + pick + 1, :], (2 * half, HEAD_DIM)))
            return parts[0] if len(parts) == 1 else jnp.concatenate(parts, axis=0)
        b3 = b.reshape(c // 8, 8, HEAD_DIM)
        out = None
        for g0 in range(0, 8, 2 * half):
            cand = jnp.broadcast_to(b3[:, g0 + pick:g0 + pick + 1, :], b3.shape)
            out = cand if out is None else jnp.where(sub >= g0, cand, out)
        return out.reshape(c, HEAD_DIM)

    refs = ((qf_ref, vf_ref, ff_ref, of_ref), (qb_ref, vb_ref, fb_ref, ob_ref))

    def gates(d, row0):
        raw = refs[d][2][0, row0:row0 + c, :]
        lbd = lb[d:d + 1, :]
        f = lbd + (1.0 - lbd) * jax.nn.sigmoid(raw)
        x = (jnp.log(f) * LOG2_E).reshape(c // 8, 8, HEAD_DIM)
        for s in (1, 2, 4):
            if d == 0:
                x = x + jnp.where(sub >= s, pltpu.roll(x, s, axis=1), 0.0)
            else:
                x = x + jnp.where(sub < 8 - s, pltpu.roll(x, 8 - s, axis=1), 0.0)
        groups = [x[g] for g in range(c // 8)]
        edge = 7 if d == 0 else 0
        totals = [grp[edge:edge + 1, :] for grp in groups]
        order = range(1, c // 8) if d == 0 else range(c // 8 - 2, -1, -1)
        carry = None
        for g in order:
            before = totals[g - 1] if d == 0 else totals[g + 1]
            carry = before if carry is None else carry + before
            groups[g] = groups[g] + carry
        return f, jnp.concatenate(groups, axis=0)

    none = jnp.full((c, c), -1, jnp.int32)
    level2 = jnp.concatenate([jnp.concatenate([level[0], none], axis=1),
                              jnp.concatenate([none, level[1]], axis=1)], axis=0)
    pad = jnp.zeros((c, HEAD_DIM), BF16)

    def diag2(fwd, bwd, fill):
        return jnp.concatenate([jnp.concatenate([fwd, fill], axis=1),
                                jnp.concatenate([fill, bwd], axis=1)], axis=0)

    def intra(rows, gated):
        q_bf, k_bf, b, own, pair = [], [], [], [], []
        for d in range(2):
            f, bd = gated[d]
            qd = refs[d][0][0, rows[d]:rows[d] + c, :]
            q = qd.astype(F32)
            key = 1.0 - f
            own.append(jnp.sum(q * key, axis=-1, keepdims=True))
            pair_key = pltpu.roll(key, 1 if d == 0 else c - 1, axis=0)
            pair.append(jnp.sum(q * f * pair_key, axis=-1, keepdims=True))
            q_bf.append(qd)
            k_bf.append(key.astype(BF16))
            b.append(bd)
        rows_of = lambda fwd, bwd: jnp.concatenate([jnp.broadcast_to(fwd, (c, 2 * c)),
                                                    jnp.broadcast_to(bwd, (c, 2 * c))], axis=0)
        att = jnp.where(level2 == N_LEVELS, rows_of(own[0], own[1]), 0.0)
        att = jnp.where(level2 == N_LEVELS - 1, rows_of(pair[0], pair[1]), att)
        for lvl in range(N_LEVELS - 1):
            half = c >> (lvl + 1)
            el = [jnp.exp2(_neg_abs(b[d] - ref_rows(b[d], half, d))).astype(BF16) for d in range(2)]
            lhs = diag2(q_bf[0] * el[0], q_bf[1] * el[1], pad)
            rhs = diag2(k_bf[0] * el[0], k_bf[1] * el[1], pad)
            att = jnp.where(level2 == lvl, lax.dot_general(lhs, rhs, _NT, preferred_element_type=F32), att)
        q_in, k_out, decay = [], [], []
        for d in range(2):
            last = c - 1 if d == 0 else 0
            blast = b[d][last:last + 1, :]
            q_in.append(q_bf[d] * jnp.exp2(b[d]).astype(BF16))
            k_out.append(k_bf[d] * jnp.exp2(blast - b[d]).astype(BF16))
            decay.append(jnp.transpose(jnp.broadcast_to(jnp.exp2(blast), (8, HEAD_DIM)))[:, 0:1])
        q_att = jnp.concatenate([diag2(q_in[0], q_in[1], pad), att.astype(BF16)], axis=1)
        return q_att, diag2(k_out[0], k_out[1], pad), jnp.concatenate(decay, axis=0)

    def readout(rows, q_att, k_out, decay, st):
        v = jnp.concatenate([refs[d][1][0, rows[d]:rows[d] + c, :] for d in range(2)], axis=0)
        o = jnp.dot(q_att, jnp.concatenate([st.astype(BF16), v], axis=0), preferred_element_type=F32)
        for d in range(2):
            refs[d][3][0, rows[d]:rows[d] + c, :] = o[d * c:(d + 1) * c]
        return st * decay + lax.dot_general(k_out, v, _TN, preferred_element_type=F32)

    work = [(ci * c, (n_chunks - 1 - ci) * c) for ci in range(n_chunks)]
    st = st_ref[...].reshape(2 * HEAD_DIM, HEAD_DIM)
    stage1 = {}
    stage2 = {}
    for n in range(len(work) + 2):
        if n >= 2:
            st = readout(work[n - 2], *stage2.pop(n - 2), st)
        if 1 <= n <= len(work):
            stage2[n - 1] = intra(work[n - 1], stage1.pop(n - 1))
        if n < len(work):
            stage1[n] = [gates(d, work[n][d]) for d in range(2)]
    st_ref[...] = st.reshape(2, HEAD_DIM, HEAD_DIM)

    @pl.when(step == pl.num_programs(2) - 1)
    def _():
        sfin_ref[0, 0] = st_ref[...]


def _scan_call(q3, f3, r3, lb_logits, s0, level, layer, tl):
    bsz, length, _ = q3.shape
    nl = length // tl
    fwd = lambda b, h, i: (b, i, h)
    bwd = lambda b, h, i: (b, nl - 1 - i, h)
    bwd_f = lambda b, h, i: (b, nl - 1 - i, HEADS + h)
    blk = (1, tl, HEAD_DIM)
    return pl.pallas_call(
        functools.partial(_scan_kernel, layer=layer, n_chunks=tl // CHUNK),
        grid=(bsz, HEADS, nl),
        in_specs=[
            pl.BlockSpec((2, DEPTH, HEAD_DIM), lambda b, h, i: (0, 0, h)),
            pl.BlockSpec(level.shape, lambda b, h, i: (0, 0, 0)),
            pl.BlockSpec(blk, fwd),
            pl.BlockSpec(blk, bwd),
            pl.BlockSpec(blk, fwd),
            pl.BlockSpec(blk, bwd),
            pl.BlockSpec(blk, fwd),
            pl.BlockSpec(blk, bwd_f),
            pl.BlockSpec((1, 1, 2, HEAD_DIM, HEAD_DIM), lambda b, h, i: (b, h, 0, 0, 0)),
        ],
        out_specs=[
            pl.BlockSpec(blk, fwd),
            pl.BlockSpec(blk, bwd),
            pl.BlockSpec((1, 1, 2, HEAD_DIM, HEAD_DIM), lambda b, h, i: (b, h, 0, 0, 0)),
        ],
        out_shape=[
            jax.ShapeDtypeStruct((bsz, length, HG_WIDTH), F32),
            jax.ShapeDtypeStruct((bsz, length, HG_WIDTH), F32),
            jax.ShapeDtypeStruct((bsz, HEADS, 2, HEAD_DIM, HEAD_DIM), F32),
        ],
        scratch_shapes=[pltpu.VMEM((2, HEAD_DIM, HEAD_DIM), F32)],
        compiler_params=_cparams(("arbitrary", "arbitrary", "arbitrary")),
        name="hgrn2_scan",
    )(lb_logits, level, q3, q3, r3, r3, f3, f3, s0)


def _cconv_kernel(a_ref, b_ref, cw_ref, cb_ref, lw_ref, lb_ref, o_ref, u_ref, *, length, rows):
    zeros = jnp.zeros((CONV_PAD, HEAD_DIM), F32)
    u_ref[0:CONV_PAD, :] = zeros
    u_ref[CONV_PAD + length:2 * CONV_PAD + length, :] = zeros
    n_tiles = length // rows

    def fill(ti, carry):
        r0 = pl.multiple_of(ti * rows, rows)
        a = a_ref[0, pl.ds(r0, rows), :].astype(F32)
        b = b_ref[0, pl.ds(r0, rows), :].astype(F32)
        u_ref[pl.ds(CONV_PAD + r0, rows), :] = a * jax.nn.sigmoid(b)
        return carry

    lax.fori_loop(0, n_tiles, fill, 0, unroll=4)

    half = CONV_WIDTH // 2

    def conv(ti, carry):
        r0 = pl.multiple_of(ti * rows, rows)
        lead = CONV_PAD - half
        span = -(-(lead + CONV_WIDTH - 1) // 8) * 8 - 8
        acc = jnp.zeros((rows, HEAD_DIM), F32)
        for phase in range(8):
            window = u_ref[pl.ds(r0 + phase, rows + span), :]
            for off in range(phase, lead + CONV_WIDTH, 8):
                if off >= lead:
                    j = off - lead
                    acc = acc + cw_ref[j:j + 1, :] * window[off - phase:off - phase + rows]
        acc = acc + cb_ref[...]
        mu = jnp.mean(acc, axis=-1, keepdims=True)
        dlt = acc - mu
        var = jnp.mean(dlt * dlt, axis=-1, keepdims=True)
        y = dlt * lax.rsqrt(var + LN_EPS) * lw_ref[...] + lb_ref[...]
        o_ref[0, pl.ds(r0, rows), :] = _silu(y).astype(BF16)
        return carry

    lax.fori_loop(0, n_tiles, conv, 0, unroll=8)


def _cconv_call(r3, cw, cb, lw, lb):
    bsz, length, _ = r3.shape
    groups = CONV_CH // HEAD_DIM
    a_col = 2 * HG_WIDTH // HEAD_DIM
    b_col = a_col + groups
    vec = pl.BlockSpec((1, HEAD_DIM), lambda b, g: (0, g))
    return pl.pallas_call(
        functools.partial(_cconv_kernel, length=length, rows=CONV_ROWS),
        grid=(bsz, groups),
        in_specs=[
            pl.BlockSpec((1, length, HEAD_DIM), lambda b, g: (b, 0, a_col + g)),
            pl.BlockSpec((1, length, HEAD_DIM), lambda b, g: (b, 0, b_col + g)),
            pl.BlockSpec((CONV_WIDTH, HEAD_DIM), lambda b, g: (0, g)),
            vec, vec, vec,
        ],
        out_specs=pl.BlockSpec((1, length, HEAD_DIM), lambda b, g: (b, 0, g)),
        out_shape=jax.ShapeDtypeStruct((bsz, length, CONV_CH), BF16),
        scratch_shapes=[pltpu.VMEM((length + 2 * CONV_PAD, HEAD_DIM), F32)],
        compiler_params=_cparams(("arbitrary", "arbitrary")),
        name="conformer_conv",
    )(r3, r3, cw, cb.reshape(1, -1), lw.reshape(1, -1), lb.reshape(1, -1))


def _outproj_kernel(of_ref, ob_ref, g_ref, cv_ref, x_ref, mod_ref, hgw_ref, nw_ref, nw2_ref, w_ref, o_ref,
                    h_ref):
    o = of_ref[...] + ob_ref[...]
    heads = []
    for h in range(HEADS):
        oh = o[:, h * HEAD_DIM:(h + 1) * HEAD_DIM]
        heads.append(_rms_rows(oh) * hgw_ref[...])
    r = jnp.concatenate(heads, axis=-1) * _silu(g_ref[...].astype(F32))
    y = jnp.dot(r.astype(BF16), w_ref[0:HG_WIDTH, :], preferred_element_type=F32)
    y = y + jnp.dot(cv_ref[...], w_ref[HG_WIDTH:D_MODEL, :], preferred_element_type=F32)
    x_new = x_ref[...] + mod_ref[0, 2:3, :] * (_rms_rows(y) * nw_ref[...])
    o_ref[...] = x_new
    h_ref[...] = _prenorm(x_new, mod_ref, nw2_ref, 3)


def _outproj_call(of2, ob2, r2, cv2, x2, mod3, hgw, nw, nw2, w_bf, layer, tm, mod_row):
    m = x2.shape[0]
    return pl.pallas_call(
        _outproj_kernel,
        grid=(m // tm,),
        in_specs=[
            pl.BlockSpec((tm, HG_WIDTH), lambda i: (i, 0)),
            pl.BlockSpec((tm, HG_WIDTH), lambda i: (i, 0)),
            pl.BlockSpec((tm, HG_WIDTH), lambda i: (i, 1)),
            pl.BlockSpec((tm, CONV_CH), lambda i: (i, 0)),
            pl.BlockSpec((tm, D_MODEL), lambda i: (i, 0)),
            pl.BlockSpec((1, MOD_ROWS, D_MODEL), lambda i: (mod_row(i), 0, 0)),
            pl.BlockSpec((1, HEAD_DIM), lambda i: (0, 0)),
            pl.BlockSpec((1, D_MODEL), lambda i: (0, 0)),
            pl.BlockSpec((1, D_MODEL), lambda i: (0, 0)),
            pl.BlockSpec((None, D_MODEL, D_MODEL), lambda i: (layer, 0, 0), pipeline_mode=pl.Buffered(1)),
        ],
        out_specs=[
            pl.BlockSpec((tm, D_MODEL), lambda i: (i, 0)),
            pl.BlockSpec((tm, D_MODEL), lambda i: (i, 0)),
        ],
        out_shape=[
            jax.ShapeDtypeStruct((m, D_MODEL), F32),
            jax.ShapeDtypeStruct((m, D_MODEL), BF16),
        ],
        compiler_params=_cparams(("arbitrary",)),
        name="out_proj",
    )(of2, ob2, r2, cv2, x2, mod3, hgw, nw, nw2, w_bf)


ROW_BLOCK = 32
MXU_COLS = 256
MXU_ROWS = 256
FFN_CHUNK = 512
LANE_BLOCK = 256


def _gate_chunk(gp_ref, gm_ref, gn_ref, val_ref, cw_ref, cb_ref, act_out, first, last, mm_tile,
                mm_tiles, *, tm, grid_mode):
    tk = FFN_CHUNK
    rows = ROW_BLOCK
    lanes = LANE_BLOCK
    n_blocks = tm // rows
    total = n_blocks * (tk // lanes)
    row_id = lax.broadcasted_iota(jnp.int32, (rows, lanes), 0)
    edge = GRID_W if grid_mode else tm
    zero = jnp.zeros((), BF16)

    def gate_rows(r0, ls):
        if r0 < 0:
            return jnp.where(first, zero, gp_ref[GRID_W + r0:GRID_W + r0 + rows, ls])
        if r0 >= tm:
            return jnp.where(last, zero, gn_ref[r0 - tm:r0 - tm + rows, ls])
        return gm_ref[r0:r0 + rows, ls]

    done = 0
    for l0 in range(0, tk, lanes):
        ls = slice(l0, l0 + lanes)
        taps = (0.5 * cw_ref[:, :, ls]).astype(BF16)
        taps = [[jnp.broadcast_to(taps[dh, dw:dw + 1], (rows, lanes)) for dw in range(3)] for dh in range(3)]
        bias = 0.5 * cb_ref[:, ls]

        def columns(r0):
            ce = gate_rows(r0, ls)
            if not grid_mode:
                return [(taps[1][dw] * ce).astype(F32) for dw in range(3)]
            up = gate_rows(r0 - GRID_W, ls)
            dn = gate_rows(r0 + GRID_W, ls)
            return [(taps[0][dw] * up + taps[1][dw] * ce + taps[2][dw] * dn).astype(F32) for dw in range(3)]

        cur = columns(0)
        prev_left_row = None
        for blk in range(n_blocks):
            r0 = blk * rows
            for n in range(done * mm_tiles // total, (done + 1) * mm_tiles // total):
                mm_tile(n)
            done += 1
            nxt = columns(r0 + rows) if blk + 1 < n_blocks else None
            left, mid, right = cur
            from_prev = pltpu.roll(left, 1, axis=0)
            if r0 % edge == 0:
                from_prev = jnp.where(row_id == 0, 0.0, from_prev)
            else:
                from_prev = jnp.where(row_id == 0, prev_left_row, from_prev)
            from_next = pltpu.roll(right, rows - 1, axis=0)
            if (r0 + rows) % edge == 0:
                from_next = jnp.where(row_id == rows - 1, 0.0, from_next)
            else:
                from_next = jnp.where(row_id == rows - 1, nxt[2][0:1, :], from_next)
            half = mid + from_prev + from_next + bias
            gelu = half * (1.0 + lax.erf(half * np.float32(np.sqrt(2.0))))
            act_out[r0:r0 + rows, ls] = gelu.astype(BF16) * val_ref[r0:r0 + rows, ls]
            prev_left_row = left[rows - 1:rows, :]
            cur = nxt


def _ffn_down_kernel(gp_ref, gm_ref, gn_ref, val_ref, cw_ref, cb_ref, w_ref, x_ref, mod_ref, nw_ref, *rest,
                     tm, nk, tiles_per_seq, grid_mode, next_norm):
    if next_norm:
        modn_ref, nwn_ref, o_ref, h_ref, acc_ref, act0_ref, act1_ref = rest
    else:
        o_ref, acc_ref, act0_ref, act1_ref = rest
    s = pl.program_id(0)
    i = jnp.minimum(s, pl.num_programs(0) - 2) // nk
    km = jnp.maximum(s - 1, 0) % nk
    first = (i % tiles_per_seq) == 0
    last = (i % tiles_per_seq) == tiles_per_seq - 1

    @pl.when(s == 0)
    def _():
        act1_ref[...] = jnp.zeros_like(act1_ref)

    @pl.when(km == 0)
    def _():
        acc_ref[...] = jnp.zeros_like(acc_ref)

    def step(act_in, act_out):
        row_parts = tm // MXU_ROWS

        def mm_tile(n):
            rows = slice((n % row_parts) * MXU_ROWS, (n % row_parts + 1) * MXU_ROWS)
            cols = slice((n // row_parts) * MXU_COLS, (n // row_parts + 1) * MXU_COLS)
            acc_ref[rows, cols] += jnp.dot(act_in[rows, :], w_ref[:, cols], preferred_element_type=F32)

        _gate_chunk(gp_ref, gm_ref, gn_ref, val_ref, cw_ref, cb_ref, act_out, first, last, mm_tile,
                    row_parts * D_MODEL // MXU_COLS, tm=tm, grid_mode=grid_mode)

    @pl.when(s % 2 == 0)
    def _():
        step(act1_ref, act0_ref)

    @pl.when(s % 2 == 1)
    def _():
        step(act0_ref, act1_ref)

    @pl.when((km == nk - 1) & (s > 0))
    def _():
        x_new = x_ref[...] + mod_ref[0, 5:6, :] * (_rms_rows(acc_ref[...]) * nw_ref[...])
        o_ref[...] = x_new
        if next_norm:
            h_ref[...] = _prenorm(x_new, modn_ref, nwn_ref, 0)


def _ffn_down_call(gv2, cw, cb, w_bf, layer, x2, mod3, nw, tm, seq_len, mod_row, grid_mode, next_norm):
    m = x2.shape[0]
    tk = FFN_CHUNK
    nk = D_FF // tk
    hb = tm // GRID_W
    n_hblk = m // GRID_W
    tiles_per_seq = seq_len // tm
    n_steps = (m // tm) * nk + 1
    bi = lambda s: jnp.minimum(s, n_steps - 2) // nk
    bk = lambda s: jnp.minimum(s, n_steps - 2) % nk
    mi = lambda s: jnp.maximum(s - 1, 0) // nk
    mk = lambda s: jnp.maximum(s - 1, 0) % nk
    row_spec = pl.BlockSpec((tm, D_MODEL), lambda s: (mi(s), 0))
    mod_spec = pl.BlockSpec((1, MOD_ROWS, D_MODEL), lambda s: (mod_row(mi(s)), 0, 0))
    vec_spec = pl.BlockSpec((1, D_MODEL), lambda s: (0, 0))
    extra_in, extra_specs, out_specs = (), [], row_spec
    out_shape = jax.ShapeDtypeStruct((m, D_MODEL), F32)
    if next_norm:
        extra_in, extra_specs = tuple(next_norm), [mod_spec, vec_spec]
        out_specs = [row_spec, row_spec]
        out_shape = [out_shape, jax.ShapeDtypeStruct((m, D_MODEL), BF16)]
    return pl.pallas_call(
        functools.partial(_ffn_down_kernel, tm=tm, nk=nk, tiles_per_seq=tiles_per_seq, grid_mode=grid_mode,
                          next_norm=bool(next_norm)),
        grid=(n_steps,),
        in_specs=[
            pl.BlockSpec((GRID_W, tk), lambda s: (jnp.maximum(bi(s) * hb - 1, 0), bk(s))),
            pl.BlockSpec((tm, tk), lambda s: (bi(s), bk(s))),
            pl.BlockSpec((GRID_W, tk), lambda s: (jnp.minimum((bi(s) + 1) * hb, n_hblk - 1), bk(s))),
            pl.BlockSpec((tm, tk), lambda s: (bi(s), nk + bk(s))),
            pl.BlockSpec((3, 3, tk), lambda s: (0, 0, bk(s))),
            pl.BlockSpec((1, tk), lambda s: (0, bk(s))),
            pl.BlockSpec((None, tk, D_MODEL), lambda s: (layer, mk(s), 0)),
            row_spec,
            mod_spec,
            vec_spec,
        ] + extra_specs,
        out_specs=out_specs,
        out_shape=out_shape,
        scratch_shapes=[
            pltpu.VMEM((tm, D_MODEL), F32),
            pltpu.VMEM((tm, tk), BF16),
            pltpu.VMEM((tm, tk), BF16),
        ],
        compiler_params=_cparams(("arbitrary",)),
        name="ffn_down",
    )(gv2, gv2, gv2, gv2, cw, cb.reshape(1, -1), w_bf, x2, mod3, nw, *extra_in)


class _Tiles(NamedTuple):
    proj: int
    proj_f32: int
    norm: int
    out: int
    down: int
    scan: int
    cols: int


def _stream(x3, h2, mod, l, params, s0, consts, tiles, row_of, grid_mode, full, emit_next):
    bsz, length, _ = x3.shape
    m = bsz * length
    x2 = x3.reshape(m, D_MODEL)
    mod3 = mod[l]
    nw = params["norm_w"][l]
    tn = tiles.cols
    w_in = params["w_in"]
    if h2 is None:
        h2 = _prenorm_call(x2, mod3, nw[0:1], tiles.norm, row_of(tiles.norm), 0, "mix_prenorm")
    f2 = _proj_call(h2, w_in, l, HG_WIDTH, 2 * HG_WIDTH, tiles.proj_f32, tn, False, F32, "in_proj_f")
    if full:
        q2 = _proj_call(h2, w_in, l, 0, HG_WIDTH, tiles.proj, tn, True, BF16, "in_proj_q")
        r2 = _proj_call(h2, w_in, l, 3 * HG_WIDTH, IN_COLS - 3 * HG_WIDTH, tiles.proj, tn, False, BF16,
                        "in_proj_vgab")
    else:
        r2 = _proj_call(h2, w_in, l, 3 * HG_WIDTH, HG_WIDTH, tiles.proj, tn, False, BF16, "in_proj_v")
        q2 = r2
    of3, ob3, sfin = _scan_call(q2.reshape(bsz, length, -1), f2.reshape(bsz, length, -1),
                                r2.reshape(bsz, length, -1), params["lb_logits"], s0, consts, l,
                                tiles.scan)
    if not full:
        return None, None, sfin
    cv3 = _cconv_call(r2.reshape(bsz, length, -1), params["conv_w"][l], params["conv_b"][l],
                      params["conv_ln_w"][l], params["conv_ln_b"][l])
    x2, h2 = _outproj_call(of3.reshape(m, -1), ob3.reshape(m, -1), r2, cv3.reshape(m, -1), x2, mod3,
                           params["hg_norm_w"][l].reshape(1, -1), nw[1:2], nw[2:3], params["w_out"], l,
                           tiles.out, row_of(tiles.out))
    gv2 = _proj_call(h2, params["ffn_up"], l, 0, 2 * D_FF, tiles.proj, tn, False, BF16, "ffn_up")
    next_norm = (mod[l + 1], params["norm_w"][l + 1][0:1]) if emit_next else None
    out = _ffn_down_call(gv2, params["ffn_conv_w"][l], params["ffn_conv_b"][l], params["ffn_down"], l,
                         x2, mod3, nw[3:4], tiles.down, length, row_of(tiles.down), grid_mode, next_norm)
    x2, h_next = out if emit_next else (out, None)
    return x2.reshape(bsz, length, D_MODEL), h_next, sfin


def kernel(x, c, ctx, c_ctx, w_mod, b_mod, norm_w, w_in, lb_logits, hg_norm_w, conv_w, conv_b, conv_ln_w,
           conv_ln_b, w_out, ffn_up, ffn_conv_w, ffn_conv_b, ffn_down):
    bsz, seq, _ = x.shape
    ctx_len = ctx.shape[1]
    params = dict(norm_w=norm_w, w_in=w_in, lb_logits=lb_logits, hg_norm_w=hg_norm_w,
                  conv_w=conv_w, conv_b=conv_b, conv_ln_w=conv_ln_w, conv_ln_b=conv_ln_b,
                  w_out=w_out.astype(BF16), ffn_up=ffn_up, ffn_conv_w=ffn_conv_w,
                  ffn_conv_b=ffn_conv_b, ffn_down=ffn_down.astype(BF16))
    consts = jnp.asarray(_scan_constants(), jnp.int32)

    cond = jnp.concatenate([c, c_ctx[None, :], jnp.zeros((MOD_ROWS - bsz - 1, D_MODEL), c.dtype)], axis=0)
    mod = _mod_call(_silu(cond).astype(BF16), w_mod, b_mod)
    mod = mod[:, :bsz + 1].reshape(DEPTH, bsz + 1, N_MOD, D_MODEL)
    mod = jnp.pad(mod, ((0, 0), (0, 0), (0, MOD_ROWS - N_MOD), (0, 0)))

    lat_row = lambda tm: (lambda i: i // (seq // tm))
    ctx_row = lambda tm: (lambda i: bsz)
    zero_state = jnp.zeros((bsz, HEADS, 2, HEAD_DIM, HEAD_DIM), F32)
    ctx_rows = bsz * ctx_len
    ctx_tiles = _Tiles(proj=ctx_rows, proj_f32=ctx_rows, norm=ctx_len, out=ctx_len, down=ctx_len,
                       scan=ctx_len, cols=1024)
    lat_tiles = _Tiles(proj=2048, proj_f32=1024, norm=512, out=512, down=512, scan=4096, cols=1024)
    xc, hc, h = ctx, None, None
    for l in range(DEPTH):
        last = l == DEPTH - 1
        xc, hc, s_ctx = _stream(xc, hc, mod, l, params, zero_state, consts, ctx_tiles, ctx_row,
                                grid_mode=False, full=not last, emit_next=not last)
        x, h, _ = _stream(x, h, mod, l, params, s_ctx, consts, lat_tiles, lat_row,
                          grid_mode=True, full=True, emit_next=not last)
    return x
```
